```python
import math
import jax, jax.numpy as jnp
from jax import lax
import numpy as np

D_MODEL = 1024
BATCH = 32
SEQ = 2048
DEPTH = 1
DEC_BATCH = 2
DEC_SEQ = 16384
PAST_LEN = 128

GRID_W = 64
HEAD_DIM = 128
N_ATTN_HEADS = 4
N_KV_HEADS = 2
N_DN_HEADS = 4
DN_KDIM = 128
DN_VDIM = 128
DN_CONV_K = 5
CHUNK = 64
Q_BLOCK = 128
ROPE_THETA = 10000.0
D_FF = 2752
FFN_CONV_K = 3
EPS = 1e-6

ATTN_Q_W = N_ATTN_HEADS * HEAD_DIM
ATTN_KV_W = N_KV_HEADS * HEAD_DIM
DN_QK_W = N_DN_HEADS * DN_KDIM
DN_V_W = N_DN_HEADS * DN_VDIM
MIX_W = ATTN_Q_W + DN_V_W
IN_W = ATTN_Q_W + 2 * ATTN_KV_W + 2 * DN_QK_W + 2 * DN_V_W + 4 * N_DN_HEADS

kernel_name = "hymba_deltanet_gqa_axialrope_convffn_encoder"


def rms_norm(x, w):
    xf = x.astype(jnp.float32)
    y = xf * lax.rsqrt(jnp.mean(xf * xf, axis=-1, keepdims=True) + EPS)
    return (y * w.astype(jnp.float32)).astype(x.dtype)


def l2_norm(x):
    return x * lax.rsqrt(jnp.sum(x * x, axis=-1, keepdims=True) + EPS)


def centred_dwconv(x, w):
    K = w.shape[0]
    T = x.shape[1]
    pad = K // 2
    xp = jnp.pad(x, ((0, 0), (pad, pad), (0, 0)))
    out = xp[:, 0:T] * w[0]
    for j in range(1, K):
        out = out + xp[:, j:j + T] * w[j]
    return out


def _rope_1d(x, pos):
    d = x.shape[-1]
    inv = ROPE_THETA ** (-jnp.arange(0, d, 2, dtype=jnp.float32) / d)
    ang = pos.astype(jnp.float32)[:, None] * inv[None, :]
    cos, sin = jnp.cos(ang), jnp.sin(ang)
    x1, x2 = x[..., : d // 2], x[..., d // 2:]
    return jnp.concatenate([x1 * cos - x2 * sin, x2 * cos + x1 * sin], axis=-1)


def axial_rope(x):
    T = x.shape[2]
    rows_count = T // GRID_W
    rows = jnp.repeat(jnp.arange(rows_count, dtype=jnp.int32), GRID_W)
    cols = jnp.tile(jnp.arange(GRID_W, dtype=jnp.int32), rows_count)
    xf = x.astype(jnp.float32)
    h = x.shape[-1] // 2
    out = jnp.concatenate([_rope_1d(xf[..., :h], rows), _rope_1d(xf[..., h:], cols)], axis=-1)
    return out.astype(x.dtype)


def gqa_attention(q, k, v, q_norm_w, k_norm_w, out_norm_w):
    B, T, _ = q.shape
    G = N_ATTN_HEADS // N_KV_HEADS
    nb = T // Q_BLOCK
    q = rms_norm(q.reshape(B, T, N_ATTN_HEADS, HEAD_DIM), q_norm_w).transpose(0, 2, 1, 3)
    k = rms_norm(k.reshape(B, T, N_KV_HEADS, HEAD_DIM), k_norm_w).transpose(0, 2, 1, 3)
    v = v.reshape(B, T, N_KV_HEADS, HEAD_DIM).transpose(0, 2, 1, 3)
    q = axial_rope(q)
    k = axial_rope(k)
    qb = q.reshape(B, N_KV_HEADS, G, nb, Q_BLOCK, HEAD_DIM).transpose(3, 0, 1, 2, 4, 5)
    scale = HEAD_DIM ** -0.5

    def block(qblk):
        s = jnp.einsum('bkgqd,bksd->bkgqs', qblk, k, preferred_element_type=jnp.float32) * scale
        p = jax.nn.softmax(s, axis=-1)
        return jnp.einsum('bkgqs,bksd->bkgqd', p.astype(v.dtype), v)

    o = lax.map(block, qb)
    o = o.transpose(1, 0, 4, 2, 3, 5).reshape(B, T, N_ATTN_HEADS, HEAD_DIM)
    o = rms_norm(o, out_norm_w)
    return o.reshape(B, T, ATTN_Q_W)


def gated_delta_chunked(q, k, v, g, beta):
    B, H, T, dk = q.shape
    dv = v.shape[-1]
    N = T // CHUNK
    q = q.reshape(B, H, N, CHUNK, dk)
    k = k.reshape(B, H, N, CHUNK, dk)
    v = v.reshape(B, H, N, CHUNK, dv)
    beta = beta.reshape(B, H, N, CHUNK)
    g = jnp.cumsum(g.reshape(B, H, N, CHUNK), axis=-1)
    tril = jnp.tril(jnp.ones((CHUNK, CHUNK), dtype=bool))
    tril_strict = jnp.tril(jnp.ones((CHUNK, CHUNK), dtype=bool), -1)
    diff = g[..., :, None] - g[..., None, :]
    decay = jnp.where(tril, jnp.exp(jnp.minimum(diff, 0.0)), 0.0)
    kb = k * beta[..., None]
    vb = v * beta[..., None]
    L = jnp.where(tril_strict, jnp.einsum('bhncd,bhnsd->bhncs', kb, k) * decay, 0.0)
    a = jnp.eye(CHUNK, dtype=jnp.float32) + L
    rhs = jnp.concatenate([vb, kb * jnp.exp(g)[..., None]], axis=-1)
    sol = lax.linalg.triangular_solve(a, rhs, left_side=True, lower=True, unit_diagonal=True)
    u = sol[..., :dv]
    w = sol[..., dv:]
    attn = jnp.einsum('bhncd,bhnsd->bhncs', q, k) * decay
    qg = q * jnp.exp(g)[..., None]
    kdec = k * jnp.exp(g[..., -1:] - g)[..., None]
    glast = jnp.exp(g[..., -1])

    def step(S, xs):
        u_c, w_c, attn_c, qg_c, kdec_c, gl_c = xs
        v_new = u_c - jnp.einsum('bhcd,bhde->bhce', w_c, S)
        o_c = jnp.einsum('bhcd,bhde->bhce', qg_c, S) + jnp.einsum('bhcs,bhse->bhce', attn_c, v_new)
        S = S * gl_c[..., None, None] + jnp.einsum('bhcd,bhce->bhde', kdec_c, v_new)
        return S, o_c

    xs = (jnp.moveaxis(u, 2, 0), jnp.moveaxis(w, 2, 0), jnp.moveaxis(attn, 2, 0),
          jnp.moveaxis(qg, 2, 0), jnp.moveaxis(kdec, 2, 0), jnp.moveaxis(glast, 2, 0))
    S0 = jnp.zeros((B, H, dk, dv), jnp.float32)
    _, o = lax.scan(step, S0, xs)
    return jnp.moveaxis(o, 0, 2).reshape(B, H, T, dv)


def deltanet_mixer(q, k, v, z, a_f, a_b, b_f, b_b, conv_w, A_log_f, A_log_b, dt_bias_f, dt_bias_b, norm_w):
    B, T, _ = q.shape
    H = N_DN_HEADS
    qkv = jax.nn.silu(centred_dwconv(jnp.concatenate([q, k, v], axis=-1), conv_w))
    qkv = qkv.astype(jnp.float32)
    qh = qkv[..., :DN_QK_W].reshape(B, T, H, DN_KDIM).transpose(0, 2, 1, 3)
    kh = qkv[..., DN_QK_W:2 * DN_QK_W].reshape(B, T, H, DN_KDIM).transpose(0, 2, 1, 3)
    vh = qkv[..., 2 * DN_QK_W:].reshape(B, T, H, DN_VDIM).transpose(0, 2, 1, 3)
    qh = l2_norm(qh) * (DN_KDIM ** -0.5)
    kh = l2_norm(kh)

    def gates(a, b, A_log, dt_bias):
        a = a.astype(jnp.float32)
        g = -jnp.exp(A_log.astype(jnp.float32)) * jax.nn.softplus(a + dt_bias.astype(jnp.float32))
        beta = jax.nn.sigmoid(b.astype(jnp.float32))
        return g.transpose(0, 2, 1), beta.transpose(0, 2, 1)

    g_f, beta_f = gates(a_f, b_f, A_log_f, dt_bias_f)
    g_b, beta_b = gates(a_b, b_b, A_log_b, dt_bias_b)
    o_f = gated_delta_chunked(qh, kh, vh, g_f, beta_f)
    o_b = jnp.flip(gated_delta_chunked(jnp.flip(qh, 2), jnp.flip(kh, 2), jnp.flip(vh, 2),
                                       jnp.flip(g_b, 2), jnp.flip(beta_b, 2)), 2)
    o = (o_f + o_b).transpose(0, 2, 1, 3)
    o = rms_norm(o, norm_w) * jax.nn.silu(z.reshape(B, T, H, DN_VDIM).astype(jnp.float32))
    return o.reshape(B, T, DN_V_W).astype(q.dtype)


def encoder_layer(x, norm1_w, w_in, dn_conv_w, dn_A_log_f, dn_A_log_b, dn_dt_bias_f, dn_dt_bias_b,
                  dn_norm_w, attn_q_norm_w, attn_k_norm_w, attn_out_norm_w, w_out, norm2_w,
                  w_ffn_in, ffn_conv_w, ffn_conv_b, w_ffn_out):
    h = rms_norm(x, norm1_w)
    proj = h @ w_in
    sizes = [ATTN_Q_W, ATTN_KV_W, ATTN_KV_W, DN_QK_W, DN_QK_W, DN_V_W, DN_V_W,
             N_DN_HEADS, N_DN_HEADS, N_DN_HEADS, N_DN_HEADS]
    offsets = np.cumsum(sizes)[:-1].tolist()
    aq, ak, av, dq, dk, dv, dz, a_f, a_b, b_f, b_b = jnp.split(proj, offsets, axis=-1)
    attn_o = gqa_attention(aq, ak, av, attn_q_norm_w, attn_k_norm_w, attn_out_norm_w)
    dn_o = deltanet_mixer(dq, dk, dv, dz, a_f, a_b, b_f, b_b, dn_conv_w,
                          dn_A_log_f, dn_A_log_b, dn_dt_bias_f, dn_dt_bias_b, dn_norm_w)
    mix = jnp.concatenate([attn_o, dn_o.astype(attn_o.dtype)], axis=-1)
    x = x + mix @ w_out
    h = rms_norm(x, norm2_w)
    u = centred_dwconv(h @ w_ffn_in, ffn_conv_w) + ffn_conv_b
    gate, up = u[..., :D_FF], u[..., D_FF:]
    return x + (jax.nn.silu(gate) * up) @ w_ffn_out


def setup_inputs(seed: int = 0) -> dict:
    key = jax.random.key(seed)
    ks = jax.random.split(key, 24)
    f32 = jnp.float32
    L = DEPTH

    def nrm(k, shape, scale):
        return jax.random.normal(k, shape, f32) * scale

    def inv_softplus_dt(k):
        dt = jnp.exp(jax.random.uniform(k, (L, N_DN_HEADS), f32, math.log(1e-3), math.log(1e-1)))
        return dt + jnp.log(-jnp.expm1(-dt))

    return {
        "x_prompt": nrm(ks[0], (BATCH, SEQ, D_MODEL), 1.0),
        "x_sample": nrm(ks[1], (DEC_BATCH, DEC_SEQ, D_MODEL), 1.0),
        "norm1_w": 1.0 + nrm(ks[2], (L, D_MODEL), 0.02),
        "w_in": nrm(ks[3], (L, D_MODEL, IN_W), D_MODEL ** -0.5),
        "dn_conv_w": nrm(ks[4], (L, DN_CONV_K, 2 * DN_QK_W + DN_V_W), DN_CONV_K ** -0.5),
        "dn_A_log_f": jnp.log(jax.random.uniform(ks[5], (L, N_DN_HEADS), f32, 1.0, 16.0)),
        "dn_A_log_b": jnp.log(jax.random.uniform(ks[6], (L, N_DN_HEADS), f32, 1.0, 16.0)),
        "dn_dt_bias_f": inv_softplus_dt(ks[7]),
        "dn_dt_bias_b": inv_softplus_dt(ks[8]),
        "dn_norm_w": 1.0 + nrm(ks[9], (L, DN_VDIM), 0.02),
        "attn_q_norm_w": 1.0 + nrm(ks[10], (L, HEAD_DIM), 0.02),
        "attn_k_norm_w": 1.0 + nrm(ks[11], (L, HEAD_DIM), 0.02),
        "attn_out_norm_w": 1.0 + nrm(ks[12], (L, HEAD_DIM), 0.02),
        "w_out": nrm(ks[13], (L, MIX_W, D_MODEL), MIX_W ** -0.5),
        "norm2_w": 1.0 + nrm(ks[14], (L, D_MODEL), 0.02),
        "w_ffn_in": nrm(ks[15], (L, D_MODEL, 2 * D_FF), D_MODEL ** -0.5),
        "ffn_conv_w": nrm(ks[16], (L, FFN_CONV_K, 2 * D_FF), FFN_CONV_K ** -0.5),
        "ffn_conv_b": nrm(ks[17], (L, 2 * D_FF), 0.02),
        "w_ffn_out": nrm(ks[18], (L, D_FF, D_MODEL), D_FF ** -0.5),
    }


def reference(x_prompt, x_sample, norm1_w, w_in, dn_conv_w, dn_A_log_f, dn_A_log_b, dn_dt_bias_f,
              dn_dt_bias_b, dn_norm_w, attn_q_norm_w, attn_k_norm_w, attn_out_norm_w, w_out, norm2_w,
              w_ffn_in, ffn_conv_w, ffn_conv_b, w_ffn_out):
    def trunk(x):
        for l in range(DEPTH):
            x = encoder_layer(x, norm1_w[l], w_in[l], dn_conv_w[l], dn_A_log_f[l], dn_A_log_b[l],
                              dn_dt_bias_f[l], dn_dt_bias_b[l], dn_norm_w[l], attn_q_norm_w[l],
                              attn_k_norm_w[l], attn_out_norm_w[l], w_out[l], norm2_w[l],
                              w_ffn_in[l], ffn_conv_w[l], ffn_conv_b[l], w_ffn_out[l])
        return x

    y_prompt = trunk(x_prompt)
    y_sample = trunk(x_sample)
    return (y_prompt, y_sample)
```

```python
import functools
import math

import jax
import jax.numpy as jnp
from jax import lax
from jax.experimental import pallas as pl
from jax.experimental.pallas import tpu as pltpu

D_MODEL = 1024
GRID_W = 64
HEAD_DIM = 128
N_ATTN_HEADS = 4
N_KV_HEADS = 2
N_DN_HEADS = 4
DN_DIM = 128
DN_CONV_K = 5
CHUNK = 64
ROPE_THETA = 10000.0
D_FF = 2752
EPS = 1e-6

ATTN_Q_W = N_ATTN_HEADS * HEAD_DIM
ATTN_KV_W = N_KV_HEADS * HEAD_DIM
DN_W = N_DN_HEADS * DN_DIM
IN_W = ATTN_Q_W + 2 * ATTN_KV_W + 4 * DN_W + 4 * N_DN_HEADS

LANES = 128
SUBLANES = 8
IN_W_PAD = 3200
D_FF_PAD = 2816
GATE_OFF = ATTN_Q_W + 2 * ATTN_KV_W + 4 * DN_W
HALO = SUBLANES
HALO_BF16 = 2 * SUBLANES

VMEM_LIMIT = 48 * 1024 * 1024

GT_G, GT_BETA, GT_EG, GT_EGL, GT_GL = 0, 8, 16, 24, 32


def _sigmoid(x):
    return 1.0 / (1.0 + jnp.exp(-x))


def _silu(x):
    return x * _sigmoid(x)


def _softplus(x):
    return jnp.maximum(x, 0.0) + jnp.log(1.0 + jnp.exp(-jnp.abs(x)))


def _rms_rows(x, w):
    return x * lax.rsqrt(jnp.mean(x * x, axis=-1, keepdims=True) + EPS) * w


def _in_proj_kernel(xp_ref, x_ref, xn_ref, n1_ref, w_ref, cos_ref, sin_ref, qn_ref, kn_ref,
                    cw_ref, gp_ref,
                    aq_ref, ak_ref, avt_ref, dq_ref, dk_ref, dv_ref, dz_ref, gcol_ref, grow_ref,
                    ext_ref, *, tm, tiles_per_seq):
    i = pl.program_id(0)
    first = (i % tiles_per_seq) == 0
    last = (i % tiles_per_seq) == tiles_per_seq - 1

    n1 = n1_ref[...]
    xe = jnp.concatenate([xp_ref[...], x_ref[...], xn_ref[...]], axis=0)
    hf = _rms_rows(xe, n1)
    he = hf.astype(jnp.bfloat16)
    h = hf[HALO:HALO + tm].astype(jnp.bfloat16)

    cos = cos_ref[...]
    sin = sin_ref[...]
    lane = lax.broadcasted_iota(jnp.int32, (tm, LANES), 1)
    low_half = (lane % 64) < 32

    def rope(xh):
        perm = jnp.where(low_half, pltpu.roll(xh, 96, axis=1), pltpu.roll(xh, 32, axis=1))
        return xh * cos + perm * sin

    pa = jnp.dot(h, w_ref[:, 0:ATTN_Q_W + 2 * ATTN_KV_W], preferred_element_type=jnp.float32)
    qn = qn_ref[...]
    kn = kn_ref[...]
    scale = HEAD_DIM ** -0.5
    for hh in range(N_ATTN_HEADS):
        xh = pa[:, hh * HEAD_DIM:(hh + 1) * HEAD_DIM]
        aq_ref[:, hh * HEAD_DIM:(hh + 1) * HEAD_DIM] = (rope(_rms_rows(xh, qn)) * scale).astype(jnp.bfloat16)
    for hh in range(N_KV_HEADS):
        o = ATTN_Q_W + hh * HEAD_DIM
        xh = pa[:, o:o + HEAD_DIM]
        ak_ref[:, hh * HEAD_DIM:(hh + 1) * HEAD_DIM] = rope(_rms_rows(xh, kn)).astype(jnp.bfloat16)
    av = pa[:, ATTN_Q_W + ATTN_KV_W:ATTN_Q_W + 2 * ATTN_KV_W]
    avt_ref[...] = av.T.astype(jnp.bfloat16)

    row = lax.broadcasted_iota(jnp.int32, (tm + 2 * HALO, 1), 0)
    halo_zero = jnp.logical_or(jnp.logical_and(row < HALO, first),
                               jnp.logical_and(row >= tm + HALO, last))
    keep = jnp.where(halo_zero, 0.0, 1.0)
    dn_off = ATTN_Q_W + 2 * ATTN_KV_W
    for part in range(3):
        c0 = dn_off + part * DN_W
        pd = jnp.dot(he, w_ref[:, c0:c0 + DN_W], preferred_element_type=jnp.float32)
        ext_ref[:, part * DN_W:(part + 1) * DN_W] = pd * keep
    pad = DN_CONV_K // 2
    out_refs = (dq_ref, dk_ref, dv_ref)
    for part in range(3):
        for hh in range(N_DN_HEADS):
            c0 = part * DN_W + hh * DN_DIM
            acc = None
            for j in range(DN_CONV_K):
                term = ext_ref[pl.ds(HALO - pad + j, tm), c0:c0 + DN_DIM] * cw_ref[j:j + 1, c0:c0 + DN_DIM]
                acc = term if acc is None else acc + term
            y = _silu(acc)
            if part < 2:
                y = y * lax.rsqrt(jnp.sum(y * y, axis=-1, keepdims=True) + EPS)
            if part == 0:
                y = y * (DN_DIM ** -0.5)
            out_refs[part][:, hh * DN_DIM:(hh + 1) * DN_DIM] = y.astype(jnp.bfloat16)

    z0 = dn_off + 3 * DN_W
    dz_ref[...] = jnp.dot(h, w_ref[:, z0:z0 + DN_W], preferred_element_type=jnp.float32).astype(jnp.bfloat16)

    pg = jnp.dot(h, w_ref[:, GATE_OFF:GATE_OFF + LANES], preferred_element_type=jnp.float32)
    a_log = gp_ref[0:1, :]
    dt_bias = gp_ref[1:2, :]
    g = -jnp.exp(a_log) * _softplus(pg + dt_bias)
    beta = _sigmoid(pg)
    rin = lax.broadcasted_iota(jnp.int32, (tm, LANES), 0) % CHUNK
    pre = g
    suf = g
    s = 1
    while s < CHUNK:
        pre = pre + jnp.where(rin >= s, pltpu.roll(pre, s, axis=0), 0.0)
        suf = suf + jnp.where(rin < CHUNK - s, pltpu.roll(suf, tm - s, axis=0), 0.0)
        s *= 2
    tot = pre + suf - g
    gcum = jnp.where(lane < N_DN_HEADS, pre, suf)
    tab = jnp.where(lane < GT_BETA, gcum, 0.0)
    tab = jnp.where(jnp.logical_and(lane >= GT_BETA, lane < GT_EG), beta, tab)
    tab = jnp.where(jnp.logical_and(lane >= GT_EG, lane < GT_EGL),
                    pltpu.roll(jnp.exp(gcum), GT_EG, axis=1), tab)
    tab = jnp.where(jnp.logical_and(lane >= GT_EGL, lane < GT_GL),
                    pltpu.roll(jnp.exp(tot - gcum), GT_EGL, axis=1), tab)
    tab = jnp.where(jnp.logical_and(lane >= GT_GL, lane < GT_GL + 8),
                    pltpu.roll(jnp.exp(tot), GT_GL, axis=1), tab)
    gcol_ref[...] = tab
    gt = jnp.where(lane < GT_BETA, gcum, 0.0).T
    for c in range(tm // CHUNK):
        grow_ref[c] = gt[0:SUBLANES, c * CHUNK:(c + 1) * CHUNK]


def _in_proj(x2, seq_len, n1, w_in, cos, sin, qn, kn, conv_w, gate_p, *, tm):
    n = x2.shape[0]
    tiles_per_seq = seq_len // tm
    nblk8 = n // HALO
    tpb = tm // HALO
    const = lambda i: (0, 0)
    tok = lambda i: (i, 0)
    bf = jnp.bfloat16
    out_shape = (
        jax.ShapeDtypeStruct((n, ATTN_Q_W), bf),
        jax.ShapeDtypeStruct((n, ATTN_KV_W), bf),
        jax.ShapeDtypeStruct((ATTN_KV_W, n), bf),
        jax.ShapeDtypeStruct((n, DN_W), bf),
        jax.ShapeDtypeStruct((n, DN_W), bf),
        jax.ShapeDtypeStruct((n, DN_W), bf),
        jax.ShapeDtypeStruct((n, DN_W), bf),
        jax.ShapeDtypeStruct((n, LANES), jnp.float32),
        jax.ShapeDtypeStruct((n // CHUNK, SUBLANES, CHUNK), jnp.float32),
    )
    out_specs = (
        pl.BlockSpec((tm, ATTN_Q_W), tok),
        pl.BlockSpec((tm, ATTN_KV_W), tok),
        pl.BlockSpec((ATTN_KV_W, tm), lambda i: (0, i)),
        pl.BlockSpec((tm, DN_W), tok),
        pl.BlockSpec((tm, DN_W), tok),
        pl.BlockSpec((tm, DN_W), tok),
        pl.BlockSpec((tm, DN_W), tok),
        pl.BlockSpec((tm, LANES), tok),
        pl.BlockSpec((tm // CHUNK, SUBLANES, CHUNK), lambda i: (i, 0, 0)),
    )
    in_specs = [
        pl.BlockSpec((HALO, D_MODEL), lambda i: (jnp.maximum(i * tpb - 1, 0), 0)),
        pl.BlockSpec((tm, D_MODEL), tok),
        pl.BlockSpec((HALO, D_MODEL), lambda i: (jnp.minimum((i + 1) * tpb, nblk8 - 1), 0)),
        pl.BlockSpec((1, D_MODEL), const),
        pl.BlockSpec((D_MODEL, IN_W_PAD), const),
        pl.BlockSpec((tm, LANES), lambda i: (i % tiles_per_seq, 0)),
        pl.BlockSpec((tm, LANES), lambda i: (i % tiles_per_seq, 0)),
        pl.BlockSpec((1, HEAD_DIM), const),
        pl.BlockSpec((1, HEAD_DIM), const),
        pl.BlockSpec((DN_CONV_K, 3 * DN_W), const),
        pl.BlockSpec((SUBLANES, LANES), const),
    ]
    return pl.pallas_call(
        functools.partial(_in_proj_kernel, tm=tm, tiles_per_seq=tiles_per_seq),
        grid=(n // tm,),
        in_specs=in_specs,
        out_specs=out_specs,
        out_shape=out_shape,
        scratch_shapes=[pltpu.VMEM((tm + 2 * HALO, 3 * DN_W), jnp.float32)],
        compiler_params=pltpu.CompilerParams(dimension_semantics=("arbitrary",),
                                             vmem_limit_bytes=VMEM_LIMIT),
        name="in_proj",
    )(x2, x2, x2, n1, w_in, cos, sin, qn, kn, conv_w, gate_p)


def _attn_kernel(q_ref, k_ref, vt_ref, on_ref, o_ref, *, tq, tk, seq_len):
    g = N_ATTN_HEADS // N_KV_HEADS
    nq = g * LANES
    on = on_ref[...]

    def q_tile(qi, carry):
        r0 = pl.multiple_of(qi * LANES, LANES)
        qblk = q_ref[pl.ds(r0, LANES), :]
        qs = jnp.concatenate([qblk[:, j * HEAD_DIM:(j + 1) * HEAD_DIM] for j in range(g)], axis=0)

        def kv_step(kj, c):
            m, l, acc = c
            k0 = pl.multiple_of(kj * tk, tk)
            kb = k_ref[pl.ds(k0, tk), :]
            st = lax.dot_general(kb, qs, (((1,), (1,)), ((), ())),
                                 preferred_element_type=jnp.float32)
            m_new = jnp.maximum(m, jnp.max(st, axis=0, keepdims=True))
            alpha = jnp.exp(m - m_new)
            p = jnp.exp(st - m_new)
            l = alpha * l + jnp.sum(p, axis=0, keepdims=True)
            vtb = vt_ref[:, pl.ds(k0, tk)]
            acc = alpha * acc + jnp.dot(vtb, p.astype(jnp.bfloat16),
                                        preferred_element_type=jnp.float32)
            return m_new, l, acc

        m0 = jnp.full((1, nq), -jnp.inf, jnp.float32)
        l0 = jnp.zeros((1, nq), jnp.float32)
        a0 = jnp.zeros((HEAD_DIM, nq), jnp.float32)
        m, l, acc = lax.fori_loop(0, seq_len // tk, kv_step, (m0, l0, a0))
        o = acc / l
        o = o * lax.rsqrt(jnp.mean(o * o, axis=0, keepdims=True) + EPS) * on
        ot = o.T
        for j in range(g):
            o_ref[pl.ds(r0, LANES), j * HEAD_DIM:(j + 1) * HEAD_DIM] = (
                ot[j * LANES:(j + 1) * LANES, :].astype(jnp.bfloat16))
        return carry

    lax.fori_loop(0, tq // LANES, q_tile, 0)


def _attention(aq, ak, avt, out_norm_col, batch, seq_len, *, tq, tk):
    n = aq.shape[0]
    g = N_ATTN_HEADS // N_KV_HEADS
    qpb = seq_len // tq
    return pl.pallas_call(
        functools.partial(_attn_kernel, tq=tq, tk=tk, seq_len=seq_len),
        grid=(batch, N_KV_HEADS, qpb),
        in_specs=[
            pl.BlockSpec((tq, g * HEAD_DIM), lambda b, h, i: (b * qpb + i, h)),
            pl.BlockSpec((seq_len, HEAD_DIM), lambda b, h, i: (b, h)),
            pl.BlockSpec((HEAD_DIM, seq_len), lambda b, h, i: (h, b)),
            pl.BlockSpec((HEAD_DIM, 1), lambda b, h, i: (0, 0)),
        ],
        out_specs=pl.BlockSpec((tq, g * HEAD_DIM), lambda b, h, i: (b * qpb + i, h)),
        out_shape=jax.ShapeDtypeStruct((n, ATTN_Q_W), jnp.bfloat16),
        compiler_params=pltpu.CompilerParams(
            dimension_semantics=("arbitrary", "arbitrary", "arbitrary"),
            vmem_limit_bytes=VMEM_LIMIT),
        name="attention",
    )(aq, ak, avt, out_norm_col)


def _dn_kernel(qf_ref, kf_ref, vf_ref, tf_ref, rf_ref, qb_ref, kb_ref, vb_ref, tb_ref, rb_ref,
               of_ref, ob_ref,
               s_ref, wq_ref, u_ref, at_ref, kd_ref, *, nchunk):
    seg = pl.program_id(1)

    @pl.when(seg == 0)
    def _():
        s_ref[...] = jnp.zeros_like(s_ref)

    ri = lax.broadcasted_iota(jnp.int32, (CHUNK, CHUNK), 0)
    ci = lax.broadcasted_iota(jnp.int32, (CHUNK, CHUNK), 1)
    eye = jnp.where(ri == ci, 1.0, 0.0)
    incl = (ri >= ci, ri <= ci)
    strict = (ri > ci, ri < ci)
    ins = ((qf_ref, kf_ref, vf_ref, tf_ref, rf_ref), (qb_ref, kb_ref, vb_ref, tb_ref, rb_ref))
    outs = (of_ref, ob_ref)
    bf = jnp.bfloat16

    def mm(a, b):
        return jnp.dot(a.astype(bf), b.astype(bf), preferred_element_type=jnp.float32)

    def prep(c, carry):
        r0 = pl.multiple_of(c * CHUNK, CHUNK)
        for d in range(2):
            q_ref, k_ref, v_ref, t_ref, g_ref = ins[d]
            tab = t_ref[pl.ds(r0, CHUNK), :]
            grow = g_ref[c]
            for hh in range(N_DN_HEADS):
                col = d * N_DN_HEADS + hh
                unit = (d * N_DN_HEADS + hh) * nchunk + c
                lo = hh * DN_DIM
                q = q_ref[pl.ds(r0, CHUNK), lo:lo + DN_DIM]
                k = k_ref[pl.ds(r0, CHUNK), lo:lo + DN_DIM]
                v = v_ref[pl.ds(r0, CHUNK), lo:lo + DN_DIM].astype(jnp.float32)
                kf32 = k.astype(jnp.float32)
                gc = tab[:, GT_G + col:GT_G + col + 1]
                beta = tab[:, GT_BETA + col:GT_BETA + col + 1]
                eg = tab[:, GT_EG + col:GT_EG + col + 1]
                egl = tab[:, GT_EGL + col:GT_EGL + col + 1]
                gr = grow[col:col + 1, :]
                dec = jnp.exp(jnp.minimum(gc - gr, 0.0))
                kq = jnp.concatenate([k, q], axis=0)
                kkqk = lax.dot_general(kq, k, (((1,), (1,)), ((), ())),
                                       preferred_element_type=jnp.float32)
                x = jnp.where(strict[d], -(beta * kkqk[0:CHUNK]) * dec, 0.0)
                attn = jnp.where(incl[d], kkqk[CHUNK:2 * CHUNK] * dec, 0.0)
                ssum = eye + x
                p = mm(x, x)
                for _ in range(4):
                    pp = mm(p, jnp.concatenate([p, ssum], axis=1))
                    ssum = ssum + pp[:, CHUNK:2 * CHUNK]
                    p = pp[:, 0:CHUNK]
                ssum = ssum + mm(p, ssum)
                rhs = jnp.concatenate([v * beta, kf32 * (beta * eg)], axis=1)
                uw = mm(ssum, rhs)
                u_ref[unit] = uw[:, 0:DN_DIM]
                wq_ref[unit, 0:CHUNK, :] = uw[:, DN_DIM:2 * DN_DIM].astype(bf)
                wq_ref[unit, CHUNK:2 * CHUNK, :] = (q.astype(jnp.float32) * eg).astype(bf)
                at_ref[unit] = attn.astype(bf)
                kd_ref[unit] = (kf32 * egl).astype(bf)
        return carry

    lax.fori_loop(0, nchunk, prep, 0)

    def scan(step, carry):
        for d in range(2):
            c = step if d == 0 else nchunk - 1 - step
            r0 = pl.multiple_of(c * CHUNK, CHUNK)
            t_ref = ins[d][3]
            tabrow = t_ref[pl.ds(r0, 1), :]
            for hh in range(N_DN_HEADS):
                col = d * N_DN_HEADS + hh
                unit = col * nchunk + c
                st = s_ref[col]
                res = jnp.dot(wq_ref[unit], st.astype(bf), preferred_element_type=jnp.float32)
                v_new = (u_ref[unit] - res[0:CHUNK]).astype(bf)
                o = res[CHUNK:2 * CHUNK] + jnp.dot(at_ref[unit], v_new,
                                                   preferred_element_type=jnp.float32)
                outs[d][pl.ds(r0, CHUNK), hh * DN_DIM:(hh + 1) * DN_DIM] = o.astype(bf)
                gl = tabrow[:, GT_GL + col:GT_GL + col + 1]
                upd = lax.dot_general(kd_ref[unit], v_new, (((0,), (0,)), ((), ())),
                                      preferred_element_type=jnp.float32)
                s_ref[col] = st * gl + upd
        return carry

    lax.fori_loop(0, nchunk, scan, 0)


def _deltanet(dq, dk, dv, gcol, grow, batch, seq_len, *, seg):
    n = dq.shape[0]
    nseg = seq_len // seg
    nchunk = seg // CHUNK
    units = 2 * N_DN_HEADS * nchunk
    fwd = lambda b, s: (b * nseg + s, 0)
    bwd = lambda b, s: (b * nseg + nseg - 1 - s, 0)
    fwd3 = lambda b, s: (b * nseg + s, 0, 0)
    bwd3 = lambda b, s: (b * nseg + nseg - 1 - s, 0, 0)
    big = lambda m: pl.BlockSpec((seg, DN_W), m)
    tabs = lambda m: pl.BlockSpec((seg, LANES), m)
    rows = lambda m: pl.BlockSpec((nchunk, SUBLANES, CHUNK), m)
    bf = jnp.bfloat16
    return pl.pallas_call(
        functools.partial(_dn_kernel, nchunk=nchunk),
        grid=(batch, nseg),
        in_specs=[big(fwd), big(fwd), big(fwd), tabs(fwd), rows(fwd3),
                  big(bwd), big(bwd), big(bwd), tabs(bwd), rows(bwd3)],
        out_specs=(big(fwd), big(bwd)),
        out_shape=(jax.ShapeDtypeStruct((n, DN_W), bf), jax.ShapeDtypeStruct((n, DN_W), bf)),
        scratch_shapes=[
            pltpu.VMEM((2 * N_DN_HEADS, DN_DIM, DN_DIM), jnp.float32),
            pltpu.VMEM((units, 2 * CHUNK, DN_DIM), bf),
            pltpu.VMEM((units, CHUNK, DN_DIM), jnp.float32),
            pltpu.VMEM((units, CHUNK, CHUNK), bf),
            pltpu.VMEM((units, CHUNK, DN_DIM), bf),
        ],
        compiler_params=pltpu.CompilerParams(dimension_semantics=("arbitrary", "arbitrary"),
                                             vmem_limit_bytes=VMEM_LIMIT),
        name="deltanet",
    )(dq, dk, dv, gcol, grow, dq, dk, dv, gcol, grow)


def _mix_kernel(x_ref, ao_ref, of_ref, ob_ref, z_ref, dnw_ref, wo_ref, n2_ref, x1_ref, h2_ref):
    dnw = dnw_ref[...]
    parts = []
    for hh in range(N_DN_HEADS):
        sl = slice(hh * DN_DIM, (hh + 1) * DN_DIM)
        o = of_ref[:, sl].astype(jnp.float32) + ob_ref[:, sl].astype(jnp.float32)
        z = z_ref[:, sl].astype(jnp.float32)
        parts.append((_rms_rows(o, dnw) * _silu(z)).astype(jnp.bfloat16))
    dn = jnp.concatenate(parts, axis=1)
    y = jnp.dot(ao_ref[...], wo_ref[0:ATTN_Q_W, :], preferred_element_type=jnp.float32)
    y = y + jnp.dot(dn, wo_ref[ATTN_Q_W:ATTN_Q_W + DN_W, :], preferred_element_type=jnp.float32)
    x1 = x_ref[...] + y
    x1_ref[...] = x1
    h2_ref[...] = _rms_rows(x1, n2_ref[...]).astype(jnp.bfloat16)


def _mix_out(x2, ao, o_f, o_b, dz, dn_norm, w_out, n2, *, tm):
    n = x2.shape[0]
    tok = lambda i: (i, 0)
    const = lambda i: (0, 0)
    half = pl.BlockSpec((tm, DN_W), tok)
    return pl.pallas_call(
        _mix_kernel,
        grid=(n // tm,),
        in_specs=[pl.BlockSpec((tm, D_MODEL), tok), half, half, half, half,
                  pl.BlockSpec((1, DN_DIM), const),
                  pl.BlockSpec((ATTN_Q_W + DN_W, D_MODEL), const),
                  pl.BlockSpec((1, D_MODEL), const)],
        out_specs=(pl.BlockSpec((tm, D_MODEL), tok), pl.BlockSpec((tm, D_MODEL), tok)),
        out_shape=(jax.ShapeDtypeStruct((n, D_MODEL), jnp.float32),
                   jax.ShapeDtypeStruct((n, D_MODEL), jnp.bfloat16)),
        compiler_params=pltpu.CompilerParams(dimension_semantics=("arbitrary",),
                                             vmem_limit_bytes=VMEM_LIMIT),
        name="mix_out",
    )(x2, ao, o_f, o_b, dz, dn_norm, w_out, n2)


def _ffn_kernel(hp_ref, h_ref, hn_ref, x1_ref, wg_ref, wu_ref, cwg_ref, cwu_ref, bg_ref, bu_ref,
                wd_ref, y_ref, acc_ref, *, tm, fc, tiles_per_seq):
    i = pl.program_id(0)
    first = (i % tiles_per_seq) == 0
    last = (i % tiles_per_seq) == tiles_per_seq - 1
    he = jnp.concatenate([hp_ref[...], h_ref[...], hn_ref[...]], axis=0)
    rows = tm + 2 * HALO_BF16
    row = lax.broadcasted_iota(jnp.int32, (rows, 1), 0)
    halo_zero = jnp.logical_or(jnp.logical_and(row < HALO_BF16, first),
                               jnp.logical_and(row >= tm + HALO_BF16, last))
    keep = jnp.where(halo_zero, 0.0, 1.0)

    def conv(u, cw_ref, b_ref, sl):
        u = u * keep
        prev = pltpu.roll(u, 1, axis=0)[HALO_BF16:HALO_BF16 + tm]
        nxt = pltpu.roll(u, rows - 1, axis=0)[HALO_BF16:HALO_BF16 + tm]
        mid = u[HALO_BF16:HALO_BF16 + tm]
        return prev * cw_ref[0:1, sl] + mid * cw_ref[1:2, sl] + nxt * cw_ref[2:3, sl] + b_ref[0:1, sl]

    for c in range(D_FF_PAD // fc):
        sl = slice(c * fc, (c + 1) * fc)
        ug = jnp.dot(he, wg_ref[:, sl], preferred_element_type=jnp.float32)
        uu = jnp.dot(he, wu_ref[:, sl], preferred_element_type=jnp.float32)
        act = (_silu(conv(ug, cwg_ref, bg_ref, sl)) * conv(uu, cwu_ref, bu_ref, sl)).astype(jnp.bfloat16)
        part = jnp.dot(act, wd_ref[sl, :], preferred_element_type=jnp.float32)
        if c == 0:
            acc_ref[...] = part
        else:
            acc_ref[...] += part
    y_ref[...] = x1_ref[...] + acc_ref[...]


def _ffn(h2, x1, seq_len, wg, wu, cwg, cwu, bg, bu, wd, *, tm, fc):
    n = h2.shape[0]
    tiles_per_seq = seq_len // tm
    nblk = n // HALO_BF16
    tpb = tm // HALO_BF16
    tok = lambda i: (i, 0)
    const = lambda i: (0, 0)
    return pl.pallas_call(
        functools.partial(_ffn_kernel, tm=tm, fc=fc, tiles_per_seq=tiles_per_seq),
        grid=(n // tm,),
        in_specs=[
            pl.BlockSpec((HALO_BF16, D_MODEL), lambda i: (jnp.maximum(i * tpb - 1, 0), 0)),
            pl.BlockSpec((tm, D_MODEL), tok),
            pl.BlockSpec((HALO_BF16, D_MODEL), lambda i: (jnp.minimum((i + 1) * tpb, nblk - 1), 0)),
            pl.BlockSpec((tm, D_MODEL), tok),
            pl.BlockSpec((D_MODEL, D_FF_PAD), const),
            pl.BlockSpec((D_MODEL, D_FF_PAD), const),
            pl.BlockSpec((3, D_FF_PAD), const),
            pl.BlockSpec((3, D_FF_PAD), const),
            pl.BlockSpec((1, D_FF_PAD), const),
            pl.BlockSpec((1, D_FF_PAD), const),
            pl.BlockSpec((D_FF_PAD, D_MODEL), const),
        ],
        out_specs=pl.BlockSpec((tm, D_MODEL), tok),
        out_shape=jax.ShapeDtypeStruct((n, D_MODEL), jnp.float32),
        scratch_shapes=[pltpu.VMEM((tm, D_MODEL), jnp.float32)],
        compiler_params=pltpu.CompilerParams(dimension_semantics=("arbitrary",),
                                             vmem_limit_bytes=56 * 1024 * 1024),
        name="ffn",
    )(h2, h2, h2, x1, wg, wu, cwg, cwu, bg, bu, wd)


def _rope_tables(seq_len):
    half = HEAD_DIM // 2
    inv = ROPE_THETA ** (-jnp.arange(0, half, 2, dtype=jnp.float32) / half)
    t = jnp.arange(seq_len, dtype=jnp.int32)
    rows = (t // GRID_W).astype(jnp.float32)[:, None] * inv[None, :]
    cols = (t % GRID_W).astype(jnp.float32)[:, None] * inv[None, :]
    cos = jnp.concatenate([jnp.cos(rows), jnp.cos(rows), jnp.cos(cols), jnp.cos(cols)], axis=1)
    sin = jnp.concatenate([-jnp.sin(rows), jnp.sin(rows), -jnp.sin(cols), jnp.sin(cols)], axis=1)
    return cos, sin


def _pad_cols(a, width):
    return jnp.pad(a, ((0, 0), (0, width - a.shape[1])))


def _prepare(norm1_w, w_in, dn_conv_w, dn_A_log_f, dn_A_log_b, dn_dt_bias_f, dn_dt_bias_b, dn_norm_w,
             attn_q_norm_w, attn_k_norm_w, attn_out_norm_w, w_out, norm2_w, w_ffn_in, ffn_conv_w,
             ffn_conv_b, w_ffn_out):
    bf = jnp.bfloat16
    gate_p = jnp.zeros((SUBLANES, LANES), jnp.float32)
    gate_p = gate_p.at[0, 0:4].set(dn_A_log_f).at[0, 4:8].set(dn_A_log_b)
    gate_p = gate_p.at[1, 0:4].set(dn_dt_bias_f).at[1, 4:8].set(dn_dt_bias_b)
    return dict(
        n1=norm1_w.reshape(1, D_MODEL),
        w_in=_pad_cols(w_in, IN_W_PAD).astype(bf),
        conv_w=dn_conv_w,
        gate_p=gate_p,
        dn_norm=dn_norm_w.reshape(1, DN_DIM),
        qn=attn_q_norm_w.reshape(1, HEAD_DIM),
        kn=attn_k_norm_w.reshape(1, HEAD_DIM),
        on=attn_out_norm_w.reshape(HEAD_DIM, 1),
        w_out=w_out.astype(bf),
        n2=norm2_w.reshape(1, D_MODEL),
        wg=_pad_cols(w_ffn_in[:, :D_FF], D_FF_PAD).astype(bf),
        wu=_pad_cols(w_ffn_in[:, D_FF:], D_FF_PAD).astype(bf),
        cwg=_pad_cols(ffn_conv_w[:, :D_FF], D_FF_PAD),
        cwu=_pad_cols(ffn_conv_w[:, D_FF:], D_FF_PAD),
        bg=_pad_cols(ffn_conv_b[None, :D_FF], D_FF_PAD),
        bu=_pad_cols(ffn_conv_b[None, D_FF:], D_FF_PAD),
        wd=jnp.pad(w_ffn_out, ((0, D_FF_PAD - D_FF), (0, 0))).astype(bf),
    )


def _tiles(seq_len):
    return dict(
        tm_in=min(512, seq_len),
        tq=min(2048, seq_len),
        tk=min(512, seq_len),
        seg=min(512, seq_len),
        tm_mix=min(512, seq_len),
        tm_ffn=min(512, seq_len),
        fc=256,
    )


def _layer(x, p):
    batch, seq_len, _ = x.shape
    t = _tiles(seq_len)
    x2 = x.reshape(batch * seq_len, D_MODEL)
    cos, sin = _rope_tables(seq_len)
    aq, ak, avt, dq, dk, dv, dz, gcol, grow = _in_proj(
        x2, seq_len, p["n1"], p["w_in"], cos, sin, p["qn"], p["kn"], p["conv_w"], p["gate_p"],
        tm=t["tm_in"])
    ao = _attention(aq, ak, avt, p["on"], batch, seq_len, tq=t["tq"], tk=t["tk"])
    o_f, o_b = _deltanet(dq, dk, dv, gcol, grow, batch, seq_len, seg=t["seg"])
    x1, h2 = _mix_out(x2, ao, o_f, o_b, dz, p["dn_norm"], p["w_out"], p["n2"], tm=t["tm_mix"])
    y = _ffn(h2, x1, seq_len, p["wg"], p["wu"], p["cwg"], p["cwu"], p["bg"], p["bu"], p["wd"],
             tm=t["tm_ffn"], fc=t["fc"])
    return y.reshape(batch, seq_len, D_MODEL)


def kernel(x_prompt, x_sample, norm1_w, w_in, dn_conv_w, dn_A_log_f, dn_A_log_b, dn_dt_bias_f, dn_dt_bias_b, dn_norm_w, attn_q_norm_w, attn_k_norm_w, attn_out_norm_w, w_out, norm2_w, w_ffn_in, ffn_conv_w, ffn_conv_b, w_ffn_out):
    depth = norm1_w.shape[0]

    def trunk(x):
        for l in range(depth):
            p = _prepare(norm1_w[l], w_in[l], dn_conv_w[l], dn_A_log_f[l], dn_A_log_b[l],
                         dn_dt_bias_f[l], dn_dt_bias_b[l], dn_norm_w[l], attn_q_norm_w[l],
                         attn_k_norm_w[l], attn_out_norm_w[l], w_out[l], norm2_w[l], w_ffn_in[l],
                         ffn_conv_w[l], ffn_conv_b[l], w_ffn_out[l])
            x = _layer(x, p)
        return x

    return (trunk(x_prompt), trunk(x_sample))
```

```python
import functools
import math

import jax
import jax.numpy as jnp
from jax import lax
from jax.experimental import pallas as pl
from jax.experimental.pallas import tpu as pltpu

D_MODEL = 1024
GRID_W = 64
HEAD_DIM = 128
N_ATTN_HEADS = 4
N_KV_HEADS = 2
N_DN_HEADS = 4
DN_DIM = 128
DN_CONV_K = 5
CHUNK = 64
ROPE_THETA = 10000.0
D_FF = 2752
EPS = 1e-6

ATTN_Q_W = N_ATTN_HEADS * HEAD_DIM
ATTN_KV_W = N_KV_HEADS * HEAD_DIM
DN_W = N_DN_HEADS * DN_DIM
IN_W = ATTN_Q_W + 2 * ATTN_KV_W + 4 * DN_W + 4 * N_DN_HEADS

LANES = 128
SUBLANES = 8
IN_W_PAD = 3200
D_FF_PAD = 2816
GATE_OFF = ATTN_Q_W + 2 * ATTN_KV_W + 4 * DN_W
HALO = SUBLANES
HALO_BF16 = 2 * SUBLANES

VMEM_LIMIT = 48 * 1024 * 1024

GT_G, GT_BETA, GT_EG, GT_EGL, GT_GL = 0, 8, 16, 24, 32

SUM_ROWS = 2 * SUBLANES
QK_SPLIT = 1


def _sigmoid(x):
    return 1.0 / (1.0 + jnp.exp(-x))


def _silu(x):
    return x * _sigmoid(x)


def _softplus(x):
    return jnp.maximum(x, 0.0) + jnp.log(1.0 + jnp.exp(-jnp.abs(x)))


def _rms_rows(x, w):
    return x * lax.rsqrt(jnp.mean(x * x, axis=-1, keepdims=True) + EPS) * w


def _aligned(x, m):
    return x if isinstance(x, int) else pl.multiple_of(x, m)


def _in_proj_kernel(xp_ref, x_ref, xn_ref, n1_ref, w_ref, cos_ref, sin_ref, qn_ref, kn_ref,
                    cw_ref, gp_ref,
                    aq_ref, ak_ref, avt_ref, dq_ref, dk_ref, dv_ref, dz_ref, gcol_ref, grow_ref,
                    ext_ref, *, tm, tiles_per_seq):
    i = pl.program_id(0)
    first = (i % tiles_per_seq) == 0
    last = (i % tiles_per_seq) == tiles_per_seq - 1

    n1 = n1_ref[...]
    xe = jnp.concatenate([xp_ref[...], x_ref[...], xn_ref[...]], axis=0)
    hf = _rms_rows(xe, n1)
    he = hf.astype(jnp.bfloat16)
    h = hf[HALO:HALO + tm].astype(jnp.bfloat16)

    cos = cos_ref[...]
    sin = sin_ref[...]
    lane = lax.broadcasted_iota(jnp.int32, (tm, LANES), 1)
    low_half = (lane % 64) < 32

    def rope(xh):
        perm = jnp.where(low_half, pltpu.roll(xh, 96, axis=1), pltpu.roll(xh, 32, axis=1))
        return xh * cos + perm * sin

    pa = jnp.dot(h, w_ref[:, 0:ATTN_Q_W + 2 * ATTN_KV_W], preferred_element_type=jnp.float32)
    qn = qn_ref[...]
    kn = kn_ref[...]
    scale = (HEAD_DIM ** -0.5) * math.log2(math.e)
    for hh in range(N_ATTN_HEADS):
        xh = pa[:, hh * HEAD_DIM:(hh + 1) * HEAD_DIM]
        aq_ref[:, hh * HEAD_DIM:(hh + 1) * HEAD_DIM] = (rope(_rms_rows(xh, qn)) * scale).astype(jnp.bfloat16)
    for hh in range(N_KV_HEADS):
        o = ATTN_Q_W + hh * HEAD_DIM
        xh = pa[:, o:o + HEAD_DIM]
        ak_ref[:, hh * HEAD_DIM:(hh + 1) * HEAD_DIM] = rope(_rms_rows(xh, kn)).astype(jnp.bfloat16)
    av = pa[:, ATTN_Q_W + ATTN_KV_W:ATTN_Q_W + 2 * ATTN_KV_W]
    avt_ref[...] = av.T.astype(jnp.bfloat16)

    row = lax.broadcasted_iota(jnp.int32, (tm + 2 * HALO, 1), 0)
    halo_zero = jnp.logical_or(jnp.logical_and(row < HALO, first),
                               jnp.logical_and(row >= tm + HALO, last))
    keep = jnp.where(halo_zero, 0.0, 1.0)
    dn_off = ATTN_Q_W + 2 * ATTN_KV_W
    for part in range(3):
        c0 = dn_off + part * DN_W
        pd = jnp.dot(he, w_ref[:, c0:c0 + DN_W], preferred_element_type=jnp.float32)
        ext_ref[:, part * DN_W:(part + 1) * DN_W] = pd * keep
    pad = DN_CONV_K // 2
    out_refs = (dq_ref, dk_ref, dv_ref)
    for part in range(3):
        for hh in range(N_DN_HEADS):
            c0 = part * DN_W + hh * DN_DIM
            acc = None
            for j in range(DN_CONV_K):
                term = ext_ref[pl.ds(HALO - pad + j, tm), c0:c0 + DN_DIM] * cw_ref[j:j + 1, c0:c0 + DN_DIM]
                acc = term if acc is None else acc + term
            y = _silu(acc)
            if part < 2:
                y = y * lax.rsqrt(jnp.sum(y * y, axis=-1, keepdims=True) + EPS)
            if part == 0:
                y = y * (DN_DIM ** -0.5)
            out_refs[part][:, hh * DN_DIM:(hh + 1) * DN_DIM] = y.astype(jnp.bfloat16)

    z0 = dn_off + 3 * DN_W
    dz_ref[...] = jnp.dot(h, w_ref[:, z0:z0 + DN_W], preferred_element_type=jnp.float32).astype(jnp.bfloat16)

    pg = jnp.dot(h, w_ref[:, GATE_OFF:GATE_OFF + LANES], preferred_element_type=jnp.float32)
    a_log = gp_ref[0:1, :]
    dt_bias = gp_ref[1:2, :]
    g = -jnp.exp(a_log) * _softplus(pg + dt_bias)
    beta = _sigmoid(pg)
    rin = lax.broadcasted_iota(jnp.int32, (tm, LANES), 0) % CHUNK
    pre = g
    suf = g
    s = 1
    while s < CHUNK:
        pre = pre + jnp.where(rin >= s, pltpu.roll(pre, s, axis=0), 0.0)
        suf = suf + jnp.where(rin < CHUNK - s, pltpu.roll(suf, tm - s, axis=0), 0.0)
        s *= 2
    tot = pre + suf - g
    gcum = jnp.where(lane < N_DN_HEADS, pre, suf)
    tab = jnp.where(lane < GT_BETA, gcum, 0.0)
    tab = jnp.where(jnp.logical_and(lane >= GT_BETA, lane < GT_EG), beta, tab)
    tab = jnp.where(jnp.logical_and(lane >= GT_EG, lane < GT_EGL),
                    pltpu.roll(jnp.exp(gcum), GT_EG, axis=1), tab)
    tab = jnp.where(jnp.logical_and(lane >= GT_EGL, lane < GT_GL),
                    pltpu.roll(jnp.exp(tot - gcum), GT_EGL, axis=1), tab)
    tab = jnp.where(jnp.logical_and(lane >= GT_GL, lane < GT_GL + 8),
                    pltpu.roll(jnp.exp(tot), GT_GL, axis=1), tab)
    gcol_ref[...] = tab
    gt = jnp.where(lane < GT_BETA, gcum, 0.0).T
    for c in range(tm // CHUNK):
        grow_ref[c] = gt[0:SUBLANES, c * CHUNK:(c + 1) * CHUNK]


def _in_proj(x2, seq_len, n1, w_in, cos, sin, qn, kn, conv_w, gate_p, *, tm):
    n = x2.shape[0]
    tiles_per_seq = seq_len // tm
    nblk8 = n // HALO
    tpb = tm // HALO
    const = lambda i: (0, 0)
    tok = lambda i: (i, 0)
    bf = jnp.bfloat16
    out_shape = (
        jax.ShapeDtypeStruct((n, ATTN_Q_W), bf),
        jax.ShapeDtypeStruct((n, ATTN_KV_W), bf),
        jax.ShapeDtypeStruct((ATTN_KV_W, n), bf),
        jax.ShapeDtypeStruct((n, DN_W), bf),
        jax.ShapeDtypeStruct((n, DN_W), bf),
        jax.ShapeDtypeStruct((n, DN_W), bf),
        jax.ShapeDtypeStruct((n, DN_W), bf),
        jax.ShapeDtypeStruct((n, LANES), jnp.float32),
        jax.ShapeDtypeStruct((n // CHUNK, SUBLANES, CHUNK), jnp.float32),
    )
    out_specs = (
        pl.BlockSpec((tm, ATTN_Q_W), tok),
        pl.BlockSpec((tm, ATTN_KV_W), tok),
        pl.BlockSpec((ATTN_KV_W, tm), lambda i: (0, i)),
        pl.BlockSpec((tm, DN_W), tok),
        pl.BlockSpec((tm, DN_W), tok),
        pl.BlockSpec((tm, DN_W), tok),
        pl.BlockSpec((tm, DN_W), tok),
        pl.BlockSpec((tm, LANES), tok),
        pl.BlockSpec((tm // CHUNK, SUBLANES, CHUNK), lambda i: (i, 0, 0)),
    )
    in_specs = [
        pl.BlockSpec((HALO, D_MODEL), lambda i: (jnp.maximum(i * tpb - 1, 0), 0)),
        pl.BlockSpec((tm, D_MODEL), tok),
        pl.BlockSpec((HALO, D_MODEL), lambda i: (jnp.minimum((i + 1) * tpb, nblk8 - 1), 0)),
        pl.BlockSpec((1, D_MODEL), const),
        pl.BlockSpec((D_MODEL, IN_W_PAD), const),
        pl.BlockSpec((tm, LANES), lambda i: (i % tiles_per_seq, 0)),
        pl.BlockSpec((tm, LANES), lambda i: (i % tiles_per_seq, 0)),
        pl.BlockSpec((1, HEAD_DIM), const),
        pl.BlockSpec((1, HEAD_DIM), const),
        pl.BlockSpec((DN_CONV_K, 3 * DN_W), const),
        pl.BlockSpec((SUBLANES, LANES), const),
    ]
    return pl.pallas_call(
        functools.partial(_in_proj_kernel, tm=tm, tiles_per_seq=tiles_per_seq),
        grid=(n // tm,),
        in_specs=in_specs,
        out_specs=out_specs,
        out_shape=out_shape,
        scratch_shapes=[pltpu.VMEM((tm + 2 * HALO, 3 * DN_W), jnp.float32)],
        compiler_params=pltpu.CompilerParams(dimension_semantics=("arbitrary",),
                                             vmem_limit_bytes=VMEM_LIMIT),
        name="in_proj",
    )(x2, x2, x2, n1, w_in, cos, sin, qn, kn, conv_w, gate_p)


def _attn_kernel(q_ref, k_ref, vt_ref, on_ref, o_ref,
                 s0_ref, s1_ref, p0_ref, p1_ref, al0_ref, al1_ref, cm0_ref, cm1_ref, m_ref, acc_ref,
                 *, tq, tk, seq_len):
    g = N_ATTN_HEADS // N_KV_HEADS
    nq = g * LANES
    nqt = tq // LANES
    nkv = seq_len // tk
    items = nqt * nkv
    kv_shift = nkv.bit_length() - 1
    s_refs = (s0_ref, s1_ref)
    p_refs = (p0_ref, p1_ref)
    al_refs = (al0_ref, al1_ref)
    cm_refs = (cm0_ref, cm1_ref)
    ones = jnp.ones((SUM_ROWS, tk // 2), jnp.bfloat16)

    def split(t):
        if isinstance(t, int):
            return t // nkv, t % nkv
        return lax.shift_right_logical(t, kv_shift), jnp.bitwise_and(t, nkv - 1)

    def stage_qk(t, slot):
        qi, kj = split(t)
        r0 = _aligned(qi * LANES, LANES)
        k0 = _aligned(kj * tk, tk)
        qblk = q_ref[pl.ds(r0, LANES), :]
        qs = jnp.concatenate([qblk[:, j * HEAD_DIM:(j + 1) * HEAD_DIM] for j in range(g)], axis=0)
        cm = None
        rk = tk // QK_SPLIT
        for part in range(QK_SPLIT):
            kp = _aligned(k0 + part * rk, rk)
            kb = k_ref[pl.ds(kp, rk), :]
            s = lax.dot_general(kb, qs, (((1,), (1,)), ((), ())),
                                preferred_element_type=jnp.float32)
            s_refs[slot][part * rk:(part + 1) * rk, :] = s
            pm = jnp.max(s, axis=0, keepdims=True)
            cm = pm if cm is None else jnp.maximum(cm, pm)
        cm_refs[slot][...] = cm

    def stage_softmax(t, slot):
        _, kj = split(t)
        s = s_refs[slot][...]
        m_prev = jnp.where(kj == 0, -jnp.inf, m_ref[...])
        m_new = jnp.maximum(m_prev, cm_refs[slot][...])
        al_refs[slot][...] = jnp.exp2(m_prev - m_new)
        p_refs[slot][...] = jnp.exp2(s - m_new).astype(jnp.bfloat16)
        m_ref[...] = m_new

    def stage_pv(t, slot):
        qi, kj = split(t)
        k0 = _aligned(kj * tk, tk)
        acc = acc_ref[...] * al_refs[slot][...]
        hk = tk // 2
        for part in range(2):
            kp = _aligned(k0 + part * hk, hk)
            lhs = jnp.concatenate([vt_ref[:, pl.ds(kp, hk)], ones], axis=0)
            acc = acc + jnp.dot(lhs, p_refs[slot][part * hk:(part + 1) * hk, :],
                                preferred_element_type=jnp.float32)
        acc_ref[...] = acc

        def finalize():
            o = acc[0:HEAD_DIM] / acc[HEAD_DIM:HEAD_DIM + 1]
            o = o * lax.rsqrt(jnp.mean(o * o, axis=0, keepdims=True) + EPS) * on_ref[...]
            ot = o.T
            r0 = _aligned(qi * LANES, LANES)
            for j in range(g):
                o_ref[pl.ds(r0, LANES), j * HEAD_DIM:(j + 1) * HEAD_DIM] = (
                    ot[j * LANES:(j + 1) * LANES, :].astype(jnp.bfloat16))

        if isinstance(kj, int):
            if kj == nkv - 1:
                finalize()
        else:
            pl.when(kj == nkv - 1)(finalize)

    acc_ref[...] = jnp.zeros_like(acc_ref)
    m_ref[...] = jnp.full(m_ref.shape, -jnp.inf, jnp.float32)
    stage_qk(0, 0)
    stage_qk(1, 1)
    stage_softmax(0, 0)

    def body(i, carry):
        t = 2 * i + 2
        stage_qk(t, 0)
        stage_softmax(t - 1, 1)
        stage_pv(t - 2, 0)
        stage_qk(t + 1, 1)
        stage_softmax(t, 0)
        stage_pv(t - 1, 1)
        return carry

    lax.fori_loop(0, (items - 2) // 2, body, 0)
    stage_softmax(items - 1, 1)
    stage_pv(items - 2, 0)
    stage_pv(items - 1, 1)


def _attention(aq, ak, avt, out_norm_col, batch, seq_len, *, tq, tk):
    n = aq.shape[0]
    g = N_ATTN_HEADS // N_KV_HEADS
    nq = g * LANES
    qpb = seq_len // tq
    nkv = seq_len // tk
    items = (tq // LANES) * nkv
    assert nkv & (nkv - 1) == 0 and items % 2 == 0 and items >= 2
    return pl.pallas_call(
        functools.partial(_attn_kernel, tq=tq, tk=tk, seq_len=seq_len),
        grid=(batch, N_KV_HEADS, qpb),
        in_specs=[
            pl.BlockSpec((tq, g * HEAD_DIM), lambda b, h, i: (b * qpb + i, h)),
            pl.BlockSpec((seq_len, HEAD_DIM), lambda b, h, i: (b, h)),
            pl.BlockSpec((HEAD_DIM, seq_len), lambda b, h, i: (h, b)),
            pl.BlockSpec((HEAD_DIM, 1), lambda b, h, i: (0, 0)),
        ],
        out_specs=pl.BlockSpec((tq, g * HEAD_DIM), lambda b, h, i: (b * qpb + i, h)),
        out_shape=jax.ShapeDtypeStruct((n, ATTN_Q_W), jnp.bfloat16),
        scratch_shapes=[
            pltpu.VMEM((tk, nq), jnp.float32), pltpu.VMEM((tk, nq), jnp.float32),
            pltpu.VMEM((tk, nq), jnp.bfloat16), pltpu.VMEM((tk, nq), jnp.bfloat16),
            pltpu.VMEM((1, nq), jnp.float32), pltpu.VMEM((1, nq), jnp.float32),
            pltpu.VMEM((1, nq), jnp.float32), pltpu.VMEM((1, nq), jnp.float32),
            pltpu.VMEM((1, nq), jnp.float32),
            pltpu.VMEM((HEAD_DIM + SUM_ROWS, nq), jnp.float32),
        ],
        compiler_params=pltpu.CompilerParams(
            dimension_semantics=("arbitrary", "arbitrary", "arbitrary"),
            vmem_limit_bytes=VMEM_LIMIT),
        name="attention",
    )(aq, ak, avt, out_norm_col)


def _dn_kernel(qf_ref, kf_ref, vf_ref, tf_ref, rf_ref, qb_ref, kb_ref, vb_ref, tb_ref, rb_ref,
               of_ref, ob_ref,
               s_ref, wq_ref, u_ref, at_ref, kdt_ref, *, nchunk, prep_chunks):
    seg = pl.program_id(1)

    @pl.when(seg == 0)
    def _():
        s_ref[...] = jnp.zeros_like(s_ref)

    ri = lax.broadcasted_iota(jnp.int32, (CHUNK, CHUNK), 0)
    ci = lax.broadcasted_iota(jnp.int32, (CHUNK, CHUNK), 1)
    eye = jnp.where(ri == ci, 1.0, 0.0)
    incl = (ri >= ci, ri <= ci)
    strict = (ri > ci, ri < ci)
    ins = ((qf_ref, kf_ref, vf_ref, tf_ref, rf_ref), (qb_ref, kb_ref, vb_ref, tb_ref, rb_ref))
    outs = (of_ref, ob_ref)
    bf = jnp.bfloat16
    cols = [(d, hh) for d in range(2) for hh in range(N_DN_HEADS)]

    def mm(a, b):
        return jnp.dot(a.astype(bf), b.astype(bf), preferred_element_type=jnp.float32)

    def prep(ci_, carry):
        units = []
        for pc in range(prep_chunks):
            c = ci_ * prep_chunks + pc
            r0 = pl.multiple_of(c * CHUNK, CHUNK)
            for d, hh in cols:
                q_ref, k_ref, v_ref, t_ref, g_ref = ins[d]
                col = d * N_DN_HEADS + hh
                lo = hh * DN_DIM
                tab = t_ref[pl.ds(r0, CHUNK), :]
                k = k_ref[pl.ds(r0, CHUNK), lo:lo + DN_DIM]
                q = q_ref[pl.ds(r0, CHUNK), lo:lo + DN_DIM]
                units.append(dict(
                    d=d, unit=col * nchunk + c, q=q, k=k,
                    v=v_ref[pl.ds(r0, CHUNK), lo:lo + DN_DIM].astype(jnp.float32),
                    gc=tab[:, GT_G + col:GT_G + col + 1],
                    beta=tab[:, GT_BETA + col:GT_BETA + col + 1],
                    eg=tab[:, GT_EG + col:GT_EG + col + 1],
                    egl=tab[:, GT_EGL + col:GT_EGL + col + 1],
                    gr=g_ref[c][col:col + 1, :]))
        for u in units:
            kq = jnp.concatenate([u["k"], u["q"]], axis=0)
            u["kkqk"] = lax.dot_general(kq, u["k"], (((1,), (1,)), ((), ())),
                                        preferred_element_type=jnp.float32)
        for u in units:
            d = u["d"]
            kf32 = u["k"].astype(jnp.float32)
            dec = jnp.exp(jnp.minimum(u["gc"] - u["gr"], 0.0))
            x = jnp.where(strict[d], -(u["beta"] * u["kkqk"][0:CHUNK]) * dec, 0.0)
            attn = jnp.where(incl[d], u["kkqk"][CHUNK:2 * CHUNK] * dec, 0.0)
            at_ref[u["unit"]] = attn.astype(bf)
            wq_ref[u["unit"], CHUNK:2 * CHUNK, :] = (u["q"].astype(jnp.float32) * u["eg"]).astype(bf)
            kdt_ref[u["unit"]] = (kf32 * u["egl"]).T.astype(bf)
            u["rhs"] = jnp.concatenate([u["v"] * u["beta"], kf32 * (u["beta"] * u["eg"])],
                                       axis=1).astype(bf)
            u["x"] = x
            u["ssum"] = eye + x
        for u in units:
            u["p"] = mm(u["x"], u["x"])
        for _ in range(4):
            for u in units:
                pp = mm(jnp.concatenate([u["p"], u["ssum"]], axis=0), u["p"])
                u["ssum"] = u["ssum"] + pp[CHUNK:2 * CHUNK]
                u["p"] = pp[0:CHUNK]
        for u in units:
            u["ssum"] = u["ssum"] + mm(u["ssum"], u["p"])
        for u in units:
            uw = jnp.dot(u["ssum"].astype(bf), u["rhs"], preferred_element_type=jnp.float32)
            u_ref[u["unit"]] = uw[:, 0:DN_DIM]
            wq_ref[u["unit"], 0:CHUNK, :] = uw[:, DN_DIM:2 * DN_DIM].astype(bf)
        return carry

    lax.fori_loop(0, nchunk // prep_chunks, prep, 0)

    def scan(step, carry):
        chains = []
        for d, hh in cols:
            c = step if d == 0 else nchunk - 1 - step
            r0 = pl.multiple_of(c * CHUNK, CHUNK)
            col = d * N_DN_HEADS + hh
            tabrow = ins[d][3][pl.ds(r0, 1), :]
            chains.append(dict(d=d, hh=hh, col=col, unit=col * nchunk + c, r0=r0,
                               gl=tabrow[:, GT_GL + col:GT_GL + col + 1]))
        for ch in chains:
            ch["st"] = s_ref[ch["col"]]
            ch["res"] = jnp.dot(wq_ref[ch["unit"]], ch["st"].astype(bf),
                                preferred_element_type=jnp.float32)
        for ch in chains:
            ch["vn"] = (u_ref[ch["unit"]] - ch["res"][0:CHUNK]).astype(bf)
        for ch in chains:
            ch["o"] = jnp.dot(at_ref[ch["unit"]], ch["vn"], preferred_element_type=jnp.float32)
            ch["upd"] = jnp.dot(kdt_ref[ch["unit"]], ch["vn"], preferred_element_type=jnp.float32)
        for ch in chains:
            hh = ch["hh"]
            o = ch["res"][CHUNK:2 * CHUNK] + ch["o"]
            outs[ch["d"]][pl.ds(ch["r0"], CHUNK), hh * DN_DIM:(hh + 1) * DN_DIM] = o.astype(bf)
            s_ref[ch["col"]] = ch["st"] * ch["gl"] + ch["upd"]
        return carry

    lax.fori_loop(0, nchunk, scan, 0)


def _deltanet(dq, dk, dv, gcol, grow, batch, seq_len, *, seg, prep_chunks):
    n = dq.shape[0]
    nseg = seq_len // seg
    nchunk = seg // CHUNK
    units = 2 * N_DN_HEADS * nchunk
    fwd = lambda b, s: (b * nseg + s, 0)
    bwd = lambda b, s: (b * nseg + nseg - 1 - s, 0)
    fwd3 = lambda b, s: (b * nseg + s, 0, 0)
    bwd3 = lambda b, s: (b * nseg + nseg - 1 - s, 0, 0)
    big = lambda m: pl.BlockSpec((seg, DN_W), m)
    tabs = lambda m: pl.BlockSpec((seg, LANES), m)
    rows = lambda m: pl.BlockSpec((nchunk, SUBLANES, CHUNK), m)
    bf = jnp.bfloat16
    return pl.pallas_call(
        functools.partial(_dn_kernel, nchunk=nchunk, prep_chunks=prep_chunks),
        grid=(batch, nseg),
        in_specs=[big(fwd), big(fwd), big(fwd), tabs(fwd), rows(fwd3),
                  big(bwd), big(bwd), big(bwd), tabs(bwd), rows(bwd3)],
        out_specs=(big(fwd), big(bwd)),
        out_shape=(jax.ShapeDtypeStruct((n, DN_W), bf), jax.ShapeDtypeStruct((n, DN_W), bf)),
        scratch_shapes=[
            pltpu.VMEM((2 * N_DN_HEADS, DN_DIM, DN_DIM), jnp.float32),
            pltpu.VMEM((units, 2 * CHUNK, DN_DIM), bf),
            pltpu.VMEM((units, CHUNK, DN_DIM), jnp.float32),
            pltpu.VMEM((units, CHUNK, CHUNK), bf),
            pltpu.VMEM((units, DN_DIM, CHUNK), bf),
        ],
        compiler_params=pltpu.CompilerParams(dimension_semantics=("arbitrary", "arbitrary"),
                                             vmem_limit_bytes=VMEM_LIMIT),
        name="deltanet",
    )(dq, dk, dv, gcol, grow, dq, dk, dv, gcol, grow)


def _mix_kernel(x_ref, ao_ref, of_ref, ob_ref, z_ref, dnw_ref, wo_ref, n2_ref, x1_ref, h2_ref):
    dnw = dnw_ref[...]
    parts = []
    for hh in range(N_DN_HEADS):
        sl = slice(hh * DN_DIM, (hh + 1) * DN_DIM)
        o = of_ref[:, sl].astype(jnp.float32) + ob_ref[:, sl].astype(jnp.float32)
        z = z_ref[:, sl].astype(jnp.float32)
        parts.append((_rms_rows(o, dnw) * _silu(z)).astype(jnp.bfloat16))
    dn = jnp.concatenate(parts, axis=1)
    y = jnp.dot(ao_ref[...], wo_ref[0:ATTN_Q_W, :], preferred_element_type=jnp.float32)
    y = y + jnp.dot(dn, wo_ref[ATTN_Q_W:ATTN_Q_W + DN_W, :], preferred_element_type=jnp.float32)
    x1 = x_ref[...] + y
    x1_ref[...] = x1
    h2_ref[...] = _rms_rows(x1, n2_ref[...]).astype(jnp.bfloat16)


def _mix_out(x2, ao, o_f, o_b, dz, dn_norm, w_out, n2, *, tm):
    n = x2.shape[0]
    tok = lambda i: (i, 0)
    const = lambda i: (0, 0)
    half = pl.BlockSpec((tm, DN_W), tok)
    return pl.pallas_call(
        _mix_kernel,
        grid=(n // tm,),
        in_specs=[pl.BlockSpec((tm, D_MODEL), tok), half, half, half, half,
                  pl.BlockSpec((1, DN_DIM), const),
                  pl.BlockSpec((ATTN_Q_W + DN_W, D_MODEL), const),
                  pl.BlockSpec((1, D_MODEL), const)],
        out_specs=(pl.BlockSpec((tm, D_MODEL), tok), pl.BlockSpec((tm, D_MODEL), tok)),
        out_shape=(jax.ShapeDtypeStruct((n, D_MODEL), jnp.float32),
                   jax.ShapeDtypeStruct((n, D_MODEL), jnp.bfloat16)),
        compiler_params=pltpu.CompilerParams(dimension_semantics=("arbitrary",),
                                             vmem_limit_bytes=VMEM_LIMIT),
        name="mix_out",
    )(x2, ao, o_f, o_b, dz, dn_norm, w_out, n2)


def _ffn_kernel(hp_ref, h_ref, hn_ref, x1_ref, wg_ref, wu_ref, cwg_ref, cwu_ref, bg_ref, bu_ref,
                wd_ref, y_ref, acc_ref, *, tm, fc, tiles_per_seq):
    i = pl.program_id(0)
    first = (i % tiles_per_seq) == 0
    last = (i % tiles_per_seq) == tiles_per_seq - 1
    he = jnp.concatenate([hp_ref[...], h_ref[...], hn_ref[...]], axis=0)
    rows = tm + 2 * HALO_BF16
    row = lax.broadcasted_iota(jnp.int32, (rows, 1), 0)
    halo_zero = jnp.logical_or(jnp.logical_and(row < HALO_BF16, first),
                               jnp.logical_and(row >= tm + HALO_BF16, last))
    keep = jnp.where(halo_zero, 0.0, 1.0)

    def conv(u, cw_ref, b_ref, sl):
        u = u * keep
        prev = pltpu.roll(u, 1, axis=0)[HALO_BF16:HALO_BF16 + tm]
        nxt = pltpu.roll(u, rows - 1, axis=0)[HALO_BF16:HALO_BF16 + tm]
        mid = u[HALO_BF16:HALO_BF16 + tm]
        return prev * cw_ref[0:1, sl] + mid * cw_ref[1:2, sl] + nxt * cw_ref[2:3, sl] + b_ref[0:1, sl]

    for c in range(D_FF_PAD // fc):
        sl = slice(c * fc, (c + 1) * fc)
        ug = jnp.dot(he, wg_ref[:, sl], preferred_element_type=jnp.float32)
        uu = jnp.dot(he, wu_ref[:, sl], preferred_element_type=jnp.float32)
        act = (_silu(conv(ug, cwg_ref, bg_ref, sl)) * conv(uu, cwu_ref, bu_ref, sl)).astype(jnp.bfloat16)
        part = jnp.dot(act, wd_ref[sl, :], preferred_element_type=jnp.float32)
        if c == 0:
            acc_ref[...] = part
        else:
            acc_ref[...] += part
    y_ref[...] = x1_ref[...] + acc_ref[...]


def _ffn(h2, x1, seq_len, wg, wu, cwg, cwu, bg, bu, wd, *, tm, fc):
    n = h2.shape[0]
    tiles_per_seq = seq_len // tm
    nblk = n // HALO_BF16
    tpb = tm // HALO_BF16
    tok = lambda i: (i, 0)
    const = lambda i: (0, 0)
    return pl.pallas_call(
        functools.partial(_ffn_kernel, tm=tm, fc=fc, tiles_per_seq=tiles_per_seq),
        grid=(n // tm,),
        in_specs=[
            pl.BlockSpec((HALO_BF16, D_MODEL), lambda i: (jnp.maximum(i * tpb - 1, 0), 0)),
            pl.BlockSpec((tm, D_MODEL), tok),
            pl.BlockSpec((HALO_BF16, D_MODEL), lambda i: (jnp.minimum((i + 1) * tpb, nblk - 1), 0)),
            pl.BlockSpec((tm, D_MODEL), tok),
            pl.BlockSpec((D_MODEL, D_FF_PAD), const),
            pl.BlockSpec((D_MODEL, D_FF_PAD), const),
            pl.BlockSpec((3, D_FF_PAD), const),
            pl.BlockSpec((3, D_FF_PAD), const),
            pl.BlockSpec((1, D_FF_PAD), const),
            pl.BlockSpec((1, D_FF_PAD), const),
            pl.BlockSpec((D_FF_PAD, D_MODEL), const),
        ],
        out_specs=pl.BlockSpec((tm, D_MODEL), tok),
        out_shape=jax.ShapeDtypeStruct((n, D_MODEL), jnp.float32),
        scratch_shapes=[pltpu.VMEM((tm, D_MODEL), jnp.float32)],
        compiler_params=pltpu.CompilerParams(dimension_semantics=("arbitrary",),
                                             vmem_limit_bytes=56 * 1024 * 1024),
        name="ffn",
    )(h2, h2, h2, x1, wg, wu, cwg, cwu, bg, bu, wd)


def _rope_tables(seq_len):
    half = HEAD_DIM // 2
    inv = ROPE_THETA ** (-jnp.arange(0, half, 2, dtype=jnp.float32) / half)
    t = jnp.arange(seq_len, dtype=jnp.int32)
    rows = (t // GRID_W).astype(jnp.float32)[:, None] * inv[None, :]
    cols = (t % GRID_W).astype(jnp.float32)[:, None] * inv[None, :]
    cos = jnp.concatenate([jnp.cos(rows), jnp.cos(rows), jnp.cos(cols), jnp.cos(cols)], axis=1)
    sin = jnp.concatenate([-jnp.sin(rows), jnp.sin(rows), -jnp.sin(cols), jnp.sin(cols)], axis=1)
    return cos, sin


def _pad_cols(a, width):
    return jnp.pad(a, ((0, 0), (0, width - a.shape[1])))


def _prepare(norm1_w, w_in, dn_conv_w, dn_A_log_f, dn_A_log_b, dn_dt_bias_f, dn_dt_bias_b, dn_norm_w,
             attn_q_norm_w, attn_k_norm_w, attn_out_norm_w, w_out, norm2_w, w_ffn_in, ffn_conv_w,
             ffn_conv_b, w_ffn_out):
    bf = jnp.bfloat16
    gate_p = jnp.zeros((SUBLANES, LANES), jnp.float32)
    gate_p = gate_p.at[0, 0:4].set(dn_A_log_f).at[0, 4:8].set(dn_A_log_b)
    gate_p = gate_p.at[1, 0:4].set(dn_dt_bias_f).at[1, 4:8].set(dn_dt_bias_b)
    return dict(
        n1=norm1_w.reshape(1, D_MODEL),
        w_in=_pad_cols(w_in, IN_W_PAD).astype(bf),
        conv_w=dn_conv_w,
        gate_p=gate_p,
        dn_norm=dn_norm_w.reshape(1, DN_DIM),
        qn=attn_q_norm_w.reshape(1, HEAD_DIM),
        kn=attn_k_norm_w.reshape(1, HEAD_DIM),
        on=attn_out_norm_w.reshape(HEAD_DIM, 1),
        w_out=w_out.astype(bf),
        n2=norm2_w.reshape(1, D_MODEL),
        wg=_pad_cols(w_ffn_in[:, :D_FF], D_FF_PAD).astype(bf),
        wu=_pad_cols(w_ffn_in[:, D_FF:], D_FF_PAD).astype(bf),
        cwg=_pad_cols(ffn_conv_w[:, :D_FF], D_FF_PAD),
        cwu=_pad_cols(ffn_conv_w[:, D_FF:], D_FF_PAD),
        bg=_pad_cols(ffn_conv_b[None, :D_FF], D_FF_PAD),
        bu=_pad_cols(ffn_conv_b[None, D_FF:], D_FF_PAD),
        wd=jnp.pad(w_ffn_out, ((0, D_FF_PAD - D_FF), (0, 0))).astype(bf),
    )


def _tiles(seq_len):
    return dict(
        tm_in=min(512, seq_len),
        tq=min(2048, seq_len),
        tk=min(1024, seq_len),
        seg=min(512, seq_len),
        prep_chunks=2,
        tm_mix=min(512, seq_len),
        tm_ffn=min(512, seq_len),
        fc=256,
    )


def _layer(x, p):
    batch, seq_len, _ = x.shape
    t = _tiles(seq_len)
    x2 = x.reshape(batch * seq_len, D_MODEL)
    cos, sin = _rope_tables(seq_len)
    aq, ak, avt, dq, dk, dv, dz, gcol, grow = _in_proj(
        x2, seq_len, p["n1"], p["w_in"], cos, sin, p["qn"], p["kn"], p["conv_w"], p["gate_p"],
        tm=t["tm_in"])
    ao = _attention(aq, ak, avt, p["on"], batch, seq_len, tq=t["tq"], tk=t["tk"])
    o_f, o_b = _deltanet(dq, dk, dv, gcol, grow, batch, seq_len, seg=t["seg"],
                         prep_chunks=t["prep_chunks"])
    x1, h2 = _mix_out(x2, ao, o_f, o_b, dz, p["dn_norm"], p["w_out"], p["n2"], tm=t["tm_mix"])
    y = _ffn(h2, x1, seq_len, p["wg"], p["wu"], p["cwg"], p["cwu"], p["bg"], p["bu"], p["wd"],
             tm=t["tm_ffn"], fc=t["fc"])
    return y.reshape(batch, seq_len, D_MODEL)


def kernel(x_prompt, x_sample, norm1_w, w_in, dn_conv_w, dn_A_log_f, dn_A_log_b, dn_dt_bias_f, dn_dt_bias_b, dn_norm_w, attn_q_norm_w, attn_k_norm_w, attn_out_norm_w, w_out, norm2_w, w_ffn_in, ffn_conv_w, ffn_conv_b, w_ffn_out):
    depth = norm1_w.shape[0]

    def trunk(x):
        for l in range(depth):
            p = _prepare(norm1_w[l], w_in[l], dn_conv_w[l], dn_A_log_f[l], dn_A_log_b[l],
                         dn_dt_bias_f[l], dn_dt_bias_b[l], dn_norm_w[l], attn_q_norm_w[l],
                         attn_k_norm_w[l], attn_out_norm_w[l], w_out[l], norm2_w[l], w_ffn_in[l],
                         ffn_conv_w[l], ffn_conv_b[l], w_ffn_out[l])
            x = _layer(x, p)
        return x

    return (trunk(x_prompt), trunk(x_sample))
```

```python
import functools
import math

import jax
import jax.numpy as jnp
from jax import lax
from jax.experimental import pallas as pl
from jax.experimental.pallas import tpu as pltpu

D_MODEL = 1024
GRID_W = 64
HEAD_DIM = 128
N_ATTN_HEADS = 4
N_KV_HEADS = 2
N_DN_HEADS = 4
DN_DIM = 128
DN_CONV_K = 5
CHUNK = 64
ROPE_THETA = 10000.0
D_FF = 2752
EPS = 1e-6

ATTN_Q_W = N_ATTN_HEADS * HEAD_DIM
ATTN_KV_W = N_KV_HEADS * HEAD_DIM
DN_W = N_DN_HEADS * DN_DIM
IN_W = ATTN_Q_W + 2 * ATTN_KV_W + 4 * DN_W + 4 * N_DN_HEADS

LANES = 128
SUBLANES = 8
IN_W_PAD = 3200
D_FF_PAD = 2816
GATE_OFF = ATTN_Q_W + 2 * ATTN_KV_W + 4 * DN_W
HALO = SUBLANES
HALO_BF16 = 2 * SUBLANES

VMEM_LIMIT = 48 * 1024 * 1024

GT_G, GT_BETA, GT_EG, GT_EGL, GT_GL = 0, 8, 16, 24, 32

SUM_ROWS = 2 * SUBLANES
ATTN_UNROLL = 4


def _sigmoid(x):
    return 1.0 / (1.0 + jnp.exp(-x))


def _silu(x):
    return x * _sigmoid(x)


def _softplus(x):
    return jnp.maximum(x, 0.0) + jnp.log(1.0 + jnp.exp(-jnp.abs(x)))


def _rms_rows(x, w):
    return x * lax.rsqrt(jnp.mean(x * x, axis=-1, keepdims=True) + EPS) * w


def _aligned(x, m):
    return x if isinstance(x, int) else pl.multiple_of(x, m)


def _in_proj_kernel(xp_ref, x_ref, xn_ref, n1_ref, w_ref, cos_ref, sin_ref, qn_ref, kn_ref,
                    cw_ref, gp_ref,
                    aq_ref, ak_ref, avt_ref, dq_ref, dk_ref, dv_ref, dz_ref, gcol_ref, grow_ref,
                    ext_ref, *, tm, tiles_per_seq):
    i = pl.program_id(0)
    first = (i % tiles_per_seq) == 0
    last = (i % tiles_per_seq) == tiles_per_seq - 1

    n1 = n1_ref[...]
    xe = jnp.concatenate([xp_ref[...], x_ref[...], xn_ref[...]], axis=0)
    hf = _rms_rows(xe, n1)
    he = hf.astype(jnp.bfloat16)
    h = hf[HALO:HALO + tm].astype(jnp.bfloat16)

    cos = cos_ref[...]
    sin = sin_ref[...]
    lane = lax.broadcasted_iota(jnp.int32, (tm, LANES), 1)
    low_half = (lane % 64) < 32

    def rope(xh):
        perm = jnp.where(low_half, pltpu.roll(xh, 96, axis=1), pltpu.roll(xh, 32, axis=1))
        return xh * cos + perm * sin

    pa = jnp.dot(h, w_ref[:, 0:ATTN_Q_W + 2 * ATTN_KV_W], preferred_element_type=jnp.float32)
    row = lax.broadcasted_iota(jnp.int32, (tm + 2 * HALO, 1), 0)
    halo_zero = jnp.logical_or(jnp.logical_and(row < HALO, first),
                               jnp.logical_and(row >= tm + HALO, last))
    keep = jnp.where(halo_zero, 0.0, 1.0)
    dn_off = ATTN_Q_W + 2 * ATTN_KV_W
    for part in range(3):
        c0 = dn_off + part * DN_W
        pd = jnp.dot(he, w_ref[:, c0:c0 + DN_W], preferred_element_type=jnp.float32)
        ext_ref[:, part * DN_W:(part + 1) * DN_W] = pd * keep
    z0 = dn_off + 3 * DN_W
    pz = jnp.dot(h, w_ref[:, z0:z0 + DN_W], preferred_element_type=jnp.float32)
    pg = jnp.dot(h, w_ref[:, GATE_OFF:GATE_OFF + LANES], preferred_element_type=jnp.float32)

    qn = qn_ref[...]
    kn = kn_ref[...]
    scale = (HEAD_DIM ** -0.5) * math.log2(math.e)
    for hh in range(N_ATTN_HEADS):
        xh = pa[:, hh * HEAD_DIM:(hh + 1) * HEAD_DIM]
        aq_ref[:, hh * HEAD_DIM:(hh + 1) * HEAD_DIM] = (rope(_rms_rows(xh, qn)) * scale).astype(jnp.bfloat16)
    for hh in range(N_KV_HEADS):
        o = ATTN_Q_W + hh * HEAD_DIM
        xh = pa[:, o:o + HEAD_DIM]
        ak_ref[:, hh * HEAD_DIM:(hh + 1) * HEAD_DIM] = rope(_rms_rows(xh, kn)).astype(jnp.bfloat16)
    av = pa[:, ATTN_Q_W + ATTN_KV_W:ATTN_Q_W + 2 * ATTN_KV_W]
    avt_ref[...] = av.T.astype(jnp.bfloat16)

    pad = DN_CONV_K // 2
    out_refs = (dq_ref, dk_ref, dv_ref)
    for part in range(3):
        for hh in range(N_DN_HEADS):
            c0 = part * DN_W + hh * DN_DIM
            acc = None
            for j in range(DN_CONV_K):
                term = ext_ref[pl.ds(HALO - pad + j, tm), c0:c0 + DN_DIM] * cw_ref[j:j + 1, c0:c0 + DN_DIM]
                acc = term if acc is None else acc + term
            y = _silu(acc)
            if part < 2:
                y = y * lax.rsqrt(jnp.sum(y * y, axis=-1, keepdims=True) + EPS)
            if part == 0:
                y = y * (DN_DIM ** -0.5)
            out_refs[part][:, hh * DN_DIM:(hh + 1) * DN_DIM] = y.astype(jnp.bfloat16)

    dz_ref[...] = pz.astype(jnp.bfloat16)

    a_log = gp_ref[0:1, :]
    dt_bias = gp_ref[1:2, :]
    g = -jnp.exp(a_log) * _softplus(pg + dt_bias)
    beta = _sigmoid(pg)
    rin = lax.broadcasted_iota(jnp.int32, (tm, LANES), 0) % CHUNK
    pre = g
    suf = g
    s = 1
    while s < CHUNK:
        pre = pre + jnp.where(rin >= s, pltpu.roll(pre, s, axis=0), 0.0)
        suf = suf + jnp.where(rin < CHUNK - s, pltpu.roll(suf, tm - s, axis=0), 0.0)
        s *= 2
    tot = pre + suf - g
    gcum = jnp.where(lane < N_DN_HEADS, pre, suf)
    tab = jnp.where(lane < GT_BETA, gcum, 0.0)
    tab = jnp.where(jnp.logical_and(lane >= GT_BETA, lane < GT_EG), beta, tab)
    tab = jnp.where(jnp.logical_and(lane >= GT_EG, lane < GT_EGL),
                    pltpu.roll(jnp.exp(gcum), GT_EG, axis=1), tab)
    tab = jnp.where(jnp.logical_and(lane >= GT_EGL, lane < GT_GL),
                    pltpu.roll(jnp.exp(tot - gcum), GT_EGL, axis=1), tab)
    tab = jnp.where(jnp.logical_and(lane >= GT_GL, lane < GT_GL + 8),
                    pltpu.roll(jnp.exp(tot), GT_GL, axis=1), tab)
    gcol_ref[...] = tab
    gt = jnp.where(lane < GT_BETA, gcum, 0.0).T
    for c in range(tm // CHUNK):
        grow_ref[c] = gt[0:SUBLANES, c * CHUNK:(c + 1) * CHUNK]


def _in_proj(x2, seq_len, n1, w_in, cos, sin, qn, kn, conv_w, gate_p, *, tm):
    n = x2.shape[0]
    tiles_per_seq = seq_len // tm
    nblk8 = n // HALO
    tpb = tm // HALO
    const = lambda i: (0, 0)
    tok = lambda i: (i, 0)
    bf = jnp.bfloat16
    out_shape = (
        jax.ShapeDtypeStruct((n, ATTN_Q_W), bf),
        jax.ShapeDtypeStruct((n, ATTN_KV_W), bf),
        jax.ShapeDtypeStruct((ATTN_KV_W, n), bf),
        jax.ShapeDtypeStruct((n, DN_W), bf),
        jax.ShapeDtypeStruct((n, DN_W), bf),
        jax.ShapeDtypeStruct((n, DN_W), bf),
        jax.ShapeDtypeStruct((n, DN_W), bf),
        jax.ShapeDtypeStruct((n, LANES), jnp.float32),
        jax.ShapeDtypeStruct((n // CHUNK, SUBLANES, CHUNK), jnp.float32),
    )
    out_specs = (
        pl.BlockSpec((tm, ATTN_Q_W), tok),
        pl.BlockSpec((tm, ATTN_KV_W), tok),
        pl.BlockSpec((ATTN_KV_W, tm), lambda i: (0, i)),
        pl.BlockSpec((tm, DN_W), tok),
        pl.BlockSpec((tm, DN_W), tok),
        pl.BlockSpec((tm, DN_W), tok),
        pl.BlockSpec((tm, DN_W), tok),
        pl.BlockSpec((tm, LANES), tok),
        pl.BlockSpec((tm // CHUNK, SUBLANES, CHUNK), lambda i: (i, 0, 0)),
    )
    in_specs = [
        pl.BlockSpec((HALO, D_MODEL), lambda i: (jnp.maximum(i * tpb - 1, 0), 0)),
        pl.BlockSpec((tm, D_MODEL), tok),
        pl.BlockSpec((HALO, D_MODEL), lambda i: (jnp.minimum((i + 1) * tpb, nblk8 - 1), 0)),
        pl.BlockSpec((1, D_MODEL), const),
        pl.BlockSpec((D_MODEL, IN_W_PAD), const),
        pl.BlockSpec((tm, LANES), lambda i: (i % tiles_per_seq, 0)),
        pl.BlockSpec((tm, LANES), lambda i: (i % tiles_per_seq, 0)),
        pl.BlockSpec((1, HEAD_DIM), const),
        pl.BlockSpec((1, HEAD_DIM), const),
        pl.BlockSpec((DN_CONV_K, 3 * DN_W), const),
        pl.BlockSpec((SUBLANES, LANES), const),
    ]
    return pl.pallas_call(
        functools.partial(_in_proj_kernel, tm=tm, tiles_per_seq=tiles_per_seq),
        grid=(n // tm,),
        in_specs=in_specs,
        out_specs=out_specs,
        out_shape=out_shape,
        scratch_shapes=[pltpu.VMEM((tm + 2 * HALO, 3 * DN_W), jnp.float32)],
        compiler_params=pltpu.CompilerParams(dimension_semantics=("arbitrary",),
                                             vmem_limit_bytes=VMEM_LIMIT),
        name="in_proj",
    )(x2, x2, x2, n1, w_in, cos, sin, qn, kn, conv_w, gate_p)


def _attn_kernel(q_ref, k_ref, vt_ref, on_ref, o_ref, s_ref, p_ref, al_ref, cm_ref, m_ref, acc_ref,
                 *, tq, tk, seq_len):
    g = N_ATTN_HEADS // N_KV_HEADS
    nqt = tq // LANES
    nkv = seq_len // tk
    items = nqt * nkv
    kv_shift = nkv.bit_length() - 1
    ones = jnp.ones((SUM_ROWS, tk), jnp.bfloat16)

    def split(t):
        if isinstance(t, int):
            return t // nkv, t % nkv
        return lax.shift_right_logical(t, kv_shift), jnp.bitwise_and(t, nkv - 1)

    def stage_qk(t, slot):
        qi, kj = split(t)
        r0 = _aligned(qi * LANES, LANES)
        k0 = _aligned(kj * tk, tk)
        qblk = q_ref[pl.ds(r0, LANES), :]
        qs = jnp.concatenate([qblk[:, j * HEAD_DIM:(j + 1) * HEAD_DIM] for j in range(g)], axis=0)
        kb = k_ref[pl.ds(k0, tk), :]
        s = lax.dot_general(kb, qs, (((1,), (1,)), ((), ())),
                            preferred_element_type=jnp.float32)
        s_ref[slot] = s
        cm_ref[slot] = jnp.max(s, axis=0, keepdims=True)

    def stage_softmax(t, slot):
        _, kj = split(t)
        m_prev = jnp.where(kj == 0, -jnp.inf, m_ref[...])
        m_new = jnp.maximum(m_prev, cm_ref[slot])
        al_ref[slot] = jnp.exp2(m_prev - m_new)
        p_ref[slot] = jnp.exp2(s_ref[slot] - m_new).astype(jnp.bfloat16)
        m_ref[...] = m_new

    def stage_pv(t, slot, may_finish):
        qi, kj = split(t)
        k0 = _aligned(kj * tk, tk)
        lhs = jnp.concatenate([vt_ref[:, pl.ds(k0, tk)], ones], axis=0)
        acc = acc_ref[...] * al_ref[slot] + jnp.dot(lhs, p_ref[slot],
                                                    preferred_element_type=jnp.float32)
        acc_ref[...] = acc

        def finalize():
            o = acc[0:HEAD_DIM] / acc[HEAD_DIM:HEAD_DIM + 1]
            o = o * lax.rsqrt(jnp.mean(o * o, axis=0, keepdims=True) + EPS) * on_ref[...]
            ot = o.T
            r0 = _aligned(qi * LANES, LANES)
            for j in range(g):
                o_ref[pl.ds(r0, LANES), j * HEAD_DIM:(j + 1) * HEAD_DIM] = (
                    ot[j * LANES:(j + 1) * LANES, :].astype(jnp.bfloat16))

        if not may_finish:
            return
        if isinstance(kj, int):
            if kj == nkv - 1:
                finalize()
        else:
            pl.when(kj == nkv - 1)(finalize)

    def run_body(u):
        static = isinstance(u, int)
        for j in range(ATTN_UNROLL):
            if not static or u + 2 + j < items:
                stage_qk(u + 2 + j, (2 + j) % ATTN_UNROLL)
            if not static or u + 1 + j < items:
                stage_softmax(u + 1 + j, (1 + j) % ATTN_UNROLL)
            stage_pv(u + j, j, may_finish=(nkv == 1 or j == ATTN_UNROLL - 1))

    acc_ref[...] = jnp.zeros_like(acc_ref)
    m_ref[...] = jnp.full(m_ref.shape, -jnp.inf, jnp.float32)
    stage_qk(0, 0)
    stage_qk(1, 1)
    stage_softmax(0, 0)

    def body(i, carry):
        run_body(pl.multiple_of(i * ATTN_UNROLL, ATTN_UNROLL))
        return carry

    lax.fori_loop(0, items // ATTN_UNROLL - 1, body, 0)
    run_body(items - ATTN_UNROLL)


def _attention(aq, ak, avt, out_norm_col, batch, seq_len, *, tq, tk):
    n = aq.shape[0]
    g = N_ATTN_HEADS // N_KV_HEADS
    nq = g * LANES
    qpb = seq_len // tq
    nkv = seq_len // tk
    items = (tq // LANES) * nkv
    assert nkv == 1 or nkv % ATTN_UNROLL == 0
    assert nkv & (nkv - 1) == 0 and items % ATTN_UNROLL == 0 and items >= ATTN_UNROLL
    return pl.pallas_call(
        functools.partial(_attn_kernel, tq=tq, tk=tk, seq_len=seq_len),
        grid=(batch, N_KV_HEADS, qpb),
        in_specs=[
            pl.BlockSpec((tq, g * HEAD_DIM), lambda b, h, i: (b * qpb + i, h)),
            pl.BlockSpec((seq_len, HEAD_DIM), lambda b, h, i: (b, h)),
            pl.BlockSpec((HEAD_DIM, seq_len), lambda b, h, i: (h, b)),
            pl.BlockSpec((HEAD_DIM, 1), lambda b, h, i: (0, 0)),
        ],
        out_specs=pl.BlockSpec((tq, g * HEAD_DIM), lambda b, h, i: (b * qpb + i, h)),
        out_shape=jax.ShapeDtypeStruct((n, ATTN_Q_W), jnp.bfloat16),
        scratch_shapes=[
            pltpu.VMEM((ATTN_UNROLL, tk, nq), jnp.float32),
            pltpu.VMEM((ATTN_UNROLL, tk, nq), jnp.bfloat16),
            pltpu.VMEM((ATTN_UNROLL, 1, nq), jnp.float32),
            pltpu.VMEM((ATTN_UNROLL, 1, nq), jnp.float32),
            pltpu.VMEM((1, nq), jnp.float32),
            pltpu.VMEM((HEAD_DIM + SUM_ROWS, nq), jnp.float32),
        ],
        compiler_params=pltpu.CompilerParams(
            dimension_semantics=("arbitrary", "arbitrary", "arbitrary"),
            vmem_limit_bytes=56 * 1024 * 1024),
        name="attention",
    )(aq, ak, avt, out_norm_col)


def _dn_kernel(qf_ref, kf_ref, vf_ref, tf_ref, rf_ref, qb_ref, kb_ref, vb_ref, tb_ref, rb_ref,
               of_ref, ob_ref,
               s_ref, wq_ref, u_ref, at_ref, kdt_ref, *, nchunk, prep_chunks):
    seg = pl.program_id(1)

    @pl.when(seg == 0)
    def _():
        s_ref[...] = jnp.zeros_like(s_ref)

    ri = lax.broadcasted_iota(jnp.int32, (CHUNK, CHUNK), 0)
    ci = lax.broadcasted_iota(jnp.int32, (CHUNK, CHUNK), 1)
    eye = jnp.where(ri == ci, 1.0, 0.0)
    incl = (ri >= ci, ri <= ci)
    strict = (ri > ci, ri < ci)
    ins = ((qf_ref, kf_ref, vf_ref, tf_ref, rf_ref), (qb_ref, kb_ref, vb_ref, tb_ref, rb_ref))
    outs = (of_ref, ob_ref)
    bf = jnp.bfloat16
    cols = [(d, hh) for d in range(2) for hh in range(N_DN_HEADS)]

    def mm(a, b):
        return jnp.dot(a.astype(bf), b.astype(bf), preferred_element_type=jnp.float32)

    def prep(ci_, carry):
        units = []
        for pc in range(prep_chunks):
            c = ci_ * prep_chunks + pc
            r0 = pl.multiple_of(c * CHUNK, CHUNK)
            for d, hh in cols:
                q_ref, k_ref, v_ref, t_ref, g_ref = ins[d]
                col = d * N_DN_HEADS + hh
                lo = hh * DN_DIM
                tab = t_ref[pl.ds(r0, CHUNK), :]
                k = k_ref[pl.ds(r0, CHUNK), lo:lo + DN_DIM]
                q = q_ref[pl.ds(r0, CHUNK), lo:lo + DN_DIM]
                units.append(dict(
                    d=d, unit=col * nchunk + c, q=q, k=k,
                    v=v_ref[pl.ds(r0, CHUNK), lo:lo + DN_DIM].astype(jnp.float32),
                    gc=tab[:, GT_G + col:GT_G + col + 1],
                    beta=tab[:, GT_BETA + col:GT_BETA + col + 1],
                    eg=tab[:, GT_EG + col:GT_EG + col + 1],
                    egl=tab[:, GT_EGL + col:GT_EGL + col + 1],
                    gr=g_ref[c][col:col + 1, :]))
        for u in units:
            kq = jnp.concatenate([u["k"], u["q"]], axis=0)
            u["kkqk"] = lax.dot_general(kq, u["k"], (((1,), (1,)), ((), ())),
                                        preferred_element_type=jnp.float32)
        for u in units:
            d = u["d"]
            kf32 = u["k"].astype(jnp.float32)
            dec = jnp.exp(jnp.minimum(u["gc"] - u["gr"], 0.0))
            x = jnp.where(strict[d], -(u["beta"] * u["kkqk"][0:CHUNK]) * dec, 0.0)
            attn = jnp.where(incl[d], u["kkqk"][CHUNK:2 * CHUNK] * dec, 0.0)
            at_ref[u["unit"]] = attn.astype(bf)
            wq_ref[u["unit"], CHUNK:2 * CHUNK, :] = (u["q"].astype(jnp.float32) * u["eg"]).astype(bf)
            kdt_ref[u["unit"]] = (kf32 * u["egl"]).T.astype(bf)
            u["rhs"] = jnp.concatenate([u["v"] * u["beta"], kf32 * (u["beta"] * u["eg"])],
                                       axis=1).astype(bf)
            u["x"] = x
            u["ssum"] = eye + x
        for u in units:
            u["p"] = mm(u["x"], u["x"])
        for _ in range(4):
            for u in units:
                pp = mm(jnp.concatenate([u["p"], u["ssum"]], axis=0), u["p"])
                u["ssum"] = u["ssum"] + pp[CHUNK:2 * CHUNK]
                u["p"] = pp[0:CHUNK]
        for u in units:
            u["ssum"] = u["ssum"] + mm(u["ssum"], u["p"])
        for u in units:
            uw = jnp.dot(u["ssum"].astype(bf), u["rhs"], preferred_element_type=jnp.float32)
            u_ref[u["unit"]] = uw[:, 0:DN_DIM]
            wq_ref[u["unit"], 0:CHUNK, :] = uw[:, DN_DIM:2 * DN_DIM].astype(bf)
        return carry

    lax.fori_loop(0, nchunk // prep_chunks, prep, 0)

    def scan(step, carry):
        chains = []
        for d, hh in cols:
            c = step if d == 0 else nchunk - 1 - step
            r0 = pl.multiple_of(c * CHUNK, CHUNK)
            col = d * N_DN_HEADS + hh
            tabrow = ins[d][3][pl.ds(r0, 1), :]
            chains.append(dict(d=d, hh=hh, col=col, unit=col * nchunk + c, r0=r0,
                               gl=tabrow[:, GT_GL + col:GT_GL + col + 1]))
        for ch in chains:
            ch["st"] = s_ref[ch["col"]]
            ch["res"] = jnp.dot(wq_ref[ch["unit"]], ch["st"].astype(bf),
                                preferred_element_type=jnp.float32)
        for ch in chains:
            ch["vn"] = (u_ref[ch["unit"]] - ch["res"][0:CHUNK]).astype(bf)
        for ch in chains:
            ch["o"] = jnp.dot(at_ref[ch["unit"]], ch["vn"], preferred_element_type=jnp.float32)
            ch["upd"] = jnp.dot(kdt_ref[ch["unit"]], ch["vn"], preferred_element_type=jnp.float32)
        for ch in chains:
            hh = ch["hh"]
            o = ch["res"][CHUNK:2 * CHUNK] + ch["o"]
            outs[ch["d"]][pl.ds(ch["r0"], CHUNK), hh * DN_DIM:(hh + 1) * DN_DIM] = o.astype(bf)
            s_ref[ch["col"]] = ch["st"] * ch["gl"] + ch["upd"]
        return carry

    lax.fori_loop(0, nchunk, scan, 0)


def _deltanet(dq, dk, dv, gcol, grow, batch, seq_len, *, seg, prep_chunks):
    n = dq.shape[0]
    nseg = seq_len // seg
    nchunk = seg // CHUNK
    units = 2 * N_DN_HEADS * nchunk
    fwd = lambda b, s: (b * nseg + s, 0)
    bwd = lambda b, s: (b * nseg + nseg - 1 - s, 0)
    fwd3 = lambda b, s: (b * nseg + s, 0, 0)
    bwd3 = lambda b, s: (b * nseg + nseg - 1 - s, 0, 0)
    big = lambda m: pl.BlockSpec((seg, DN_W), m)
    tabs = lambda m: pl.BlockSpec((seg, LANES), m)
    rows = lambda m: pl.BlockSpec((nchunk, SUBLANES, CHUNK), m)
    bf = jnp.bfloat16
    return pl.pallas_call(
        functools.partial(_dn_kernel, nchunk=nchunk, prep_chunks=prep_chunks),
        grid=(batch, nseg),
        in_specs=[big(fwd), big(fwd), big(fwd), tabs(fwd), rows(fwd3),
                  big(bwd), big(bwd), big(bwd), tabs(bwd), rows(bwd3)],
        out_specs=(big(fwd), big(bwd)),
        out_shape=(jax.ShapeDtypeStruct((n, DN_W), bf), jax.ShapeDtypeStruct((n, DN_W), bf)),
        scratch_shapes=[
            pltpu.VMEM((2 * N_DN_HEADS, DN_DIM, DN_DIM), jnp.float32),
            pltpu.VMEM((units, 2 * CHUNK, DN_DIM), bf),
            pltpu.VMEM((units, CHUNK, DN_DIM), jnp.float32),
            pltpu.VMEM((units, CHUNK, CHUNK), bf),
            pltpu.VMEM((units, DN_DIM, CHUNK), bf),
        ],
        compiler_params=pltpu.CompilerParams(dimension_semantics=("arbitrary", "arbitrary"),
                                             vmem_limit_bytes=VMEM_LIMIT),
        name="deltanet",
    )(dq, dk, dv, gcol, grow, dq, dk, dv, gcol, grow)


def _mix_kernel(x_ref, ao_ref, of_ref, ob_ref, z_ref, dnw_ref, wo_ref, n2_ref, x1_ref, h2_ref):
    dnw = dnw_ref[...]
    parts = []
    for hh in range(N_DN_HEADS):
        sl = slice(hh * DN_DIM, (hh + 1) * DN_DIM)
        o = of_ref[:, sl].astype(jnp.float32) + ob_ref[:, sl].astype(jnp.float32)
        z = z_ref[:, sl].astype(jnp.float32)
        parts.append((_rms_rows(o, dnw) * _silu(z)).astype(jnp.bfloat16))
    dn = jnp.concatenate(parts, axis=1)
    y = jnp.dot(ao_ref[...], wo_ref[0:ATTN_Q_W, :], preferred_element_type=jnp.float32)
    y = y + jnp.dot(dn, wo_ref[ATTN_Q_W:ATTN_Q_W + DN_W, :], preferred_element_type=jnp.float32)
    x1 = x_ref[...] + y
    x1_ref[...] = x1
    h2_ref[...] = _rms_rows(x1, n2_ref[...]).astype(jnp.bfloat16)


def _mix_out(x2, ao, o_f, o_b, dz, dn_norm, w_out, n2, *, tm):
    n = x2.shape[0]
    tok = lambda i: (i, 0)
    const = lambda i: (0, 0)
    half = pl.BlockSpec((tm, DN_W), tok)
    return pl.pallas_call(
        _mix_kernel,
        grid=(n // tm,),
        in_specs=[pl.BlockSpec((tm, D_MODEL), tok), half, half, half, half,
                  pl.BlockSpec((1, DN_DIM), const),
                  pl.BlockSpec((ATTN_Q_W + DN_W, D_MODEL), const),
                  pl.BlockSpec((1, D_MODEL), const)],
        out_specs=(pl.BlockSpec((tm, D_MODEL), tok), pl.BlockSpec((tm, D_MODEL), tok)),
        out_shape=(jax.ShapeDtypeStruct((n, D_MODEL), jnp.float32),
                   jax.ShapeDtypeStruct((n, D_MODEL), jnp.bfloat16)),
        compiler_params=pltpu.CompilerParams(dimension_semantics=("arbitrary",),
                                             vmem_limit_bytes=VMEM_LIMIT),
        name="mix_out",
    )(x2, ao, o_f, o_b, dz, dn_norm, w_out, n2)


def _ffn_kernel(hp_ref, h_ref, hn_ref, x1_ref, wg_ref, wu_ref, cwg_ref, cwu_ref, bg_ref, bu_ref,
                wd_ref, y_ref, acc_ref, *, tm, fc, tiles_per_seq):
    i = pl.program_id(0)
    first = (i % tiles_per_seq) == 0
    last = (i % tiles_per_seq) == tiles_per_seq - 1
    he = jnp.concatenate([hp_ref[...], h_ref[...], hn_ref[...]], axis=0)
    rows = tm + 2 * HALO_BF16
    row = lax.broadcasted_iota(jnp.int32, (rows, 1), 0)
    halo_zero = jnp.logical_or(jnp.logical_and(row < HALO_BF16, first),
                               jnp.logical_and(row >= tm + HALO_BF16, last))
    keep = jnp.where(halo_zero, 0.0, 1.0)

    def conv(u, cw_ref, b_ref, sl):
        u = u * keep
        prev = pltpu.roll(u, 1, axis=0)[HALO_BF16:HALO_BF16 + tm]
        nxt = pltpu.roll(u, rows - 1, axis=0)[HALO_BF16:HALO_BF16 + tm]
        mid = u[HALO_BF16:HALO_BF16 + tm]
        return prev * cw_ref[0:1, sl] + mid * cw_ref[1:2, sl] + nxt * cw_ref[2:3, sl] + b_ref[0:1, sl]

    def up(c):
        sl = slice(c * fc, (c + 1) * fc)
        return (jnp.dot(he, wg_ref[:, sl], preferred_element_type=jnp.float32),
                jnp.dot(he, wu_ref[:, sl], preferred_element_type=jnp.float32))

    nchunks = D_FF_PAD // fc
    cur = up(0)
    for c in range(nchunks):
        nxt = up(c + 1) if c + 1 < nchunks else None
        sl = slice(c * fc, (c + 1) * fc)
        ug, uu = cur
        act = (_silu(conv(ug, cwg_ref, bg_ref, sl)) * conv(uu, cwu_ref, bu_ref, sl)).astype(jnp.bfloat16)
        part = jnp.dot(act, wd_ref[sl, :], preferred_element_type=jnp.float32)
        if c == 0:
            acc_ref[...] = part
        else:
            acc_ref[...] += part
        cur = nxt
    y_ref[...] = x1_ref[...] + acc_ref[...]


def _ffn(h2, x1, seq_len, wg, wu, cwg, cwu, bg, bu, wd, *, tm, fc):
    n = h2.shape[0]
    tiles_per_seq = seq_len // tm
    nblk = n // HALO_BF16
    tpb = tm // HALO_BF16
    tok = lambda i: (i, 0)
    const = lambda i: (0, 0)
    return pl.pallas_call(
        functools.partial(_ffn_kernel, tm=tm, fc=fc, tiles_per_seq=tiles_per_seq),
        grid=(n // tm,),
        in_specs=[
            pl.BlockSpec((HALO_BF16, D_MODEL), lambda i: (jnp.maximum(i * tpb - 1, 0), 0)),
            pl.BlockSpec((tm, D_MODEL), tok),
            pl.BlockSpec((HALO_BF16, D_MODEL), lambda i: (jnp.minimum((i + 1) * tpb, nblk - 1), 0)),
            pl.BlockSpec((tm, D_MODEL), tok),
            pl.BlockSpec((D_MODEL, D_FF_PAD), const),
            pl.BlockSpec((D_MODEL, D_FF_PAD), const),
            pl.BlockSpec((3, D_FF_PAD), const),
            pl.BlockSpec((3, D_FF_PAD), const),
            pl.BlockSpec((1, D_FF_PAD), const),
            pl.BlockSpec((1, D_FF_PAD), const),
            pl.BlockSpec((D_FF_PAD, D_MODEL), const),
        ],
        out_specs=pl.BlockSpec((tm, D_MODEL), tok),
        out_shape=jax.ShapeDtypeStruct((n, D_MODEL), jnp.float32),
        scratch_shapes=[pltpu.VMEM((tm, D_MODEL), jnp.float32)],
        compiler_params=pltpu.CompilerParams(dimension_semantics=("arbitrary",),
                                             vmem_limit_bytes=56 * 1024 * 1024),
        name="ffn",
    )(h2, h2, h2, x1, wg, wu, cwg, cwu, bg, bu, wd)


def _rope_tables(seq_len):
    half = HEAD_DIM // 2
    inv = ROPE_THETA ** (-jnp.arange(0, half, 2, dtype=jnp.float32) / half)
    t = jnp.arange(seq_len, dtype=jnp.int32)
    rows = (t // GRID_W).astype(jnp.float32)[:, None] * inv[None, :]
    cols = (t % GRID_W).astype(jnp.float32)[:, None] * inv[None, :]
    cos = jnp.concatenate([jnp.cos(rows), jnp.cos(rows), jnp.cos(cols), jnp.cos(cols)], axis=1)
    sin = jnp.concatenate([-jnp.sin(rows), jnp.sin(rows), -jnp.sin(cols), jnp.sin(cols)], axis=1)
    return cos, sin


def _pad_cols(a, width):
    return jnp.pad(a, ((0, 0), (0, width - a.shape[1])))


def _prepare(norm1_w, w_in, dn_conv_w, dn_A_log_f, dn_A_log_b, dn_dt_bias_f, dn_dt_bias_b, dn_norm_w,
             attn_q_norm_w, attn_k_norm_w, attn_out_norm_w, w_out, norm2_w, w_ffn_in, ffn_conv_w,
             ffn_conv_b, w_ffn_out):
    bf = jnp.bfloat16
    gate_p = jnp.zeros((SUBLANES, LANES), jnp.float32)
    gate_p = gate_p.at[0, 0:4].set(dn_A_log_f).at[0, 4:8].set(dn_A_log_b)
    gate_p = gate_p.at[1, 0:4].set(dn_dt_bias_f).at[1, 4:8].set(dn_dt_bias_b)
    return dict(
        n1=norm1_w.reshape(1, D_MODEL),
        w_in=_pad_cols(w_in, IN_W_PAD).astype(bf),
        conv_w=dn_conv_w,
        gate_p=gate_p,
        dn_norm=dn_norm_w.reshape(1, DN_DIM),
        qn=attn_q_norm_w.reshape(1, HEAD_DIM),
        kn=attn_k_norm_w.reshape(1, HEAD_DIM),
        on=attn_out_norm_w.reshape(HEAD_DIM, 1),
        w_out=w_out.astype(bf),
        n2=norm2_w.reshape(1, D_MODEL),
        wg=_pad_cols(w_ffn_in[:, :D_FF], D_FF_PAD).astype(bf),
        wu=_pad_cols(w_ffn_in[:, D_FF:], D_FF_PAD).astype(bf),
        cwg=_pad_cols(ffn_conv_w[:, :D_FF], D_FF_PAD),
        cwu=_pad_cols(ffn_conv_w[:, D_FF:], D_FF_PAD),
        bg=_pad_cols(ffn_conv_b[None, :D_FF], D_FF_PAD),
        bu=_pad_cols(ffn_conv_b[None, D_FF:], D_FF_PAD),
        wd=jnp.pad(w_ffn_out, ((0, D_FF_PAD - D_FF), (0, 0))).astype(bf),
    )


def _tiles(seq_len):
    return dict(
        tm_in=min(512, seq_len),
        tq=min(2048, seq_len),
        tk=min(2048, seq_len),
        seg=min(512, seq_len),
        prep_chunks=2,
        tm_mix=min(512, seq_len),
        tm_ffn=min(512, seq_len),
        fc=256,
    )


def _layer(x, p):
    batch, seq_len, _ = x.shape
    t = _tiles(seq_len)
    x2 = x.reshape(batch * seq_len, D_MODEL)
    cos, sin = _rope_tables(seq_len)
    aq, ak, avt, dq, dk, dv, dz, gcol, grow = _in_proj(
        x2, seq_len, p["n1"], p["w_in"], cos, sin, p["qn"], p["kn"], p["conv_w"], p["gate_p"],
        tm=t["tm_in"])
    ao = _attention(aq, ak, avt, p["on"], batch, seq_len, tq=t["tq"], tk=t["tk"])
    o_f, o_b = _deltanet(dq, dk, dv, gcol, grow, batch, seq_len, seg=t["seg"],
                         prep_chunks=t["prep_chunks"])
    x1, h2 = _mix_out(x2, ao, o_f, o_b, dz, p["dn_norm"], p["w_out"], p["n2"], tm=t["tm_mix"])
    y = _ffn(h2, x1, seq_len, p["wg"], p["wu"], p["cwg"], p["cwu"], p["bg"], p["bu"], p["wd"],
             tm=t["tm_ffn"], fc=t["fc"])
    return y.reshape(batch, seq_len, D_MODEL)


def kernel(x_prompt, x_sample, norm1_w, w_in, dn_conv_w, dn_A_log_f, dn_A_log_b, dn_dt_bias_f, dn_dt_bias_b, dn_norm_w, attn_q_norm_w, attn_k_norm_w, attn_out_norm_w, w_out, norm2_w, w_ffn_in, ffn_conv_w, ffn_conv_b, w_ffn_out):
    depth = norm1_w.shape[0]

    def trunk(x):
        for l in range(depth):
            p = _prepare(norm1_w[l], w_in[l], dn_conv_w[l], dn_A_log_f[l], dn_A_log_b[l],
                         dn_dt_bias_f[l], dn_dt_bias_b[l], dn_norm_w[l], attn_q_norm_w[l],
                         attn_k_norm_w[l], attn_out_norm_w[l], w_out[l], norm2_w[l], w_ffn_in[l],
                         ffn_conv_w[l], ffn_conv_b[l], w_ffn_out[l])
            x = _layer(x, p)
        return x

    return (trunk(x_prompt), trunk(x_sample))
```

```python
import functools
import math

import jax
import jax.numpy as jnp
from jax import lax
from jax.experimental import pallas as pl
from jax.experimental.pallas import tpu as pltpu

D_MODEL = 1024
GRID_W = 64
HEAD_DIM = 128
N_ATTN_HEADS = 4
N_KV_HEADS = 2
N_DN_HEADS = 4
DN_DIM = 128
DN_CONV_K = 5
CHUNK = 64
ROPE_THETA = 10000.0
D_FF = 2752
EPS = 1e-6

ATTN_Q_W = N_ATTN_HEADS * HEAD_DIM
ATTN_KV_W = N_KV_HEADS * HEAD_DIM
DN_W = N_DN_HEADS * DN_DIM
IN_W = ATTN_Q_W + 2 * ATTN_KV_W + 4 * DN_W + 4 * N_DN_HEADS

LANES = 128
SUBLANES = 8
IN_W_PAD = 3200
D_FF_PAD = 2816
GATE_OFF = ATTN_Q_W + 2 * ATTN_KV_W + 4 * DN_W
HALO = SUBLANES
HALO_BF16 = 2 * SUBLANES

VMEM_LIMIT = 48 * 1024 * 1024

GT_G, GT_BETA, GT_EG, GT_EGL, GT_GL = 0, 8, 16, 24, 32

SUM_ROWS = 2 * SUBLANES
ATTN_UNROLL = 4
SCORE_BOUND = 60.0


def _sigmoid(x):
    return 1.0 / (1.0 + jnp.exp(-x))


def _silu(x):
    return x * _sigmoid(x)


def _softplus(x):
    return jnp.maximum(x, 0.0) + jnp.log(1.0 + jnp.exp(-jnp.abs(x)))


def _rms_rows(x, w):
    return x * lax.rsqrt(jnp.mean(x * x, axis=-1, keepdims=True) + EPS) * w


def _aligned(x, m):
    return x if isinstance(x, int) else pl.multiple_of(x, m)


def _in_proj_kernel(xp_ref, x_ref, xn_ref, n1_ref, w_ref, cos_ref, sin_ref, qn_ref, kn_ref,
                    cw_ref, gp_ref,
                    aq_ref, ak_ref, avt_ref, dq_ref, dk_ref, dv_ref, dz_ref, gcol_ref, grow_ref,
                    ext_ref, *, tm, tiles_per_seq):
    i = pl.program_id(0)
    first = (i % tiles_per_seq) == 0
    last = (i % tiles_per_seq) == tiles_per_seq - 1

    n1 = n1_ref[...]
    xe = jnp.concatenate([xp_ref[...], x_ref[...], xn_ref[...]], axis=0)
    hf = _rms_rows(xe, n1)
    he = hf.astype(jnp.bfloat16)
    h = hf[HALO:HALO + tm].astype(jnp.bfloat16)

    cos = cos_ref[...]
    sin = sin_ref[...]
    lane = lax.broadcasted_iota(jnp.int32, (tm, LANES), 1)
    low_half = (lane % 64) < 32

    def rope(xh):
        perm = jnp.where(low_half, pltpu.roll(xh, 96, axis=1), pltpu.roll(xh, 32, axis=1))
        return xh * cos + perm * sin

    pa = jnp.dot(h, w_ref[:, 0:ATTN_Q_W + 2 * ATTN_KV_W], preferred_element_type=jnp.float32)
    row = lax.broadcasted_iota(jnp.int32, (tm + 2 * HALO, 1), 0)
    halo_zero = jnp.logical_or(jnp.logical_and(row < HALO, first),
                               jnp.logical_and(row >= tm + HALO, last))
    keep = jnp.where(halo_zero, 0.0, 1.0)
    dn_off = ATTN_Q_W + 2 * ATTN_KV_W
    for part in range(3):
        c0 = dn_off + part * DN_W
        pd = jnp.dot(he, w_ref[:, c0:c0 + DN_W], preferred_element_type=jnp.float32)
        ext_ref[:, part * DN_W:(part + 1) * DN_W] = pd * keep
    z0 = dn_off + 3 * DN_W
    pz = jnp.dot(h, w_ref[:, z0:z0 + DN_W], preferred_element_type=jnp.float32)
    pg = jnp.dot(h, w_ref[:, GATE_OFF:GATE_OFF + LANES], preferred_element_type=jnp.float32)

    qn = qn_ref[...]
    kn = kn_ref[...]
    scale = (HEAD_DIM ** -0.5) * math.log2(math.e)
    for hh in range(N_ATTN_HEADS):
        xh = pa[:, hh * HEAD_DIM:(hh + 1) * HEAD_DIM]
        aq_ref[:, hh * HEAD_DIM:(hh + 1) * HEAD_DIM] = (rope(_rms_rows(xh, qn)) * scale).astype(jnp.bfloat16)
    for hh in range(N_KV_HEADS):
        o = ATTN_Q_W + hh * HEAD_DIM
        xh = pa[:, o:o + HEAD_DIM]
        ak_ref[:, hh * HEAD_DIM:(hh + 1) * HEAD_DIM] = rope(_rms_rows(xh, kn)).astype(jnp.bfloat16)
    av = pa[:, ATTN_Q_W + ATTN_KV_W:ATTN_Q_W + 2 * ATTN_KV_W]
    avt_ref[...] = av.T.astype(jnp.bfloat16)

    pad = DN_CONV_K // 2
    out_refs = (dq_ref, dk_ref, dv_ref)
    for part in range(3):
        for hh in range(N_DN_HEADS):
            c0 = part * DN_W + hh * DN_DIM
            acc = None
            for j in range(DN_CONV_K):
                term = ext_ref[pl.ds(HALO - pad + j, tm), c0:c0 + DN_DIM] * cw_ref[j:j + 1, c0:c0 + DN_DIM]
                acc = term if acc is None else acc + term
            y = _silu(acc)
            if part < 2:
                y = y * lax.rsqrt(jnp.sum(y * y, axis=-1, keepdims=True) + EPS)
            if part == 0:
                y = y * (DN_DIM ** -0.5)
            out_refs[part][:, hh * DN_DIM:(hh + 1) * DN_DIM] = y.astype(jnp.bfloat16)

    dz_ref[...] = pz.astype(jnp.bfloat16)

    a_log = gp_ref[0:1, :]
    dt_bias = gp_ref[1:2, :]
    g = -jnp.exp(a_log) * _softplus(pg + dt_bias)
    beta = _sigmoid(pg)
    rin = lax.broadcasted_iota(jnp.int32, (tm, LANES), 0) % CHUNK
    pre = g
    suf = g
    s = 1
    while s < CHUNK:
        pre = pre + jnp.where(rin >= s, pltpu.roll(pre, s, axis=0), 0.0)
        suf = suf + jnp.where(rin < CHUNK - s, pltpu.roll(suf, tm - s, axis=0), 0.0)
        s *= 2
    tot = pre + suf - g
    gcum = jnp.where(lane < N_DN_HEADS, pre, suf)
    tab = jnp.where(lane < GT_BETA, gcum, 0.0)
    tab = jnp.where(jnp.logical_and(lane >= GT_BETA, lane < GT_EG), beta, tab)
    tab = jnp.where(jnp.logical_and(lane >= GT_EG, lane < GT_EGL),
                    pltpu.roll(jnp.exp(gcum), GT_EG, axis=1), tab)
    tab = jnp.where(jnp.logical_and(lane >= GT_EGL, lane < GT_GL),
                    pltpu.roll(jnp.exp(tot - gcum), GT_EGL, axis=1), tab)
    tab = jnp.where(jnp.logical_and(lane >= GT_GL, lane < GT_GL + 8),
                    pltpu.roll(jnp.exp(tot), GT_GL, axis=1), tab)
    gcol_ref[...] = tab
    gt = jnp.where(lane < GT_BETA, gcum, 0.0).T
    for c in range(tm // CHUNK):
        grow_ref[c] = gt[0:SUBLANES, c * CHUNK:(c + 1) * CHUNK]


def _in_proj(x2, seq_len, n1, w_in, cos, sin, qn, kn, conv_w, gate_p, *, tm):
    n = x2.shape[0]
    tiles_per_seq = seq_len // tm
    nblk8 = n // HALO
    tpb = tm // HALO
    const = lambda i: (0, 0)
    tok = lambda i: (i, 0)
    bf = jnp.bfloat16
    out_shape = (
        jax.ShapeDtypeStruct((n, ATTN_Q_W), bf),
        jax.ShapeDtypeStruct((n, ATTN_KV_W), bf),
        jax.ShapeDtypeStruct((ATTN_KV_W, n), bf),
        jax.ShapeDtypeStruct((n, DN_W), bf),
        jax.ShapeDtypeStruct((n, DN_W), bf),
        jax.ShapeDtypeStruct((n, DN_W), bf),
        jax.ShapeDtypeStruct((n, DN_W), bf),
        jax.ShapeDtypeStruct((n, LANES), jnp.float32),
        jax.ShapeDtypeStruct((n // CHUNK, SUBLANES, CHUNK), jnp.float32),
    )
    out_specs = (
        pl.BlockSpec((tm, ATTN_Q_W), tok),
        pl.BlockSpec((tm, ATTN_KV_W), tok),
        pl.BlockSpec((ATTN_KV_W, tm), lambda i: (0, i)),
        pl.BlockSpec((tm, DN_W), tok),
        pl.BlockSpec((tm, DN_W), tok),
        pl.BlockSpec((tm, DN_W), tok),
        pl.BlockSpec((tm, DN_W), tok),
        pl.BlockSpec((tm, LANES), tok),
        pl.BlockSpec((tm // CHUNK, SUBLANES, CHUNK), lambda i: (i, 0, 0)),
    )
    in_specs = [
        pl.BlockSpec((HALO, D_MODEL), lambda i: (jnp.maximum(i * tpb - 1, 0), 0)),
        pl.BlockSpec((tm, D_MODEL), tok),
        pl.BlockSpec((HALO, D_MODEL), lambda i: (jnp.minimum((i + 1) * tpb, nblk8 - 1), 0)),
        pl.BlockSpec((1, D_MODEL), const),
        pl.BlockSpec((D_MODEL, IN_W_PAD), const),
        pl.BlockSpec((tm, LANES), lambda i: (i % tiles_per_seq, 0)),
        pl.BlockSpec((tm, LANES), lambda i: (i % tiles_per_seq, 0)),
        pl.BlockSpec((1, HEAD_DIM), const),
        pl.BlockSpec((1, HEAD_DIM), const),
        pl.BlockSpec((DN_CONV_K, 3 * DN_W), const),
        pl.BlockSpec((SUBLANES, LANES), const),
    ]
    return pl.pallas_call(
        functools.partial(_in_proj_kernel, tm=tm, tiles_per_seq=tiles_per_seq),
        grid=(n // tm,),
        in_specs=in_specs,
        out_specs=out_specs,
        out_shape=out_shape,
        scratch_shapes=[pltpu.VMEM((tm + 2 * HALO, 3 * DN_W), jnp.float32)],
        compiler_params=pltpu.CompilerParams(dimension_semantics=("arbitrary",),
                                             vmem_limit_bytes=VMEM_LIMIT),
        name="in_proj",
    )(x2, x2, x2, n1, w_in, cos, sin, qn, kn, conv_w, gate_p)


def _attn_kernel(q_ref, k_ref, vt_ref, on_ref, o_ref, s_ref, p_ref, al_ref, cm_ref, m_ref, acc_ref,
                 *, tq, tk, seq_len):
    g = N_ATTN_HEADS // N_KV_HEADS
    nqt = tq // LANES
    nkv = seq_len // tk
    items = nqt * nkv
    kv_shift = nkv.bit_length() - 1
    ones = jnp.ones((SUM_ROWS, tk), jnp.bfloat16)

    def split(t):
        if isinstance(t, int):
            return t // nkv, t % nkv
        return lax.shift_right_logical(t, kv_shift), jnp.bitwise_and(t, nkv - 1)

    def scores(t):
        qi, kj = split(t)
        r0 = _aligned(qi * LANES, LANES)
        k0 = _aligned(kj * tk, tk)
        qblk = q_ref[pl.ds(r0, LANES), :]
        qs = jnp.concatenate([qblk[:, j * HEAD_DIM:(j + 1) * HEAD_DIM] for j in range(g)], axis=0)
        kb = k_ref[pl.ds(k0, tk), :]
        return lax.dot_general(kb, qs, (((1,), (1,)), ((), ())),
                               preferred_element_type=jnp.float32)

    def stage_qk(t, slot):
        s = scores(t)
        s_ref[slot] = s
        cm_ref[slot] = jnp.max(s, axis=0, keepdims=True)

    def stage_qk_exp(t, slot):
        p_ref[slot] = jnp.exp2(scores(t)).astype(jnp.bfloat16)

    def stage_softmax(t, slot):
        _, kj = split(t)
        m_prev = jnp.where(kj == 0, -jnp.inf, m_ref[...])
        m_new = jnp.maximum(m_prev, cm_ref[slot])
        al_ref[slot] = jnp.exp2(m_prev - m_new)
        p_ref[slot] = jnp.exp2(s_ref[slot] - m_new).astype(jnp.bfloat16)
        m_ref[...] = m_new

    def stage_pv(t, slot, may_finish, bounded):
        qi, kj = split(t)
        k0 = _aligned(kj * tk, tk)
        lhs = jnp.concatenate([vt_ref[:, pl.ds(k0, tk)], ones], axis=0)
        if bounded:
            prev = jnp.where(kj == 0, 0.0, acc_ref[...])
        else:
            prev = acc_ref[...] * al_ref[slot]
        acc = prev + jnp.dot(lhs, p_ref[slot], preferred_element_type=jnp.float32)
        acc_ref[...] = acc

        def finalize():
            o = acc[0:HEAD_DIM] / acc[HEAD_DIM:HEAD_DIM + 1]
            o = o * lax.rsqrt(jnp.mean(o * o, axis=0, keepdims=True) + EPS) * on_ref[...]
            ot = o.T
            r0 = _aligned(qi * LANES, LANES)
            for j in range(g):
                o_ref[pl.ds(r0, LANES), j * HEAD_DIM:(j + 1) * HEAD_DIM] = (
                    ot[j * LANES:(j + 1) * LANES, :].astype(jnp.bfloat16))

        if not may_finish:
            return
        if isinstance(kj, int):
            if kj == nkv - 1:
                finalize()
        else:
            pl.when(kj == nkv - 1)(finalize)

    def run_body(u, bounded):
        static = isinstance(u, int)
        for j in range(ATTN_UNROLL):
            may_finish = nkv == 1 or j == ATTN_UNROLL - 1
            if bounded:
                if not static or u + 1 + j < items:
                    stage_qk_exp(u + 1 + j, (1 + j) % ATTN_UNROLL)
            else:
                if not static or u + 2 + j < items:
                    stage_qk(u + 2 + j, (2 + j) % ATTN_UNROLL)
                if not static or u + 1 + j < items:
                    stage_softmax(u + 1 + j, (1 + j) % ATTN_UNROLL)
            stage_pv(u + j, j, may_finish, bounded)

    def pipeline(bounded):
        acc_ref[...] = jnp.zeros_like(acc_ref)
        if bounded:
            stage_qk_exp(0, 0)
        else:
            m_ref[...] = jnp.full(m_ref.shape, -jnp.inf, jnp.float32)
            stage_qk(0, 0)
            stage_qk(1, 1)
            stage_softmax(0, 0)

        def body(i, carry):
            run_body(pl.multiple_of(i * ATTN_UNROLL, ATTN_UNROLL), bounded)
            return carry

        lax.fori_loop(0, items // ATTN_UNROLL - 1, body, 0)
        run_body(items - ATTN_UNROLL, bounded)

    def max_sq_norm(ref, rows, heads):
        blk = min(rows, 512)

        def step(i, mx):
            r0 = pl.multiple_of(i * blk, blk)
            for j in range(heads):
                x = ref[pl.ds(r0, blk), j * HEAD_DIM:(j + 1) * HEAD_DIM].astype(jnp.float32)
                mx = jnp.maximum(mx, jnp.sum(x * x, axis=1, keepdims=True))
            return mx

        mx = lax.fori_loop(0, rows // blk, step, jnp.zeros((blk, 1), jnp.float32))
        return jnp.max(mx)

    bound_sq = max_sq_norm(q_ref, tq, g) * max_sq_norm(k_ref, seq_len, 1)
    bounded = bound_sq <= SCORE_BOUND * SCORE_BOUND
    pl.when(bounded)(lambda: pipeline(True))
    pl.when(jnp.logical_not(bounded))(lambda: pipeline(False))


def _attention(aq, ak, avt, out_norm_col, batch, seq_len, *, tq, tk):
    n = aq.shape[0]
    g = N_ATTN_HEADS // N_KV_HEADS
    nq = g * LANES
    qpb = seq_len // tq
    nkv = seq_len // tk
    items = (tq // LANES) * nkv
    assert nkv == 1 or nkv % ATTN_UNROLL == 0
    assert nkv & (nkv - 1) == 0 and items % ATTN_UNROLL == 0 and items >= ATTN_UNROLL
    return pl.pallas_call(
        functools.partial(_attn_kernel, tq=tq, tk=tk, seq_len=seq_len),
        grid=(batch, N_KV_HEADS, qpb),
        in_specs=[
            pl.BlockSpec((tq, g * HEAD_DIM), lambda b, h, i: (b * qpb + i, h)),
            pl.BlockSpec((seq_len, HEAD_DIM), lambda b, h, i: (b, h)),
            pl.BlockSpec((HEAD_DIM, seq_len), lambda b, h, i: (h, b)),
            pl.BlockSpec((HEAD_DIM, 1), lambda b, h, i: (0, 0)),
        ],
        out_specs=pl.BlockSpec((tq, g * HEAD_DIM), lambda b, h, i: (b * qpb + i, h)),
        out_shape=jax.ShapeDtypeStruct((n, ATTN_Q_W), jnp.bfloat16),
        scratch_shapes=[
            pltpu.VMEM((ATTN_UNROLL, tk, nq), jnp.float32),
            pltpu.VMEM((ATTN_UNROLL, tk, nq), jnp.bfloat16),
            pltpu.VMEM((ATTN_UNROLL, 1, nq), jnp.float32),
            pltpu.VMEM((ATTN_UNROLL, 1, nq), jnp.float32),
            pltpu.VMEM((1, nq), jnp.float32),
            pltpu.VMEM((HEAD_DIM + SUM_ROWS, nq), jnp.float32),
        ],
        compiler_params=pltpu.CompilerParams(
            dimension_semantics=("arbitrary", "arbitrary", "arbitrary"),
            vmem_limit_bytes=56 * 1024 * 1024),
        name="attention",
    )(aq, ak, avt, out_norm_col)


def _dn_kernel(qf_ref, kf_ref, vf_ref, tf_ref, rf_ref, qb_ref, kb_ref, vb_ref, tb_ref, rb_ref,
               of_ref, ob_ref,
               s_ref, wq_ref, u_ref, at_ref, kdt_ref, *, nchunk, prep_chunks):
    seg = pl.program_id(1)

    @pl.when(seg == 0)
    def _():
        s_ref[...] = jnp.zeros_like(s_ref)

    ri = lax.broadcasted_iota(jnp.int32, (CHUNK, CHUNK), 0)
    ci = lax.broadcasted_iota(jnp.int32, (CHUNK, CHUNK), 1)
    eye = jnp.where(ri == ci, 1.0, 0.0)
    incl = (ri >= ci, ri <= ci)
    strict = (ri > ci, ri < ci)
    ins = ((qf_ref, kf_ref, vf_ref, tf_ref, rf_ref), (qb_ref, kb_ref, vb_ref, tb_ref, rb_ref))
    outs = (of_ref, ob_ref)
    bf = jnp.bfloat16
    cols = [(d, hh) for d in range(2) for hh in range(N_DN_HEADS)]

    def mm(a, b):
        return jnp.dot(a.astype(bf), b.astype(bf), preferred_element_type=jnp.float32)

    def prep(ci_, carry):
        units = []
        for pc in range(prep_chunks):
            c = ci_ * prep_chunks + pc
            r0 = pl.multiple_of(c * CHUNK, CHUNK)
            for d, hh in cols:
                q_ref, k_ref, v_ref, t_ref, g_ref = ins[d]
                col = d * N_DN_HEADS + hh
                lo = hh * DN_DIM
                tab = t_ref[pl.ds(r0, CHUNK), :]
                k = k_ref[pl.ds(r0, CHUNK), lo:lo + DN_DIM]
                q = q_ref[pl.ds(r0, CHUNK), lo:lo + DN_DIM]
                units.append(dict(
                    d=d, unit=col * nchunk + c, q=q, k=k,
                    v=v_ref[pl.ds(r0, CHUNK), lo:lo + DN_DIM].astype(jnp.float32),
                    gc=tab[:, GT_G + col:GT_G + col + 1],
                    beta=tab[:, GT_BETA + col:GT_BETA + col + 1],
                    eg=tab[:, GT_EG + col:GT_EG + col + 1],
                    egl=tab[:, GT_EGL + col:GT_EGL + col + 1],
                    gr=g_ref[c][col:col + 1, :]))
        for u in units:
            kq = jnp.concatenate([u["k"], u["q"]], axis=0)
            u["kkqk"] = lax.dot_general(kq, u["k"], (((1,), (1,)), ((), ())),
                                        preferred_element_type=jnp.float32)
        for u in units:
            d = u["d"]
            kf32 = u["k"].astype(jnp.float32)
            dec = jnp.exp(jnp.minimum(u["gc"] - u["gr"], 0.0))
            x = jnp.where(strict[d], -(u["beta"] * u["kkqk"][0:CHUNK]) * dec, 0.0)
            attn = jnp.where(incl[d], u["kkqk"][CHUNK:2 * CHUNK] * dec, 0.0)
            at_ref[u["unit"]] = attn.astype(bf)
            wq_ref[u["unit"], CHUNK:2 * CHUNK, :] = (u["q"].astype(jnp.float32) * u["eg"]).astype(bf)
            kdt_ref[u["unit"]] = (kf32 * u["egl"]).T.astype(bf)
            u["rhs"] = jnp.concatenate([u["v"] * u["beta"], kf32 * (u["beta"] * u["eg"])],
                                       axis=1).astype(bf)
            u["x"] = x
            u["ssum"] = eye + x
        for u in units:
            u["p"] = mm(u["x"], u["x"])
        for _ in range(4):
            for u in units:
                pp = mm(jnp.concatenate([u["p"], u["ssum"]], axis=0), u["p"])
                u["ssum"] = u["ssum"] + pp[CHUNK:2 * CHUNK]
                u["p"] = pp[0:CHUNK]
        for u in units:
            u["ssum"] = u["ssum"] + mm(u["ssum"], u["p"])
        for u in units:
            uw = jnp.dot(u["ssum"].astype(bf), u["rhs"], preferred_element_type=jnp.float32)
            u_ref[u["unit"]] = uw[:, 0:DN_DIM]
            wq_ref[u["unit"], 0:CHUNK, :] = uw[:, DN_DIM:2 * DN_DIM].astype(bf)
        return carry

    lax.fori_loop(0, nchunk // prep_chunks, prep, 0)

    def scan(step, carry):
        chains = []
        for d, hh in cols:
            c = step if d == 0 else nchunk - 1 - step
            r0 = pl.multiple_of(c * CHUNK, CHUNK)
            col = d * N_DN_HEADS + hh
            tabrow = ins[d][3][pl.ds(r0, 1), :]
            chains.append(dict(d=d, hh=hh, col=col, unit=col * nchunk + c, r0=r0,
                               gl=tabrow[:, GT_GL + col:GT_GL + col + 1]))
        for ch in chains:
            ch["st"] = s_ref[ch["col"]]
            ch["res"] = jnp.dot(wq_ref[ch["unit"]], ch["st"].astype(bf),
                                preferred_element_type=jnp.float32)
        for ch in chains:
            ch["vn"] = (u_ref[ch["unit"]] - ch["res"][0:CHUNK]).astype(bf)
        for ch in chains:
            ch["o"] = jnp.dot(at_ref[ch["unit"]], ch["vn"], preferred_element_type=jnp.float32)
            ch["upd"] = jnp.dot(kdt_ref[ch["unit"]], ch["vn"], preferred_element_type=jnp.float32)
        for ch in chains:
            hh = ch["hh"]
            o = ch["res"][CHUNK:2 * CHUNK] + ch["o"]
            outs[ch["d"]][pl.ds(ch["r0"], CHUNK), hh * DN_DIM:(hh + 1) * DN_DIM] = o.astype(bf)
            s_ref[ch["col"]] = ch["st"] * ch["gl"] + ch["upd"]
        return carry

    lax.fori_loop(0, nchunk, scan, 0)


def _deltanet(dq, dk, dv, gcol, grow, batch, seq_len, *, seg, prep_chunks):
    n = dq.shape[0]
    nseg = seq_len // seg
    nchunk = seg // CHUNK
    units = 2 * N_DN_HEADS * nchunk
    fwd = lambda b, s: (b * nseg + s, 0)
    bwd = lambda b, s: (b * nseg + nseg - 1 - s, 0)
    fwd3 = lambda b, s: (b * nseg + s, 0, 0)
    bwd3 = lambda b, s: (b * nseg + nseg - 1 - s, 0, 0)
    big = lambda m: pl.BlockSpec((seg, DN_W), m)
    tabs = lambda m: pl.BlockSpec((seg, LANES), m)
    rows = lambda m: pl.BlockSpec((nchunk, SUBLANES, CHUNK), m)
    bf = jnp.bfloat16
    return pl.pallas_call(
        functools.partial(_dn_kernel, nchunk=nchunk, prep_chunks=prep_chunks),
        grid=(batch, nseg),
        in_specs=[big(fwd), big(fwd), big(fwd), tabs(fwd), rows(fwd3),
                  big(bwd), big(bwd), big(bwd), tabs(bwd), rows(bwd3)],
        out_specs=(big(fwd), big(bwd)),
        out_shape=(jax.ShapeDtypeStruct((n, DN_W), bf), jax.ShapeDtypeStruct((n, DN_W), bf)),
        scratch_shapes=[
            pltpu.VMEM((2 * N_DN_HEADS, DN_DIM, DN_DIM), jnp.float32),
            pltpu.VMEM((units, 2 * CHUNK, DN_DIM), bf),
            pltpu.VMEM((units, CHUNK, DN_DIM), jnp.float32),
            pltpu.VMEM((units, CHUNK, CHUNK), bf),
            pltpu.VMEM((units, DN_DIM, CHUNK), bf),
        ],
        compiler_params=pltpu.CompilerParams(dimension_semantics=("arbitrary", "arbitrary"),
                                             vmem_limit_bytes=VMEM_LIMIT),
        name="deltanet",
    )(dq, dk, dv, gcol, grow, dq, dk, dv, gcol, grow)


def _mix_kernel(x_ref, ao_ref, of_ref, ob_ref, z_ref, dnw_ref, wo_ref, n2_ref, x1_ref, h2_ref):
    dnw = dnw_ref[...]
    parts = []
    for hh in range(N_DN_HEADS):
        sl = slice(hh * DN_DIM, (hh + 1) * DN_DIM)
        o = of_ref[:, sl].astype(jnp.float32) + ob_ref[:, sl].astype(jnp.float32)
        z = z_ref[:, sl].astype(jnp.float32)
        parts.append((_rms_rows(o, dnw) * _silu(z)).astype(jnp.bfloat16))
    dn = jnp.concatenate(parts, axis=1)
    y = jnp.dot(ao_ref[...], wo_ref[0:ATTN_Q_W, :], preferred_element_type=jnp.float32)
    y = y + jnp.dot(dn, wo_ref[ATTN_Q_W:ATTN_Q_W + DN_W, :], preferred_element_type=jnp.float32)
    x1 = x_ref[...] + y
    x1_ref[...] = x1
    h2_ref[...] = _rms_rows(x1, n2_ref[...]).astype(jnp.bfloat16)


def _mix_out(x2, ao, o_f, o_b, dz, dn_norm, w_out, n2, *, tm):
    n = x2.shape[0]
    tok = lambda i: (i, 0)
    const = lambda i: (0, 0)
    half = pl.BlockSpec((tm, DN_W), tok)
    return pl.pallas_call(
        _mix_kernel,
        grid=(n // tm,),
        in_specs=[pl.BlockSpec((tm, D_MODEL), tok), half, half, half, half,
                  pl.BlockSpec((1, DN_DIM), const),
                  pl.BlockSpec((ATTN_Q_W + DN_W, D_MODEL), const),
                  pl.BlockSpec((1, D_MODEL), const)],
        out_specs=(pl.BlockSpec((tm, D_MODEL), tok), pl.BlockSpec((tm, D_MODEL), tok)),
        out_shape=(jax.ShapeDtypeStruct((n, D_MODEL), jnp.float32),
                   jax.ShapeDtypeStruct((n, D_MODEL), jnp.bfloat16)),
        compiler_params=pltpu.CompilerParams(dimension_semantics=("arbitrary",),
                                             vmem_limit_bytes=VMEM_LIMIT),
        name="mix_out",
    )(x2, ao, o_f, o_b, dz, dn_norm, w_out, n2)


def _ffn_kernel(hp_ref, h_ref, hn_ref, x1_ref, wg_ref, wu_ref, cwg_ref, cwu_ref, bg_ref, bu_ref,
                wd_ref, y_ref, acc_ref, *, tm, fc, tiles_per_seq):
    i = pl.program_id(0)
    first = (i % tiles_per_seq) == 0
    last = (i % tiles_per_seq) == tiles_per_seq - 1
    he = jnp.concatenate([hp_ref[...], h_ref[...], hn_ref[...]], axis=0)
    rows = tm + 2 * HALO_BF16
    row = lax.broadcasted_iota(jnp.int32, (rows, 1), 0)
    halo_zero = jnp.logical_or(jnp.logical_and(row < HALO_BF16, first),
                               jnp.logical_and(row >= tm + HALO_BF16, last))
    keep = jnp.where(halo_zero, 0.0, 1.0)

    def conv(u, cw_ref, b_ref, sl):
        u = u * keep
        prev = pltpu.roll(u, 1, axis=0)[HALO_BF16:HALO_BF16 + tm]
        nxt = pltpu.roll(u, rows - 1, axis=0)[HALO_BF16:HALO_BF16 + tm]
        mid = u[HALO_BF16:HALO_BF16 + tm]
        return prev * cw_ref[0:1, sl] + mid * cw_ref[1:2, sl] + nxt * cw_ref[2:3, sl] + b_ref[0:1, sl]

    def up(c):
        sl = slice(c * fc, (c + 1) * fc)
        return (jnp.dot(he, wg_ref[:, sl], preferred_element_type=jnp.float32),
                jnp.dot(he, wu_ref[:, sl], preferred_element_type=jnp.float32))

    nchunks = D_FF_PAD // fc
    cur = up(0)
    for c in range(nchunks):
        nxt = up(c + 1) if c + 1 < nchunks else None
        sl = slice(c * fc, (c + 1) * fc)
        ug, uu = cur
        act = (_silu(conv(ug, cwg_ref, bg_ref, sl)) * conv(uu, cwu_ref, bu_ref, sl)).astype(jnp.bfloat16)
        part = jnp.dot(act, wd_ref[sl, :], preferred_element_type=jnp.float32)
        if c == 0:
            acc_ref[...] = part
        else:
            acc_ref[...] += part
        cur = nxt
    y_ref[...] = x1_ref[...] + acc_ref[...]


def _ffn(h2, x1, seq_len, wg, wu, cwg, cwu, bg, bu, wd, *, tm, fc):
    n = h2.shape[0]
    tiles_per_seq = seq_len // tm
    nblk = n // HALO_BF16
    tpb = tm // HALO_BF16
    tok = lambda i: (i, 0)
    const = lambda i: (0, 0)
    return pl.pallas_call(
        functools.partial(_ffn_kernel, tm=tm, fc=fc, tiles_per_seq=tiles_per_seq),
        grid=(n // tm,),
        in_specs=[
            pl.BlockSpec((HALO_BF16, D_MODEL), lambda i: (jnp.maximum(i * tpb - 1, 0), 0)),
            pl.BlockSpec((tm, D_MODEL), tok),
            pl.BlockSpec((HALO_BF16, D_MODEL), lambda i: (jnp.minimum((i + 1) * tpb, nblk - 1), 0)),
            pl.BlockSpec((tm, D_MODEL), tok),
            pl.BlockSpec((D_MODEL, D_FF_PAD), const),
            pl.BlockSpec((D_MODEL, D_FF_PAD), const),
            pl.BlockSpec((3, D_FF_PAD), const),
            pl.BlockSpec((3, D_FF_PAD), const),
            pl.BlockSpec((1, D_FF_PAD), const),
            pl.BlockSpec((1, D_FF_PAD), const),
            pl.BlockSpec((D_FF_PAD, D_MODEL), const),
        ],
        out_specs=pl.BlockSpec((tm, D_MODEL), tok),
        out_shape=jax.ShapeDtypeStruct((n, D_MODEL), jnp.float32),
        scratch_shapes=[pltpu.VMEM((tm, D_MODEL), jnp.float32)],
        compiler_params=pltpu.CompilerParams(dimension_semantics=("arbitrary",),
                                             vmem_limit_bytes=56 * 1024 * 1024),
        name="ffn",
    )(h2, h2, h2, x1, wg, wu, cwg, cwu, bg, bu, wd)


def _rope_tables(seq_len):
    half = HEAD_DIM // 2
    inv = ROPE_THETA ** (-jnp.arange(0, half, 2, dtype=jnp.float32) / half)
    t = jnp.arange(seq_len, dtype=jnp.int32)
    rows = (t // GRID_W).astype(jnp.float32)[:, None] * inv[None, :]
    cols = (t % GRID_W).astype(jnp.float32)[:, None] * inv[None, :]
    cos = jnp.concatenate([jnp.cos(rows), jnp.cos(rows), jnp.cos(cols), jnp.cos(cols)], axis=1)
    sin = jnp.concatenate([-jnp.sin(rows), jnp.sin(rows), -jnp.sin(cols), jnp.sin(cols)], axis=1)
    return cos, sin


def _pad_cols(a, width):
    return jnp.pad(a, ((0, 0), (0, width - a.shape[1])))


def _prepare(norm1_w, w_in, dn_conv_w, dn_A_log_f, dn_A_log_b, dn_dt_bias_f, dn_dt_bias_b, dn_norm_w,
             attn_q_norm_w, attn_k_norm_w, attn_out_norm_w, w_out, norm2_w, w_ffn_in, ffn_conv_w,
             ffn_conv_b, w_ffn_out):
    bf = jnp.bfloat16
    gate_p = jnp.zeros((SUBLANES, LANES), jnp.float32)
    gate_p = gate_p.at[0, 0:4].set(dn_A_log_f).at[0, 4:8].set(dn_A_log_b)
    gate_p = gate_p.at[1, 0:4].set(dn_dt_bias_f).at[1, 4:8].set(dn_dt_bias_b)
    return dict(
        n1=norm1_w.reshape(1, D_MODEL),
        w_in=_pad_cols(w_in, IN_W_PAD).astype(bf),
        conv_w=dn_conv_w,
        gate_p=gate_p,
        dn_norm=dn_norm_w.reshape(1, DN_DIM),
        qn=attn_q_norm_w.reshape(1, HEAD_DIM),
        kn=attn_k_norm_w.reshape(1, HEAD_DIM),
        on=attn_out_norm_w.reshape(HEAD_DIM, 1),
        w_out=w_out.astype(bf),
        n2=norm2_w.reshape(1, D_MODEL),
        wg=_pad_cols(w_ffn_in[:, :D_FF], D_FF_PAD).astype(bf),
        wu=_pad_cols(w_ffn_in[:, D_FF:], D_FF_PAD).astype(bf),
        cwg=_pad_cols(ffn_conv_w[:, :D_FF], D_FF_PAD),
        cwu=_pad_cols(ffn_conv_w[:, D_FF:], D_FF_PAD),
        bg=_pad_cols(ffn_conv_b[None, :D_FF], D_FF_PAD),
        bu=_pad_cols(ffn_conv_b[None, D_FF:], D_FF_PAD),
        wd=jnp.pad(w_ffn_out, ((0, D_FF_PAD - D_FF), (0, 0))).astype(bf),
    )


def _tiles(seq_len):
    return dict(
        tm_in=min(512, seq_len),
        tq=min(2048, seq_len),
        tk=min(2048, seq_len),
        seg=min(512, seq_len),
        prep_chunks=2,
        tm_mix=min(512, seq_len),
        tm_ffn=min(512, seq_len),
        fc=256,
    )


def _layer(x, p):
    batch, seq_len, _ = x.shape
    t = _tiles(seq_len)
    x2 = x.reshape(batch * seq_len, D_MODEL)
    cos, sin = _rope_tables(seq_len)
    aq, ak, avt, dq, dk, dv, dz, gcol, grow = _in_proj(
        x2, seq_len, p["n1"], p["w_in"], cos, sin, p["qn"], p["kn"], p["conv_w"], p["gate_p"],
        tm=t["tm_in"])
    ao = _attention(aq, ak, avt, p["on"], batch, seq_len, tq=t["tq"], tk=t["tk"])
    o_f, o_b = _deltanet(dq, dk, dv, gcol, grow, batch, seq_len, seg=t["seg"],
                         prep_chunks=t["prep_chunks"])
    x1, h2 = _mix_out(x2, ao, o_f, o_b, dz, p["dn_norm"], p["w_out"], p["n2"], tm=t["tm_mix"])
    y = _ffn(h2, x1, seq_len, p["wg"], p["wu"], p["cwg"], p["cwu"], p["bg"], p["bu"], p["wd"],
             tm=t["tm_ffn"], fc=t["fc"])
    return y.reshape(batch, seq_len, D_MODEL)


def kernel(x_prompt, x_sample, norm1_w, w_in, dn_conv_w, dn_A_log_f, dn_A_log_b, dn_dt_bias_f, dn_dt_bias_b, dn_norm_w, attn_q_norm_w, attn_k_norm_w, attn_out_norm_w, w_out, norm2_w, w_ffn_in, ffn_conv_w, ffn_conv_b, w_ffn_out):
    depth = norm1_w.shape[0]

    def trunk(x):
        for l in range(depth):
            p = _prepare(norm1_w[l], w_in[l], dn_conv_w[l], dn_A_log_f[l], dn_A_log_b[l],
                         dn_dt_bias_f[l], dn_dt_bias_b[l], dn_norm_w[l], attn_q_norm_w[l],
                         attn_k_norm_w[l], attn_out_norm_w[l], w_out[l], norm2_w[l], w_ffn_in[l],
                         ffn_conv_w[l], ffn_conv_b[l], w_ffn_out[l])
            x = _layer(x, p)
        return x

    return (trunk(x_prompt), trunk(x_sample))
```

```python
import functools
import math

import jax
import jax.numpy as jnp
from jax import lax
from jax.experimental import pallas as pl
from jax.experimental.pallas import tpu as pltpu

D_MODEL = 1024
GRID_W = 64
HEAD_DIM = 128
N_ATTN_HEADS = 4
N_KV_HEADS = 2
N_DN_HEADS = 4
DN_DIM = 128
DN_CONV_K = 5
CHUNK = 64
ROPE_THETA = 10000.0
D_FF = 2752
EPS = 1e-6

ATTN_Q_W = N_ATTN_HEADS * HEAD_DIM
ATTN_KV_W = N_KV_HEADS * HEAD_DIM
DN_W = N_DN_HEADS * DN_DIM
IN_W = ATTN_Q_W + 2 * ATTN_KV_W + 4 * DN_W + 4 * N_DN_HEADS

LANES = 128
SUBLANES = 8
IN_W_PAD = 3200
D_FF_PAD = 2816
GATE_OFF = ATTN_Q_W + 2 * ATTN_KV_W + 4 * DN_W
HALO = 2 * SUBLANES

VMEM_LIMIT = 48 * 1024 * 1024

GT_G, GT_BETA, GT_EG, GT_EGL, GT_GL = 0, 8, 16, 24, 32

SUM_ROWS = 2 * SUBLANES
ATTN_UNROLL = 4
Q_SCALE = (HEAD_DIM ** -0.5) * math.log2(math.e)
SCORE_BOUND = 60.0


def _sigmoid(x):
    return 1.0 / (1.0 + jnp.exp(-x))


def _silu(x):
    return x * _sigmoid(x)


def _softplus(x):
    return jnp.maximum(x, 0.0) + jnp.log(1.0 + jnp.exp(-jnp.abs(x)))


def _rms_rows(x, w):
    return x * lax.rsqrt(jnp.mean(x * x, axis=-1, keepdims=True) + EPS) * w


def _aligned(x, m):
    return x if isinstance(x, int) else pl.multiple_of(x, m)


def _in_proj_kernel(xp_ref, x_ref, xn_ref, n1_ref, w_ref, cos_ref, sin_ref, qn_ref, kn_ref,
                    cw_ref, gp_ref,
                    aq_ref, ak_ref, avt_ref, dq_ref, dk_ref, dv_ref, dz_ref, gcol_ref, grow_ref,
                    ext_ref, he_ref, *, tm, tiles_per_seq):
    i = pl.program_id(0)
    first = (i % tiles_per_seq) == 0
    last = (i % tiles_per_seq) == tiles_per_seq - 1

    n1 = n1_ref[...]
    xp = jnp.where(first, jnp.zeros_like(xp_ref), xp_ref[...])
    xn = jnp.where(last, jnp.zeros_like(xn_ref), xn_ref[...])
    he_ref[0:HALO, :] = _rms_rows(xp, n1).astype(jnp.bfloat16)
    he_ref[HALO:HALO + tm, :] = _rms_rows(x_ref[...], n1).astype(jnp.bfloat16)
    he_ref[HALO + tm:tm + 2 * HALO, :] = _rms_rows(xn, n1).astype(jnp.bfloat16)
    h_rows = slice(HALO, HALO + tm)

    cos = cos_ref[...]
    sin = sin_ref[...]
    lane = lax.broadcasted_iota(jnp.int32, (tm, LANES), 1)
    low_half = (lane % 64) < 32

    def rope(xh):
        perm = jnp.where(low_half, pltpu.roll(xh, 96, axis=1), pltpu.roll(xh, 32, axis=1))
        return xh * cos + perm * sin

    pa = jnp.dot(he_ref[h_rows, :], w_ref[:, 0:ATTN_Q_W + 2 * ATTN_KV_W],
                 preferred_element_type=jnp.float32)
    dn_off = ATTN_Q_W + 2 * ATTN_KV_W
    for part in range(3):
        c0 = dn_off + part * DN_W
        ext_ref[:, part * DN_W:(part + 1) * DN_W] = jnp.dot(
            he_ref[...], w_ref[:, c0:c0 + DN_W], preferred_element_type=jnp.float32)
    z0 = dn_off + 3 * DN_W
    pz = jnp.dot(he_ref[h_rows, :], w_ref[:, z0:z0 + DN_W], preferred_element_type=jnp.float32)
    pg = jnp.dot(he_ref[h_rows, :], w_ref[:, GATE_OFF:GATE_OFF + LANES],
                 preferred_element_type=jnp.float32)

    qn = qn_ref[...]
    kn = kn_ref[...]
    for hh in range(N_ATTN_HEADS):
        xh = pa[:, hh * HEAD_DIM:(hh + 1) * HEAD_DIM]
        aq_ref[:, hh * HEAD_DIM:(hh + 1) * HEAD_DIM] = (rope(_rms_rows(xh, qn)) * Q_SCALE).astype(jnp.bfloat16)
    for hh in range(N_KV_HEADS):
        o = ATTN_Q_W + hh * HEAD_DIM
        xh = pa[:, o:o + HEAD_DIM]
        ak_ref[:, hh * HEAD_DIM:(hh + 1) * HEAD_DIM] = rope(_rms_rows(xh, kn)).astype(jnp.bfloat16)
    av = pa[:, ATTN_Q_W + ATTN_KV_W:ATTN_Q_W + 2 * ATTN_KV_W]
    avt_ref[...] = av.T.astype(jnp.bfloat16)

    pad = DN_CONV_K // 2
    out_refs = (dq_ref, dk_ref, dv_ref)
    for part in range(3):
        for hh in range(N_DN_HEADS):
            c0 = part * DN_W + hh * DN_DIM
            acc = None
            for j in range(DN_CONV_K):
                term = ext_ref[pl.ds(HALO - pad + j, tm), c0:c0 + DN_DIM] * cw_ref[j:j + 1, c0:c0 + DN_DIM]
                acc = term if acc is None else acc + term
            y = _silu(acc)
            if part < 2:
                y = y * lax.rsqrt(jnp.sum(y * y, axis=-1, keepdims=True) + EPS)
            if part == 0:
                y = y * (DN_DIM ** -0.5)
            out_refs[part][:, hh * DN_DIM:(hh + 1) * DN_DIM] = y.astype(jnp.bfloat16)

    dz_ref[...] = pz.astype(jnp.bfloat16)

    a_log = gp_ref[0:1, :]
    dt_bias = gp_ref[1:2, :]
    g = -jnp.exp(a_log) * _softplus(pg + dt_bias)
    beta = _sigmoid(pg)
    rin = lax.broadcasted_iota(jnp.int32, (tm, LANES), 0) % CHUNK
    pre = g
    suf = g
    s = 1
    while s < CHUNK:
        pre = pre + jnp.where(rin >= s, pltpu.roll(pre, s, axis=0), 0.0)
        suf = suf + jnp.where(rin < CHUNK - s, pltpu.roll(suf, tm - s, axis=0), 0.0)
        s *= 2
    tot = pre + suf - g
    gcum = jnp.where(lane < N_DN_HEADS, pre, suf)
    tab = jnp.where(lane < GT_BETA, gcum, 0.0)
    tab = jnp.where(jnp.logical_and(lane >= GT_BETA, lane < GT_EG), beta, tab)
    tab = jnp.where(jnp.logical_and(lane >= GT_EG, lane < GT_EGL),
                    pltpu.roll(jnp.exp(gcum), GT_EG, axis=1), tab)
    tab = jnp.where(jnp.logical_and(lane >= GT_EGL, lane < GT_GL),
                    pltpu.roll(jnp.exp(tot - gcum), GT_EGL, axis=1), tab)
    tab = jnp.where(jnp.logical_and(lane >= GT_GL, lane < GT_GL + 8),
                    pltpu.roll(jnp.exp(tot), GT_GL, axis=1), tab)
    gcol_ref[...] = tab
    gt = jnp.where(lane < GT_BETA, gcum, 0.0).T
    for c in range(tm // CHUNK):
        grow_ref[c] = gt[0:SUBLANES, c * CHUNK:(c + 1) * CHUNK]


def _in_proj(x2, seq_len, n1, w_in, cos, sin, qn, kn, conv_w, gate_p, *, tm):
    n = x2.shape[0]
    tiles_per_seq = seq_len // tm
    nblk8 = n // HALO
    tpb = tm // HALO
    const = lambda i: (0, 0)
    tok = lambda i: (i, 0)
    bf = jnp.bfloat16
    out_shape = (
        jax.ShapeDtypeStruct((n, ATTN_Q_W), bf),
        jax.ShapeDtypeStruct((n, ATTN_KV_W), bf),
        jax.ShapeDtypeStruct((ATTN_KV_W, n), bf),
        jax.ShapeDtypeStruct((n, DN_W), bf),
        jax.ShapeDtypeStruct((n, DN_W), bf),
        jax.ShapeDtypeStruct((n, DN_W), bf),
        jax.ShapeDtypeStruct((n, DN_W), bf),
        jax.ShapeDtypeStruct((n, LANES), jnp.float32),
        jax.ShapeDtypeStruct((n // CHUNK, SUBLANES, CHUNK), jnp.float32),
    )
    out_specs = (
        pl.BlockSpec((tm, ATTN_Q_W), tok),
        pl.BlockSpec((tm, ATTN_KV_W), tok),
        pl.BlockSpec((ATTN_KV_W, tm), lambda i: (0, i)),
        pl.BlockSpec((tm, DN_W), tok),
        pl.BlockSpec((tm, DN_W), tok),
        pl.BlockSpec((tm, DN_W), tok),
        pl.BlockSpec((tm, DN_W), tok),
        pl.BlockSpec((tm, LANES), tok),
        pl.BlockSpec((tm // CHUNK, SUBLANES, CHUNK), lambda i: (i, 0, 0)),
    )
    in_specs = [
        pl.BlockSpec((HALO, D_MODEL), lambda i: (jnp.maximum(i * tpb - 1, 0), 0)),
        pl.BlockSpec((tm, D_MODEL), tok),
        pl.BlockSpec((HALO, D_MODEL), lambda i: (jnp.minimum((i + 1) * tpb, nblk8 - 1), 0)),
        pl.BlockSpec((1, D_MODEL), const),
        pl.BlockSpec((D_MODEL, IN_W_PAD), const),
        pl.BlockSpec((tm, LANES), lambda i: (i % tiles_per_seq, 0)),
        pl.BlockSpec((tm, LANES), lambda i: (i % tiles_per_seq, 0)),
        pl.BlockSpec((1, HEAD_DIM), const),
        pl.BlockSpec((1, HEAD_DIM), const),
        pl.BlockSpec((DN_CONV_K, 3 * DN_W), const),
        pl.BlockSpec((SUBLANES, LANES), const),
    ]
    return pl.pallas_call(
        functools.partial(_in_proj_kernel, tm=tm, tiles_per_seq=tiles_per_seq),
        grid=(n // tm,),
        in_specs=in_specs,
        out_specs=out_specs,
        out_shape=out_shape,
        scratch_shapes=[pltpu.VMEM((tm + 2 * HALO, 3 * DN_W), jnp.float32),
                        pltpu.VMEM((tm + 2 * HALO, D_MODEL), jnp.bfloat16)],
        compiler_params=pltpu.CompilerParams(dimension_semantics=("arbitrary",),
                                             vmem_limit_bytes=VMEM_LIMIT),
        name="in_proj",
    )(x2, x2, x2, n1, w_in, cos, sin, qn, kn, conv_w, gate_p)


def _attn_kernel(q_ref, k_ref, vt_ref, on_ref, nw_ref, o_ref, s_ref, p_ref, al_ref, cm_ref, m_ref, acc_ref,
                 *, tq, tk, seq_len):
    g = N_ATTN_HEADS // N_KV_HEADS
    nqt = tq // LANES
    nkv = seq_len // tk
    items = nqt * nkv
    kv_shift = nkv.bit_length() - 1
    ones = jnp.ones((SUM_ROWS, tk), jnp.bfloat16)

    def split(t):
        if isinstance(t, int):
            return t // nkv, t % nkv
        return lax.shift_right_logical(t, kv_shift), jnp.bitwise_and(t, nkv - 1)

    def scores(t):
        qi, kj = split(t)
        r0 = _aligned(qi * LANES, LANES)
        k0 = _aligned(kj * tk, tk)
        qblk = q_ref[pl.ds(r0, LANES), :]
        qs = jnp.concatenate([qblk[:, j * HEAD_DIM:(j + 1) * HEAD_DIM] for j in range(g)], axis=0)
        kb = k_ref[pl.ds(k0, tk), :]
        return lax.dot_general(kb, qs, (((1,), (1,)), ((), ())),
                               preferred_element_type=jnp.float32)

    def stage_qk(t, slot):
        s = scores(t)
        s_ref[slot] = s
        cm_ref[slot] = jnp.max(s, axis=0, keepdims=True)

    def stage_qk_exp(t, slot):
        p_ref[slot] = jnp.exp2(scores(t)).astype(jnp.bfloat16)

    def stage_softmax(t, slot):
        _, kj = split(t)
        m_prev = jnp.where(kj == 0, -jnp.inf, m_ref[...])
        m_new = jnp.maximum(m_prev, cm_ref[slot])
        al_ref[slot] = jnp.exp2(m_prev - m_new)
        p_ref[slot] = jnp.exp2(s_ref[slot] - m_new).astype(jnp.bfloat16)
        m_ref[...] = m_new

    def stage_pv(t, slot, may_finish, bounded):
        qi, kj = split(t)
        k0 = _aligned(kj * tk, tk)
        lhs = jnp.concatenate([vt_ref[:, pl.ds(k0, tk)], ones], axis=0)
        if bounded:
            prev = jnp.where(kj == 0, 0.0, acc_ref[...])
        else:
            prev = acc_ref[...] * al_ref[slot]
        acc = prev + jnp.dot(lhs, p_ref[slot], preferred_element_type=jnp.float32)
        acc_ref[...] = acc

        def finalize():
            o = acc[0:HEAD_DIM] / acc[HEAD_DIM:HEAD_DIM + 1]
            o = o * lax.rsqrt(jnp.mean(o * o, axis=0, keepdims=True) + EPS) * on_ref[...]
            ot = o.T
            r0 = _aligned(qi * LANES, LANES)
            for j in range(g):
                o_ref[pl.ds(r0, LANES), j * HEAD_DIM:(j + 1) * HEAD_DIM] = (
                    ot[j * LANES:(j + 1) * LANES, :].astype(jnp.bfloat16))

        if not may_finish:
            return
        if isinstance(kj, int):
            if kj == nkv - 1:
                finalize()
        else:
            pl.when(kj == nkv - 1)(finalize)

    def run_body(u, bounded):
        static = isinstance(u, int)
        for j in range(ATTN_UNROLL):
            may_finish = nkv == 1 or j == ATTN_UNROLL - 1
            if bounded:
                if not static or u + 1 + j < items:
                    stage_qk_exp(u + 1 + j, (1 + j) % ATTN_UNROLL)
            else:
                if not static or u + 2 + j < items:
                    stage_qk(u + 2 + j, (2 + j) % ATTN_UNROLL)
                if not static or u + 1 + j < items:
                    stage_softmax(u + 1 + j, (1 + j) % ATTN_UNROLL)
            stage_pv(u + j, j, may_finish, bounded)

    def pipeline(bounded):
        acc_ref[...] = jnp.zeros_like(acc_ref)
        if bounded:
            stage_qk_exp(0, 0)
        else:
            m_ref[...] = jnp.full(m_ref.shape, -jnp.inf, jnp.float32)
            stage_qk(0, 0)
            stage_qk(1, 1)
            stage_softmax(0, 0)

        def body(i, carry):
            run_body(pl.multiple_of(i * ATTN_UNROLL, ATTN_UNROLL), bounded)
            return carry

        lax.fori_loop(0, items // ATTN_UNROLL - 1, body, 0)
        run_body(items - ATTN_UNROLL, bounded)

    w2 = nw_ref[...] * nw_ref[...]
    bound_sq = (1.02 * (HEAD_DIM * Q_SCALE) ** 2) * jnp.max(w2[0:1]) * jnp.max(w2[1:2])
    bounded = bound_sq <= SCORE_BOUND * SCORE_BOUND
    pl.when(bounded)(lambda: pipeline(True))
    pl.when(jnp.logical_not(bounded))(lambda: pipeline(False))


def _attention(aq, ak, avt, out_norm_col, qk_norm_w, batch, seq_len, *, tq, tk):
    n = aq.shape[0]
    g = N_ATTN_HEADS // N_KV_HEADS
    nq = g * LANES
    qpb = seq_len // tq
    nkv = seq_len // tk
    items = (tq // LANES) * nkv
    assert nkv == 1 or nkv % ATTN_UNROLL == 0
    assert nkv & (nkv - 1) == 0 and items % ATTN_UNROLL == 0 and items >= ATTN_UNROLL
    return pl.pallas_call(
        functools.partial(_attn_kernel, tq=tq, tk=tk, seq_len=seq_len),
        grid=(batch, N_KV_HEADS, qpb),
        in_specs=[
            pl.BlockSpec((tq, g * HEAD_DIM), lambda b, h, i: (b * qpb + i, h)),
            pl.BlockSpec((seq_len, HEAD_DIM), lambda b, h, i: (b, h)),
            pl.BlockSpec((HEAD_DIM, seq_len), lambda b, h, i: (h, b)),
            pl.BlockSpec((HEAD_DIM, 1), lambda b, h, i: (0, 0)),
            pl.BlockSpec((2, HEAD_DIM), lambda b, h, i: (0, 0)),
        ],
        out_specs=pl.BlockSpec((tq, g * HEAD_DIM), lambda b, h, i: (b * qpb + i, h)),
        out_shape=jax.ShapeDtypeStruct((n, ATTN_Q_W), jnp.bfloat16),
        scratch_shapes=[
            pltpu.VMEM((ATTN_UNROLL, tk, nq), jnp.float32),
            pltpu.VMEM((ATTN_UNROLL, tk, nq), jnp.bfloat16),
            pltpu.VMEM((ATTN_UNROLL, 1, nq), jnp.float32),
            pltpu.VMEM((ATTN_UNROLL, 1, nq), jnp.float32),
            pltpu.VMEM((1, nq), jnp.float32),
            pltpu.VMEM((HEAD_DIM + SUM_ROWS, nq), jnp.float32),
        ],
        compiler_params=pltpu.CompilerParams(
            dimension_semantics=("arbitrary", "arbitrary", "arbitrary"),
            vmem_limit_bytes=56 * 1024 * 1024),
        name="attention",
    )(aq, ak, avt, out_norm_col, qk_norm_w)


def _dn_kernel(qf_ref, kf_ref, vf_ref, tf_ref, rf_ref, qb_ref, kb_ref, vb_ref, tb_ref, rb_ref,
               of_ref, ob_ref,
               s_ref, wq_ref, u_ref, at_ref, kdt_ref, *, nb, nchunk, prep_chunks):
    seg = pl.program_id(1)

    @pl.when(seg == 0)
    def _():
        s_ref[...] = jnp.zeros_like(s_ref)

    ri = lax.broadcasted_iota(jnp.int32, (CHUNK, CHUNK), 0)
    ci = lax.broadcasted_iota(jnp.int32, (CHUNK, CHUNK), 1)
    eye = jnp.where(ri == ci, 1.0, 0.0)
    incl = (ri >= ci, ri <= ci)
    strict = (ri > ci, ri < ci)
    ins = ((qf_ref, kf_ref, vf_ref, tf_ref, rf_ref), (qb_ref, kb_ref, vb_ref, tb_ref, rb_ref))
    outs = (of_ref, ob_ref)
    bf = jnp.bfloat16
    probs = [(bb, d, hh) for bb in range(nb) for d in range(2) for hh in range(N_DN_HEADS)]

    def mm(a, b):
        return jnp.dot(a.astype(bf), b.astype(bf), preferred_element_type=jnp.float32)

    def prep(ci_, carry):
        units = []
        for pc in range(prep_chunks):
            c = ci_ * prep_chunks + pc
            r0 = pl.multiple_of(c * CHUNK, CHUNK)
            for bb, d, hh in probs:
                q_ref, k_ref, v_ref, t_ref, g_ref = ins[d]
                col = d * N_DN_HEADS + hh
                lo = hh * DN_DIM
                tab = t_ref[bb, pl.ds(r0, CHUNK), :]
                k = k_ref[bb, pl.ds(r0, CHUNK), lo:lo + DN_DIM]
                q = q_ref[bb, pl.ds(r0, CHUNK), lo:lo + DN_DIM]
                units.append(dict(
                    d=d, unit=(bb * 2 * N_DN_HEADS + col) * nchunk + c, q=q, k=k,
                    v=v_ref[bb, pl.ds(r0, CHUNK), lo:lo + DN_DIM].astype(jnp.float32),
                    gc=tab[:, GT_G + col:GT_G + col + 1],
                    beta=tab[:, GT_BETA + col:GT_BETA + col + 1],
                    eg=tab[:, GT_EG + col:GT_EG + col + 1],
                    egl=tab[:, GT_EGL + col:GT_EGL + col + 1],
                    gr=g_ref[bb, c][col:col + 1, :]))
        for u in units:
            kq = jnp.concatenate([u["k"], u["q"]], axis=0)
            u["kkqk"] = lax.dot_general(kq, u["k"], (((1,), (1,)), ((), ())),
                                        preferred_element_type=jnp.float32)
        for u in units:
            d = u["d"]
            kf32 = u["k"].astype(jnp.float32)
            dec = jnp.exp(jnp.minimum(u["gc"] - u["gr"], 0.0))
            x = jnp.where(strict[d], -(u["beta"] * u["kkqk"][0:CHUNK]) * dec, 0.0)
            attn = jnp.where(incl[d], u["kkqk"][CHUNK:2 * CHUNK] * dec, 0.0)
            at_ref[u["unit"]] = attn.astype(bf)
            wq_ref[u["unit"], CHUNK:2 * CHUNK, :] = (u["q"].astype(jnp.float32) * u["eg"]).astype(bf)
            kdt_ref[u["unit"]] = (kf32 * u["egl"]).T.astype(bf)
            u["rhs"] = jnp.concatenate([u["v"] * u["beta"], kf32 * (u["beta"] * u["eg"])],
                                       axis=1).astype(bf)
            u["x"] = x
            u["ssum"] = eye + x
        for u in units:
            u["p"] = mm(u["x"], u["x"])
        for _ in range(4):
            for u in units:
                pp = mm(jnp.concatenate([u["p"], u["ssum"]], axis=0), u["p"])
                u["ssum"] = u["ssum"] + pp[CHUNK:2 * CHUNK]
                u["p"] = pp[0:CHUNK]
        for u in units:
            u["ssum"] = u["ssum"] + mm(u["ssum"], u["p"])
        for u in units:
            uw = jnp.dot(u["ssum"].astype(bf), u["rhs"], preferred_element_type=jnp.float32)
            u_ref[u["unit"]] = uw[:, 0:DN_DIM]
            wq_ref[u["unit"], 0:CHUNK, :] = uw[:, DN_DIM:2 * DN_DIM].astype(bf)
        return carry

    lax.fori_loop(0, nchunk // prep_chunks, prep, 0)

    def scan(step, carry):
        chains = []
        for bb, d, hh in probs:
            c = step if d == 0 else nchunk - 1 - step
            r0 = pl.multiple_of(c * CHUNK, CHUNK)
            col = d * N_DN_HEADS + hh
            sidx = bb * 2 * N_DN_HEADS + col
            tabrow = ins[d][3][bb, pl.ds(r0, 1), :]
            chains.append(dict(bb=bb, d=d, hh=hh, sidx=sidx, unit=sidx * nchunk + c, r0=r0,
                               gl=tabrow[:, GT_GL + col:GT_GL + col + 1]))
        for ch in chains:
            ch["st"] = s_ref[ch["sidx"]]
            ch["res"] = jnp.dot(wq_ref[ch["unit"]], ch["st"].astype(bf),
                                preferred_element_type=jnp.float32)
        for ch in chains:
            ch["vn"] = (u_ref[ch["unit"]] - ch["res"][0:CHUNK]).astype(bf)
        for ch in chains:
            ch["o"] = jnp.dot(at_ref[ch["unit"]], ch["vn"], preferred_element_type=jnp.float32)
            ch["upd"] = jnp.dot(kdt_ref[ch["unit"]], ch["vn"], preferred_element_type=jnp.float32)
        for ch in chains:
            hh = ch["hh"]
            o = ch["res"][CHUNK:2 * CHUNK] + ch["o"]
            outs[ch["d"]][ch["bb"], pl.ds(ch["r0"], CHUNK), hh * DN_DIM:(hh + 1) * DN_DIM] = o.astype(bf)
            s_ref[ch["sidx"]] = ch["st"] * ch["gl"] + ch["upd"]
        return carry

    lax.fori_loop(0, nchunk, scan, 0)


def _deltanet(dq, dk, dv, gcol, grow, batch, seq_len, *, seg, prep_chunks, nb):
    nseg = seq_len // seg
    nchunk = seg // CHUNK
    units = nb * 2 * N_DN_HEADS * nchunk
    fwd = lambda b, s: (b, s, 0)
    bwd = lambda b, s: (b, nseg - 1 - s, 0)
    fwd4 = lambda b, s: (b, s, 0, 0)
    bwd4 = lambda b, s: (b, nseg - 1 - s, 0, 0)
    big = lambda m: pl.BlockSpec((nb, seg, DN_W), m)
    tabs = lambda m: pl.BlockSpec((nb, seg, LANES), m)
    rows = lambda m: pl.BlockSpec((nb, nchunk, SUBLANES, CHUNK), m)
    bf = jnp.bfloat16
    out = jax.ShapeDtypeStruct((batch, seq_len, DN_W), bf)
    return pl.pallas_call(
        functools.partial(_dn_kernel, nb=nb, nchunk=nchunk, prep_chunks=prep_chunks),
        grid=(batch // nb, nseg),
        in_specs=[big(fwd), big(fwd), big(fwd), tabs(fwd), rows(fwd4),
                  big(bwd), big(bwd), big(bwd), tabs(bwd), rows(bwd4)],
        out_specs=(big(fwd), big(bwd)),
        out_shape=(out, out),
        scratch_shapes=[
            pltpu.VMEM((nb * 2 * N_DN_HEADS, DN_DIM, DN_DIM), jnp.float32),
            pltpu.VMEM((units, 2 * CHUNK, DN_DIM), bf),
            pltpu.VMEM((units, CHUNK, DN_DIM), jnp.float32),
            pltpu.VMEM((units, CHUNK, CHUNK), bf),
            pltpu.VMEM((units, DN_DIM, CHUNK), bf),
        ],
        compiler_params=pltpu.CompilerParams(dimension_semantics=("arbitrary", "arbitrary"),
                                             vmem_limit_bytes=VMEM_LIMIT),
        name="deltanet",
    )(dq, dk, dv, gcol, grow, dq, dk, dv, gcol, grow)


def _mix_kernel(x_ref, ao_ref, of_ref, ob_ref, z_ref, dnw_ref, wo_ref, n2_ref, x1_ref, h2_ref):
    dnw = dnw_ref[...]
    parts = []
    for hh in range(N_DN_HEADS):
        sl = slice(hh * DN_DIM, (hh + 1) * DN_DIM)
        o = of_ref[:, sl].astype(jnp.float32) + ob_ref[:, sl].astype(jnp.float32)
        z = z_ref[:, sl].astype(jnp.float32)
        parts.append((_rms_rows(o, dnw) * _silu(z)).astype(jnp.bfloat16))
    dn = jnp.concatenate(parts, axis=1)
    y = jnp.dot(ao_ref[...], wo_ref[0:ATTN_Q_W, :], preferred_element_type=jnp.float32)
    y = y + jnp.dot(dn, wo_ref[ATTN_Q_W:ATTN_Q_W + DN_W, :], preferred_element_type=jnp.float32)
    x1 = x_ref[...] + y
    x1_ref[...] = x1
    h2_ref[...] = _rms_rows(x1, n2_ref[...]).astype(jnp.bfloat16)


def _mix_out(x2, ao, o_f, o_b, dz, dn_norm, w_out, n2, *, tm):
    n = x2.shape[0]
    tok = lambda i: (i, 0)
    const = lambda i: (0, 0)
    half = pl.BlockSpec((tm, DN_W), tok)
    return pl.pallas_call(
        _mix_kernel,
        grid=(n // tm,),
        in_specs=[pl.BlockSpec((tm, D_MODEL), tok), half, half, half, half,
                  pl.BlockSpec((1, DN_DIM), const),
                  pl.BlockSpec((ATTN_Q_W + DN_W, D_MODEL), const),
                  pl.BlockSpec((1, D_MODEL), const)],
        out_specs=(pl.BlockSpec((tm, D_MODEL), tok), pl.BlockSpec((tm, D_MODEL), tok)),
        out_shape=(jax.ShapeDtypeStruct((n, D_MODEL), jnp.float32),
                   jax.ShapeDtypeStruct((n, D_MODEL), jnp.bfloat16)),
        compiler_params=pltpu.CompilerParams(dimension_semantics=("arbitrary",),
                                             vmem_limit_bytes=VMEM_LIMIT),
        name="mix_out",
    )(x2, ao, o_f, o_b, dz, dn_norm, w_out, n2)


def _ffn_kernel(hp_ref, h_ref, hn_ref, x1_ref, wg_ref, wu_ref, cwg_ref, cwu_ref, bg_ref, bu_ref,
                wd_ref, y_ref, acc_ref, he_ref, *, tm, fc, tiles_per_seq):
    i = pl.program_id(0)
    first = (i % tiles_per_seq) == 0
    last = (i % tiles_per_seq) == tiles_per_seq - 1
    rows = tm + 2 * HALO
    he_ref[0:HALO, :] = jnp.where(first, jnp.zeros_like(hp_ref), hp_ref[...])
    he_ref[HALO:HALO + tm, :] = h_ref[...]
    he_ref[HALO + tm:rows, :] = jnp.where(last, jnp.zeros_like(hn_ref), hn_ref[...])

    def conv(u, cw_ref, b_ref, sl):
        prev = pltpu.roll(u, 1, axis=0)[HALO:HALO + tm]
        nxt = pltpu.roll(u, rows - 1, axis=0)[HALO:HALO + tm]
        mid = u[HALO:HALO + tm]
        return prev * cw_ref[0:1, sl] + mid * cw_ref[1:2, sl] + nxt * cw_ref[2:3, sl] + b_ref[0:1, sl]

    def up(c):
        sl = slice(c * fc, (c + 1) * fc)
        return (jnp.dot(he_ref[...], wg_ref[:, sl], preferred_element_type=jnp.float32),
                jnp.dot(he_ref[...], wu_ref[:, sl], preferred_element_type=jnp.float32))

    nchunks = D_FF_PAD // fc
    cur = up(0)
    for c in range(nchunks):
        nxt = up(c + 1) if c + 1 < nchunks else None
        sl = slice(c * fc, (c + 1) * fc)
        ug, uu = cur
        act = (_silu(conv(ug, cwg_ref, bg_ref, sl)) * conv(uu, cwu_ref, bu_ref, sl)).astype(jnp.bfloat16)
        part = jnp.dot(act, wd_ref[sl, :], preferred_element_type=jnp.float32)
        if c == 0:
            acc_ref[...] = part
        else:
            acc_ref[...] += part
        cur = nxt
    y_ref[...] = x1_ref[...] + acc_ref[...]


def _ffn(h2, x1, seq_len, wg, wu, cwg, cwu, bg, bu, wd, *, tm, fc):
    n = h2.shape[0]
    tiles_per_seq = seq_len // tm
    nblk = n // HALO
    tpb = tm // HALO
    tok = lambda i: (i, 0)
    const = lambda i: (0, 0)
    return pl.pallas_call(
        functools.partial(_ffn_kernel, tm=tm, fc=fc, tiles_per_seq=tiles_per_seq),
        grid=(n // tm,),
        in_specs=[
            pl.BlockSpec((HALO, D_MODEL), lambda i: (jnp.maximum(i * tpb - 1, 0), 0)),
            pl.BlockSpec((tm, D_MODEL), tok),
            pl.BlockSpec((HALO, D_MODEL), lambda i: (jnp.minimum((i + 1) * tpb, nblk - 1), 0)),
            pl.BlockSpec((tm, D_MODEL), tok),
            pl.BlockSpec((D_MODEL, D_FF_PAD), const),
            pl.BlockSpec((D_MODEL, D_FF_PAD), const),
            pl.BlockSpec((3, D_FF_PAD), const),
            pl.BlockSpec((3, D_FF_PAD), const),
            pl.BlockSpec((1, D_FF_PAD), const),
            pl.BlockSpec((1, D_FF_PAD), const),
            pl.BlockSpec((D_FF_PAD, D_MODEL), const),
        ],
        out_specs=pl.BlockSpec((tm, D_MODEL), tok),
        out_shape=jax.ShapeDtypeStruct((n, D_MODEL), jnp.float32),
        scratch_shapes=[pltpu.VMEM((tm, D_MODEL), jnp.float32),
                        pltpu.VMEM((tm + 2 * HALO, D_MODEL), jnp.bfloat16)],
        compiler_params=pltpu.CompilerParams(dimension_semantics=("arbitrary",),
                                             vmem_limit_bytes=56 * 1024 * 1024),
        name="ffn",
    )(h2, h2, h2, x1, wg, wu, cwg, cwu, bg, bu, wd)


def _rope_tables(seq_len):
    half = HEAD_DIM // 2
    inv = ROPE_THETA ** (-jnp.arange(0, half, 2, dtype=jnp.float32) / half)
    t = jnp.arange(seq_len, dtype=jnp.int32)
    rows = (t // GRID_W).astype(jnp.float32)[:, None] * inv[None, :]
    cols = (t % GRID_W).astype(jnp.float32)[:, None] * inv[None, :]
    cos = jnp.concatenate([jnp.cos(rows), jnp.cos(rows), jnp.cos(cols), jnp.cos(cols)], axis=1)
    sin = jnp.concatenate([-jnp.sin(rows), jnp.sin(rows), -jnp.sin(cols), jnp.sin(cols)], axis=1)
    return cos, sin


def _pad_cols(a, width):
    return jnp.pad(a, ((0, 0), (0, width - a.shape[1])))


def _prepare(norm1_w, w_in, dn_conv_w, dn_A_log_f, dn_A_log_b, dn_dt_bias_f, dn_dt_bias_b, dn_norm_w,
             attn_q_norm_w, attn_k_norm_w, attn_out_norm_w, w_out, norm2_w, w_ffn_in, ffn_conv_w,
             ffn_conv_b, w_ffn_out):
    bf = jnp.bfloat16
    gate_p = jnp.zeros((SUBLANES, LANES), jnp.float32)
    gate_p = gate_p.at[0, 0:4].set(dn_A_log_f).at[0, 4:8].set(dn_A_log_b)
    gate_p = gate_p.at[1, 0:4].set(dn_dt_bias_f).at[1, 4:8].set(dn_dt_bias_b)
    return dict(
        n1=norm1_w.reshape(1, D_MODEL),
        w_in=_pad_cols(w_in, IN_W_PAD).astype(bf),
        conv_w=dn_conv_w,
        gate_p=gate_p,
        dn_norm=dn_norm_w.reshape(1, DN_DIM),
        qn=attn_q_norm_w.reshape(1, HEAD_DIM),
        kn=attn_k_norm_w.reshape(1, HEAD_DIM),
        on=attn_out_norm_w.reshape(HEAD_DIM, 1),
        w_out=w_out.astype(bf),
        n2=norm2_w.reshape(1, D_MODEL),
        wg=_pad_cols(w_ffn_in[:, :D_FF], D_FF_PAD).astype(bf),
        wu=_pad_cols(w_ffn_in[:, D_FF:], D_FF_PAD).astype(bf),
        cwg=_pad_cols(ffn_conv_w[:, :D_FF], D_FF_PAD),
        cwu=_pad_cols(ffn_conv_w[:, D_FF:], D_FF_PAD),
        bg=_pad_cols(ffn_conv_b[None, :D_FF], D_FF_PAD),
        bu=_pad_cols(ffn_conv_b[None, D_FF:], D_FF_PAD),
        wd=jnp.pad(w_ffn_out, ((0, D_FF_PAD - D_FF), (0, 0))).astype(bf),
    )


def _tiles(seq_len):
    return dict(
        tm_in=min(512, seq_len),
        tq=min(2048, seq_len),
        tk=min(2048, seq_len),
        seg=min(256, seq_len),
        prep_chunks=2,
        dn_batch=2,
        tm_mix=min(512, seq_len),
        tm_ffn=min(512, seq_len),
        fc=256,
    )


def _layer(x, p):
    batch, seq_len, _ = x.shape
    t = _tiles(seq_len)
    x2 = x.reshape(batch * seq_len, D_MODEL)
    cos, sin = _rope_tables(seq_len)
    aq, ak, avt, dq, dk, dv, dz, gcol, grow = _in_proj(
        x2, seq_len, p["n1"], p["w_in"], cos, sin, p["qn"], p["kn"], p["conv_w"], p["gate_p"],
        tm=t["tm_in"])
    ao = _attention(aq, ak, avt, p["on"], jnp.concatenate([p["qn"], p["kn"]], axis=0),
                    batch, seq_len, tq=t["tq"], tk=t["tk"])
    n = batch * seq_len
    o_f, o_b = _deltanet(dq.reshape(batch, seq_len, DN_W), dk.reshape(batch, seq_len, DN_W),
                         dv.reshape(batch, seq_len, DN_W), gcol.reshape(batch, seq_len, LANES),
                         grow.reshape(batch, seq_len // CHUNK, SUBLANES, CHUNK),
                         batch, seq_len, seg=t["seg"], prep_chunks=t["prep_chunks"], nb=t["dn_batch"])
    o_f = o_f.reshape(n, DN_W)
    o_b = o_b.reshape(n, DN_W)
    x1, h2 = _mix_out(x2, ao, o_f, o_b, dz, p["dn_norm"], p["w_out"], p["n2"], tm=t["tm_mix"])
    y = _ffn(h2, x1, seq_len, p["wg"], p["wu"], p["cwg"], p["cwu"], p["bg"], p["bu"], p["wd"],
             tm=t["tm_ffn"], fc=t["fc"])
    return y.reshape(batch, seq_len, D_MODEL)


def kernel(x_prompt, x_sample, norm1_w, w_in, dn_conv_w, dn_A_log_f, dn_A_log_b, dn_dt_bias_f, dn_dt_bias_b, dn_norm_w, attn_q_norm_w, attn_k_norm_w, attn_out_norm_w, w_out, norm2_w, w_ffn_in, ffn_conv_w, ffn_conv_b, w_ffn_out):
    depth = norm1_w.shape[0]

    def trunk(x):
        for l in range(depth):
            p = _prepare(norm1_w[l], w_in[l], dn_conv_w[l], dn_A_log_f[l], dn_A_log_b[l],
                         dn_dt_bias_f[l], dn_dt_bias_b[l], dn_norm_w[l], attn_q_norm_w[l],
                         attn_k_norm_w[l], attn_out_norm_w[l], w_out[l], norm2_w[l], w_ffn_in[l],
                         ffn_conv_w[l], ffn_conv_b[l], w_ffn_out[l])
            x = _layer(x, p)
        return x

    return (trunk(x_prompt), trunk(x_sample))
```

```python
import functools
import math

import jax
import jax.numpy as jnp
import numpy as np
from jax import lax
from jax.experimental import pallas as pl
from jax.experimental.pallas import tpu as pltpu

D_MODEL = 1024
GRID_W = 64
HEAD_DIM = 128
N_ATTN_HEADS = 4
N_KV_HEADS = 2
N_DN_HEADS = 4
DN_DIM = 128
DN_CONV_K = 5
CHUNK = 64
ROPE_THETA = 10000.0
D_FF = 2752
EPS = 1e-6

ATTN_Q_W = N_ATTN_HEADS * HEAD_DIM
ATTN_KV_W = N_KV_HEADS * HEAD_DIM
DN_W = N_DN_HEADS * DN_DIM
IN_W = ATTN_Q_W + 2 * ATTN_KV_W + 4 * DN_W + 4 * N_DN_HEADS

LANES = 128
SUBLANES = 8
IN_W_PAD = 3200
D_FF_PAD = 2816
GATE_OFF = ATTN_Q_W + 2 * ATTN_KV_W + 4 * DN_W
HALO = 2 * SUBLANES

VMEM_LIMIT = 48 * 1024 * 1024

GT_G, GT_BETA, GT_EG, GT_EGL, GT_GL, GT_BEG = 0, 8, 16, 24, 32, 40
GT_ROWS = 48

SUM_ROWS = 2 * SUBLANES
ATTN_UNROLL = 4
ATTN_UNROLL_BOUNDED = 8
IN_ROW_BLOCKS = 2
FFN_ROW_BLOCKS = 4
Q_SCALE = (HEAD_DIM ** -0.5) * math.log2(math.e)
SCORE_BOUND = 60.0


def _sigmoid(x):
    return 1.0 / (1.0 + jnp.exp(-x))


def _silu(x):
    return x * _sigmoid(x)


def _softplus(x):
    return jnp.maximum(x, 0.0) + jnp.log(1.0 + jnp.exp(-jnp.abs(x)))


def _rms_rows(x, w):
    return x * lax.rsqrt(jnp.mean(x * x, axis=-1, keepdims=True) + EPS) * w


def _row_blocks(nrows, nblocks):
    tile = 2 * SUBLANES
    cuts = [-(-(nrows * b // nblocks) // tile) * tile for b in range(nblocks)] + [nrows]
    return list(zip(cuts[:-1], cuts[1:]))


def _aligned(x, m):
    return x if isinstance(x, int) else pl.multiple_of(x, m)


def _in_proj_kernel(xp_ref, x_ref, xn_ref, n1_ref, w_ref, cos_ref, sin_ref, qn_ref, kn_ref,
                    cw_ref, gp_ref,
                    aq_ref, ak_ref, avt_ref, dq_ref, dk_ref, dv_ref, dz_ref, gcol_ref, grow_ref,
                    ext_ref, he_ref, *, tm, tiles_per_seq):
    i = pl.program_id(0)
    first = (i % tiles_per_seq) == 0
    last = (i % tiles_per_seq) == tiles_per_seq - 1

    n1 = n1_ref[...]
    xp = jnp.where(first, jnp.zeros_like(xp_ref), xp_ref[...])
    xn = jnp.where(last, jnp.zeros_like(xn_ref), xn_ref[...])
    he_ref[0:HALO, :] = _rms_rows(xp, n1).astype(jnp.bfloat16)
    he_ref[HALO:HALO + tm, :] = _rms_rows(x_ref[...], n1).astype(jnp.bfloat16)
    he_ref[HALO + tm:tm + 2 * HALO, :] = _rms_rows(xn, n1).astype(jnp.bfloat16)
    h_rows = slice(HALO, HALO + tm)

    cos = cos_ref[...]
    sin = sin_ref[...]
    lane = lax.broadcasted_iota(jnp.int32, (tm, LANES), 1)
    low_half = (lane % 64) < 32

    def rope(xh):
        perm = jnp.where(low_half, pltpu.roll(xh, 96, axis=1), pltpu.roll(xh, 32, axis=1))
        return xh * cos + perm * sin

    def proj(lo, width, r0, nrows):
        return jnp.concatenate(
            [jnp.dot(he_ref[r0 + a:r0 + b, :], w_ref[:, lo:lo + width],
                     preferred_element_type=jnp.float32)
             for a, b in _row_blocks(nrows, IN_ROW_BLOCKS)], axis=0)

    pa = proj(0, ATTN_Q_W + 2 * ATTN_KV_W, HALO, tm)
    dn_off = ATTN_Q_W + 2 * ATTN_KV_W
    for part in range(3):
        ext_ref[:, part * DN_W:(part + 1) * DN_W] = proj(dn_off + part * DN_W, DN_W, 0, tm + 2 * HALO)
    pz = proj(dn_off + 3 * DN_W, DN_W, HALO, tm)
    pg = proj(GATE_OFF, LANES, HALO, tm)

    qn = qn_ref[...]
    kn = kn_ref[...]
    for hh in range(N_ATTN_HEADS):
        xh = pa[:, hh * HEAD_DIM:(hh + 1) * HEAD_DIM]
        aq_ref[:, hh * HEAD_DIM:(hh + 1) * HEAD_DIM] = (rope(_rms_rows(xh, qn)) * Q_SCALE).astype(jnp.bfloat16)
    for hh in range(N_KV_HEADS):
        o = ATTN_Q_W + hh * HEAD_DIM
        xh = pa[:, o:o + HEAD_DIM]
        ak_ref[:, hh * HEAD_DIM:(hh + 1) * HEAD_DIM] = rope(_rms_rows(xh, kn)).astype(jnp.bfloat16)
    av = pa[:, ATTN_Q_W + ATTN_KV_W:ATTN_Q_W + 2 * ATTN_KV_W]
    avt_ref[...] = av.T.astype(jnp.bfloat16)

    pad = DN_CONV_K // 2
    out_refs = (dq_ref, dk_ref, dv_ref)
    for part in range(3):
        for hh in range(N_DN_HEADS):
            c0 = part * DN_W + hh * DN_DIM
            win = ext_ref[HALO - SUBLANES:HALO + tm + SUBLANES, c0:c0 + DN_DIM]
            nwin = tm + 2 * SUBLANES
            acc = None
            for j in range(DN_CONV_K):
                tap = win if j == pad else pltpu.roll(win, (pad - j) % nwin, axis=0)
                term = tap[SUBLANES:SUBLANES + tm] * cw_ref[j:j + 1, c0:c0 + DN_DIM]
                acc = term if acc is None else acc + term
            y = _silu(acc)
            if part < 2:
                y = y * lax.rsqrt(jnp.sum(y * y, axis=-1, keepdims=True) + EPS)
            if part == 0:
                y = y * (DN_DIM ** -0.5)
            out_refs[part][:, hh * DN_DIM:(hh + 1) * DN_DIM] = y.astype(jnp.bfloat16)

    dz_ref[...] = pz.astype(jnp.bfloat16)

    a_log = gp_ref[0:1, :]
    dt_bias = gp_ref[1:2, :]
    g = -jnp.exp(a_log) * _softplus(pg + dt_bias)
    beta = _sigmoid(pg)
    rin = lax.broadcasted_iota(jnp.int32, (tm, LANES), 0) % CHUNK
    pre = g
    suf = g
    s = 1
    while s < CHUNK:
        pre = pre + jnp.where(rin >= s, pltpu.roll(pre, s, axis=0), 0.0)
        suf = suf + jnp.where(rin < CHUNK - s, pltpu.roll(suf, tm - s, axis=0), 0.0)
        s *= 2
    tot = pre + suf - g
    gcum = jnp.where(lane < N_DN_HEADS, pre, suf)
    tab = jnp.where(lane < GT_BETA, gcum, 0.0)
    tab = jnp.where(jnp.logical_and(lane >= GT_BETA, lane < GT_EG), beta, tab)
    tab = jnp.where(jnp.logical_and(lane >= GT_EG, lane < GT_EGL),
                    pltpu.roll(jnp.exp(gcum), GT_EG, axis=1), tab)
    tab = jnp.where(jnp.logical_and(lane >= GT_EGL, lane < GT_GL),
                    pltpu.roll(jnp.exp(tot - gcum), GT_EGL, axis=1), tab)
    tab = jnp.where(jnp.logical_and(lane >= GT_GL, lane < GT_GL + 8),
                    pltpu.roll(jnp.exp(tot), GT_GL, axis=1), tab)
    beg = pltpu.roll(beta, LANES - GT_BETA, axis=1) * jnp.exp(gcum)
    tab = jnp.where(jnp.logical_and(lane >= GT_BEG, lane < GT_BEG + 8),
                    pltpu.roll(beg, GT_BEG, axis=1), tab)
    gcol_ref[...] = tab
    gt = tab.T
    for c in range(tm // CHUNK):
        blk = gt[0:GT_ROWS, c * CHUNK:(c + 1) * CHUNK]
        grow_ref[c] = jnp.concatenate([blk, blk], axis=1)


def _in_proj(x2, seq_len, n1, w_in, cos, sin, qn, kn, conv_w, gate_p, *, tm):
    n = x2.shape[0]
    tiles_per_seq = seq_len // tm
    nblk8 = n // HALO
    tpb = tm // HALO
    const = lambda i: (0, 0)
    tok = lambda i: (i, 0)
    bf = jnp.bfloat16
    out_shape = (
        jax.ShapeDtypeStruct((n, ATTN_Q_W), bf),
        jax.ShapeDtypeStruct((n, ATTN_KV_W), bf),
        jax.ShapeDtypeStruct((ATTN_KV_W, n), bf),
        jax.ShapeDtypeStruct((n, DN_W), bf),
        jax.ShapeDtypeStruct((n, DN_W), bf),
        jax.ShapeDtypeStruct((n, DN_W), bf),
        jax.ShapeDtypeStruct((n, DN_W), bf),
        jax.ShapeDtypeStruct((n, LANES), jnp.float32),
        jax.ShapeDtypeStruct((n // CHUNK, GT_ROWS, LANES), jnp.float32),
    )
    out_specs = (
        pl.BlockSpec((tm, ATTN_Q_W), tok),
        pl.BlockSpec((tm, ATTN_KV_W), tok),
        pl.BlockSpec((ATTN_KV_W, tm), lambda i: (0, i)),
        pl.BlockSpec((tm, DN_W), tok),
        pl.BlockSpec((tm, DN_W), tok),
        pl.BlockSpec((tm, DN_W), tok),
        pl.BlockSpec((tm, DN_W), tok),
        pl.BlockSpec((tm, LANES), tok),
        pl.BlockSpec((tm // CHUNK, GT_ROWS, LANES), lambda i: (i, 0, 0)),
    )
    in_specs = [
        pl.BlockSpec((HALO, D_MODEL), lambda i: (jnp.maximum(i * tpb - 1, 0), 0)),
        pl.BlockSpec((tm, D_MODEL), tok),
        pl.BlockSpec((HALO, D_MODEL), lambda i: (jnp.minimum((i + 1) * tpb, nblk8 - 1), 0)),
        pl.BlockSpec((1, D_MODEL), const),
        pl.BlockSpec((D_MODEL, IN_W_PAD), const),
        pl.BlockSpec((tm, LANES), lambda i: (i % tiles_per_seq, 0)),
        pl.BlockSpec((tm, LANES), lambda i: (i % tiles_per_seq, 0)),
        pl.BlockSpec((1, HEAD_DIM), const),
        pl.BlockSpec((1, HEAD_DIM), const),
        pl.BlockSpec((DN_CONV_K, 3 * DN_W), const),
        pl.BlockSpec((SUBLANES, LANES), const),
    ]
    return pl.pallas_call(
        functools.partial(_in_proj_kernel, tm=tm, tiles_per_seq=tiles_per_seq),
        grid=(n // tm,),
        in_specs=in_specs,
        out_specs=out_specs,
        out_shape=out_shape,
        scratch_shapes=[pltpu.VMEM((tm + 2 * HALO, 3 * DN_W), jnp.float32),
                        pltpu.VMEM((tm + 2 * HALO, D_MODEL), jnp.bfloat16)],
        compiler_params=pltpu.CompilerParams(dimension_semantics=("arbitrary",),
                                             vmem_limit_bytes=VMEM_LIMIT),
        name="in_proj",
    )(x2, x2, x2, n1, w_in, cos, sin, qn, kn, conv_w, gate_p)


def _attn_kernel(q_ref, k_ref, vt_ref, on_ref, nw_ref, o_ref, s_ref, p_ref, al_ref, cm_ref, m_ref, acc_ref,
                 *, tq, tk, seq_len):
    g = N_ATTN_HEADS // N_KV_HEADS
    nqt = tq // LANES
    nkv = seq_len // tk
    items = nqt * nkv
    kv_shift = nkv.bit_length() - 1
    ones = jnp.ones((SUM_ROWS, tk), jnp.bfloat16)

    def split(t):
        if isinstance(t, int):
            return t // nkv, t % nkv
        return lax.shift_right_logical(t, kv_shift), jnp.bitwise_and(t, nkv - 1)

    def scores(t):
        qi, kj = split(t)
        r0 = _aligned(qi * LANES, LANES)
        k0 = _aligned(kj * tk, tk)
        qblk = q_ref[pl.ds(r0, LANES), :]
        qs = jnp.concatenate([qblk[:, j * HEAD_DIM:(j + 1) * HEAD_DIM] for j in range(g)], axis=0)
        kb = k_ref[pl.ds(k0, tk), :]
        return lax.dot_general(kb, qs, (((1,), (1,)), ((), ())),
                               preferred_element_type=jnp.float32)

    def stage_qk(t, slot):
        s = scores(t)
        s_ref[slot] = s
        cm_ref[slot] = jnp.max(s, axis=0, keepdims=True)

    def stage_qk_exp(t, slot):
        p_ref[slot] = jnp.exp2(scores(t)).astype(jnp.bfloat16)

    def stage_softmax(t, slot):
        _, kj = split(t)
        m_prev = jnp.where(kj == 0, -jnp.inf, m_ref[...])
        m_new = jnp.maximum(m_prev, cm_ref[slot])
        al_ref[slot] = jnp.exp2(m_prev - m_new)
        p_ref[slot] = jnp.exp2(s_ref[slot] - m_new).astype(jnp.bfloat16)
        m_ref[...] = m_new

    def stage_pv(t, slot, may_finish, bounded):
        qi, kj = split(t)
        k0 = _aligned(kj * tk, tk)
        lhs = jnp.concatenate([vt_ref[:, pl.ds(k0, tk)], ones], axis=0)
        if bounded:
            prev = jnp.where(kj == 0, 0.0, acc_ref[...])
        else:
            prev = acc_ref[...] * al_ref[slot]
        acc = prev + jnp.dot(lhs, p_ref[slot], preferred_element_type=jnp.float32)
        acc_ref[...] = acc

        def finalize():
            o = acc[0:HEAD_DIM] / acc[HEAD_DIM:HEAD_DIM + 1]
            o = o * lax.rsqrt(jnp.mean(o * o, axis=0, keepdims=True) + EPS) * on_ref[...]
            ot = o.T
            r0 = _aligned(qi * LANES, LANES)
            for j in range(g):
                o_ref[pl.ds(r0, LANES), j * HEAD_DIM:(j + 1) * HEAD_DIM] = (
                    ot[j * LANES:(j + 1) * LANES, :].astype(jnp.bfloat16))

        if not may_finish:
            return
        if isinstance(kj, int):
            if kj == nkv - 1:
                finalize()
        else:
            pl.when(kj == nkv - 1)(finalize)

    def run_body(u, bounded):
        static = isinstance(u, int)
        n = ATTN_UNROLL_BOUNDED if bounded else ATTN_UNROLL
        for j in range(n):
            may_finish = nkv == 1 or j == n - 1
            if bounded:
                if not static or u + 1 + j < items:
                    stage_qk_exp(u + 1 + j, (1 + j) % n)
            else:
                if not static or u + 2 + j < items:
                    stage_qk(u + 2 + j, (2 + j) % n)
                if not static or u + 1 + j < items:
                    stage_softmax(u + 1 + j, (1 + j) % n)
            stage_pv(u + j, j, may_finish, bounded)

    def pipeline(bounded):
        acc_ref[...] = jnp.zeros_like(acc_ref)
        if bounded:
            stage_qk_exp(0, 0)
        else:
            m_ref[...] = jnp.full(m_ref.shape, -jnp.inf, jnp.float32)
            stage_qk(0, 0)
            stage_qk(1, 1)
            stage_softmax(0, 0)

        n = ATTN_UNROLL_BOUNDED if bounded else ATTN_UNROLL

        def body(i, carry):
            run_body(pl.multiple_of(i * n, n), bounded)
            return carry

        lax.fori_loop(0, items // n - 1, body, 0)
        run_body(items - n, bounded)

    w2 = nw_ref[...] * nw_ref[...]
    bound_sq = (1.02 * (HEAD_DIM * Q_SCALE) ** 2) * jnp.max(w2[0:1]) * jnp.max(w2[1:2])
    bounded = bound_sq <= SCORE_BOUND * SCORE_BOUND
    pl.when(bounded)(lambda: pipeline(True))
    pl.when(jnp.logical_not(bounded))(lambda: pipeline(False))


def _attention(aq, ak, avt, out_norm_col, qk_norm_w, batch, seq_len, *, tq, tk):
    n = aq.shape[0]
    g = N_ATTN_HEADS // N_KV_HEADS
    nq = g * LANES
    qpb = seq_len // tq
    nkv = seq_len // tk
    items = (tq // LANES) * nkv
    assert nkv == 1 or nkv % ATTN_UNROLL_BOUNDED == 0
    assert nkv & (nkv - 1) == 0 and items % ATTN_UNROLL_BOUNDED == 0 and items >= ATTN_UNROLL_BOUNDED
    return pl.pallas_call(
        functools.partial(_attn_kernel, tq=tq, tk=tk, seq_len=seq_len),
        grid=(batch, N_KV_HEADS, qpb),
        in_specs=[
            pl.BlockSpec((tq, g * HEAD_DIM), lambda b, h, i: (b * qpb + i, h)),
            pl.BlockSpec((seq_len, HEAD_DIM), lambda b, h, i: (b, h)),
            pl.BlockSpec((HEAD_DIM, seq_len), lambda b, h, i: (h, b)),
            pl.BlockSpec((HEAD_DIM, 1), lambda b, h, i: (0, 0)),
            pl.BlockSpec((2, HEAD_DIM), lambda b, h, i: (0, 0)),
        ],
        out_specs=pl.BlockSpec((tq, g * HEAD_DIM), lambda b, h, i: (b * qpb + i, h)),
        out_shape=jax.ShapeDtypeStruct((n, ATTN_Q_W), jnp.bfloat16),
        scratch_shapes=[
            pltpu.VMEM((ATTN_UNROLL, tk, nq), jnp.float32),
            pltpu.VMEM((ATTN_UNROLL_BOUNDED, tk, nq), jnp.bfloat16),
            pltpu.VMEM((ATTN_UNROLL, 1, nq), jnp.float32),
            pltpu.VMEM((ATTN_UNROLL, 1, nq), jnp.float32),
            pltpu.VMEM((1, nq), jnp.float32),
            pltpu.VMEM((HEAD_DIM + SUM_ROWS, nq), jnp.float32),
        ],
        compiler_params=pltpu.CompilerParams(
            dimension_semantics=("arbitrary", "arbitrary", "arbitrary"),
            vmem_limit_bytes=56 * 1024 * 1024),
        name="attention",
    )(aq, ak, avt, out_norm_col, qk_norm_w)


def _dn_kernel(qf_ref, kf_ref, vf_ref, tf_ref, rf_ref, qb_ref, kb_ref, vb_ref, tb_ref, rb_ref,
               of_ref, ob_ref,
               s_ref, wq_ref, u_ref, at_ref, kdt_ref, *, nb, nchunk, prep_chunks):
    seg = pl.program_id(1)

    @pl.when(seg == 0)
    def _():
        s_ref[...] = jnp.zeros_like(s_ref)

    ri = lax.broadcasted_iota(jnp.int32, (CHUNK, CHUNK), 0)
    ci = lax.broadcasted_iota(jnp.int32, (CHUNK, CHUNK), 1)
    rw = lax.broadcasted_iota(jnp.int32, (CHUNK, 2 * CHUNK), 0)
    cw = lax.broadcasted_iota(jnp.int32, (CHUNK, 2 * CHUNK), 1)
    hi_half = cw >= CHUNK
    eye_hi = jnp.where(cw == rw + CHUNK, 1.0, 0.0)
    rk = lax.broadcasted_iota(jnp.int32, (DN_DIM, DN_DIM), 0)
    ck = lax.broadcasted_iota(jnp.int32, (DN_DIM, DN_DIM), 1)
    eye_k = jnp.where(rk == ck, 1.0, 0.0).astype(jnp.bfloat16)
    incl = (ri >= ci, ri <= ci)
    strict = (ri > ci, ri < ci)
    ins = ((qf_ref, kf_ref, vf_ref, tf_ref, rf_ref), (qb_ref, kb_ref, vb_ref, tb_ref, rb_ref))
    outs = (of_ref, ob_ref)
    bf = jnp.bfloat16
    probs = [(bb, d, hh) for bb in range(nb) for d in range(2) for hh in range(N_DN_HEADS)]

    def mm(a, b):
        return jnp.dot(a.astype(bf), b.astype(bf), preferred_element_type=jnp.float32)

    def prep(ci_, carry):
        units = []
        for pc in range(prep_chunks):
            c = ci_ * prep_chunks + pc
            r0 = pl.multiple_of(c * CHUNK, CHUNK)
            for bb, d, hh in probs:
                q_ref, k_ref, v_ref, t_ref, g_ref = ins[d]
                col = d * N_DN_HEADS + hh
                lo = hh * DN_DIM
                tab = t_ref[bb, pl.ds(r0, CHUNK), :]
                k = k_ref[bb, pl.ds(r0, CHUNK), lo:lo + DN_DIM]
                q = q_ref[bb, pl.ds(r0, CHUNK), lo:lo + DN_DIM]
                rtab = g_ref[bb, c]
                units.append(dict(
                    d=d, unit=(bb * 2 * N_DN_HEADS + col) * nchunk + c, q=q, k=k,
                    v=v_ref[bb, pl.ds(r0, CHUNK), lo:lo + DN_DIM],
                    gc=tab[:, GT_G + col:GT_G + col + 1],
                    beta=tab[:, GT_BETA + col:GT_BETA + col + 1],
                    eg=tab[:, GT_EG + col:GT_EG + col + 1],
                    gr=rtab[GT_G + col:GT_G + col + 1, 0:CHUNK],
                    beta_r=rtab[GT_BETA + col:GT_BETA + col + 1, :],
                    beg_r=rtab[GT_BEG + col:GT_BEG + col + 1, :],
                    egl_r=rtab[GT_EGL + col:GT_EGL + col + 1, 0:CHUNK]))
        for u in units:
            kq = jnp.concatenate([u["k"], u["q"]], axis=0)
            u["kkqk"] = lax.dot_general(kq, u["k"], (((1,), (1,)), ((), ())),
                                        preferred_element_type=jnp.float32)
            u["kt"] = lax.dot_general(eye_k, u["k"], (((1,), (1,)), ((), ())),
                                      preferred_element_type=jnp.float32)
        for u in units:
            d = u["d"]
            dec = jnp.exp(jnp.minimum(u["gc"] - u["gr"], 0.0))
            x = jnp.where(strict[d], -(u["beta"] * u["kkqk"][0:CHUNK]) * dec, 0.0)
            attn = jnp.where(incl[d], u["kkqk"][CHUNK:2 * CHUNK] * dec, 0.0)
            at_ref[u["unit"]] = attn.astype(bf)
            wq_ref[u["unit"], CHUNK:2 * CHUNK, :] = (u["q"].astype(jnp.float32) * u["eg"]).astype(bf)
            kdt_ref[u["unit"]] = (u["kt"] * u["egl_r"]).astype(bf)
            u["z"] = jnp.concatenate([x, jnp.zeros_like(x)], axis=1) + eye_hi
        for _ in range(6):
            for u in units:
                z = u["z"]
                u["z"] = mm(z[:, 0:CHUNK], z) + jnp.where(hi_half, z, 0.0)
        zeros = jnp.zeros((CHUNK, DN_DIM), bf)
        for u in units:
            z = u["z"]
            u_ref[u["unit"]] = jnp.dot((z * u["beta_r"]).astype(bf),
                                       jnp.concatenate([zeros, u["v"]], axis=0),
                                       preferred_element_type=jnp.float32)
            wq_ref[u["unit"], 0:CHUNK, :] = jnp.dot((z * u["beg_r"]).astype(bf),
                                                    jnp.concatenate([zeros, u["k"]], axis=0),
                                                    preferred_element_type=jnp.float32).astype(bf)
        return carry

    lax.fori_loop(0, nchunk // prep_chunks, prep, 0)

    def scan(step, carry):
        chains = []
        for bb, d, hh in probs:
            c = step if d == 0 else nchunk - 1 - step
            r0 = pl.multiple_of(c * CHUNK, CHUNK)
            col = d * N_DN_HEADS + hh
            sidx = bb * 2 * N_DN_HEADS + col
            tabrow = ins[d][3][bb, pl.ds(r0, 1), :]
            chains.append(dict(bb=bb, d=d, hh=hh, sidx=sidx, unit=sidx * nchunk + c, r0=r0,
                               gl=tabrow[:, GT_GL + col:GT_GL + col + 1]))
        for ch in chains:
            ch["st"] = s_ref[ch["sidx"]]
            ch["res"] = jnp.dot(wq_ref[ch["unit"]], ch["st"].astype(bf),
                                preferred_element_type=jnp.float32)
        for ch in chains:
            ch["vn"] = (u_ref[ch["unit"]] - ch["res"][0:CHUNK]).astype(bf)
        for ch in chains:
            ch["o"] = jnp.dot(at_ref[ch["unit"]], ch["vn"], preferred_element_type=jnp.float32)
            ch["upd"] = jnp.dot(kdt_ref[ch["unit"]], ch["vn"], preferred_element_type=jnp.float32)
        for ch in chains:
            hh = ch["hh"]
            o = ch["res"][CHUNK:2 * CHUNK] + ch["o"]
            outs[ch["d"]][ch["bb"], pl.ds(ch["r0"], CHUNK), hh * DN_DIM:(hh + 1) * DN_DIM] = o.astype(bf)
            s_ref[ch["sidx"]] = ch["st"] * ch["gl"] + ch["upd"]
        return carry

    lax.fori_loop(0, nchunk, scan, 0)


def _deltanet(dq, dk, dv, gcol, grow, batch, seq_len, *, seg, prep_chunks, nb):
    nseg = seq_len // seg
    nchunk = seg // CHUNK
    units = nb * 2 * N_DN_HEADS * nchunk
    fwd = lambda b, s: (b, s, 0)
    bwd = lambda b, s: (b, nseg - 1 - s, 0)
    fwd4 = lambda b, s: (b, s, 0, 0)
    bwd4 = lambda b, s: (b, nseg - 1 - s, 0, 0)
    big = lambda m: pl.BlockSpec((nb, seg, DN_W), m)
    tabs = lambda m: pl.BlockSpec((nb, seg, LANES), m)
    rows = lambda m: pl.BlockSpec((nb, nchunk, GT_ROWS, LANES), m)
    bf = jnp.bfloat16
    out = jax.ShapeDtypeStruct((batch, seq_len, DN_W), bf)
    return pl.pallas_call(
        functools.partial(_dn_kernel, nb=nb, nchunk=nchunk, prep_chunks=prep_chunks),
        grid=(batch // nb, nseg),
        in_specs=[big(fwd), big(fwd), big(fwd), tabs(fwd), rows(fwd4),
                  big(bwd), big(bwd), big(bwd), tabs(bwd), rows(bwd4)],
        out_specs=(big(fwd), big(bwd)),
        out_shape=(out, out),
        scratch_shapes=[
            pltpu.VMEM((nb * 2 * N_DN_HEADS, DN_DIM, DN_DIM), jnp.float32),
            pltpu.VMEM((units, 2 * CHUNK, DN_DIM), bf),
            pltpu.VMEM((units, CHUNK, DN_DIM), jnp.float32),
            pltpu.VMEM((units, CHUNK, CHUNK), bf),
            pltpu.VMEM((units, DN_DIM, CHUNK), bf),
        ],
        compiler_params=pltpu.CompilerParams(dimension_semantics=("arbitrary", "arbitrary"),
                                             vmem_limit_bytes=VMEM_LIMIT),
        name="deltanet",
    )(dq, dk, dv, gcol, grow, dq, dk, dv, gcol, grow)


def _mix_kernel(x_ref, ao_ref, of_ref, ob_ref, z_ref, dnw_ref, wo_ref, n2_ref, x1_ref, h2_ref):
    dnw = dnw_ref[...]
    parts = []
    for hh in range(N_DN_HEADS):
        sl = slice(hh * DN_DIM, (hh + 1) * DN_DIM)
        o = of_ref[:, sl].astype(jnp.float32) + ob_ref[:, sl].astype(jnp.float32)
        z = z_ref[:, sl].astype(jnp.float32)
        parts.append((_rms_rows(o, dnw) * _silu(z)).astype(jnp.bfloat16))
    dn = jnp.concatenate(parts, axis=1)
    y = jnp.dot(ao_ref[...], wo_ref[0:ATTN_Q_W, :], preferred_element_type=jnp.float32)
    y = y + jnp.dot(dn, wo_ref[ATTN_Q_W:ATTN_Q_W + DN_W, :], preferred_element_type=jnp.float32)
    x1 = x_ref[...] + y
    x1_ref[...] = x1
    h2_ref[...] = _rms_rows(x1, n2_ref[...]).astype(jnp.bfloat16)


def _mix_out(x2, ao, o_f, o_b, dz, dn_norm, w_out, n2, *, tm):
    n = x2.shape[0]
    tok = lambda i: (i, 0)
    const = lambda i: (0, 0)
    half = pl.BlockSpec((tm, DN_W), tok)
    return pl.pallas_call(
        _mix_kernel,
        grid=(n // tm,),
        in_specs=[pl.BlockSpec((tm, D_MODEL), tok), half, half, half, half,
                  pl.BlockSpec((1, DN_DIM), const),
                  pl.BlockSpec((ATTN_Q_W + DN_W, D_MODEL), const),
                  pl.BlockSpec((1, D_MODEL), const)],
        out_specs=(pl.BlockSpec((tm, D_MODEL), tok), pl.BlockSpec((tm, D_MODEL), tok)),
        out_shape=(jax.ShapeDtypeStruct((n, D_MODEL), jnp.float32),
                   jax.ShapeDtypeStruct((n, D_MODEL), jnp.bfloat16)),
        compiler_params=pltpu.CompilerParams(dimension_semantics=("arbitrary",),
                                             vmem_limit_bytes=VMEM_LIMIT),
        name="mix_out",
    )(x2, ao, o_f, o_b, dz, dn_norm, w_out, n2)


def _ffn_kernel(hp_ref, h_ref, hn_ref, x1_ref, wg_ref, wu_ref, cwg_ref, cwu_ref, bg_ref, bu_ref,
                wd_ref, y_ref, acc_ref, he_ref, ug_ref, uu_ref, act_ref, *, tm, fc, tiles_per_seq):
    i = pl.program_id(0)
    first = (i % tiles_per_seq) == 0
    last = (i % tiles_per_seq) == tiles_per_seq - 1
    rows = tm + 2 * HALO
    he_ref[0:HALO, :] = jnp.where(first, jnp.zeros_like(hp_ref), hp_ref[...])
    he_ref[HALO:HALO + tm, :] = h_ref[...]
    he_ref[HALO + tm:rows, :] = jnp.where(last, jnp.zeros_like(hn_ref), hn_ref[...])

    def cols(c):
        return c * fc if isinstance(c, int) else pl.multiple_of(c * fc, fc)

    def conv(u_ref, slot, a, b, cw_ref, b_ref, sl):
        w = u_ref[slot, a + HALO - SUBLANES:b + HALO + SUBLANES, :]
        n = b - a
        prev = pltpu.roll(w, 1, axis=0)[SUBLANES:SUBLANES + n]
        nxt = pltpu.roll(w, n + 2 * SUBLANES - 1, axis=0)[SUBLANES:SUBLANES + n]
        mid = w[SUBLANES:SUBLANES + n]
        return prev * cw_ref[0:1, sl] + mid * cw_ref[1:2, sl] + nxt * cw_ref[2:3, sl] + b_ref[0:1, sl]

    def up(c, slot):
        sl = pl.ds(cols(c), fc)
        for a, b in _row_blocks(rows, FFN_ROW_BLOCKS):
            ug_ref[slot, a:b, :] = jnp.dot(he_ref[a:b, :], wg_ref[:, sl], preferred_element_type=jnp.float32)
            uu_ref[slot, a:b, :] = jnp.dot(he_ref[a:b, :], wu_ref[:, sl], preferred_element_type=jnp.float32)

    def gate(c, slot):
        sl = pl.ds(cols(c), fc)
        for a, b in _row_blocks(tm, FFN_ROW_BLOCKS):
            act_ref[slot, a:b, :] = (_silu(conv(ug_ref, slot, a, b, cwg_ref, bg_ref, sl))
                                     * conv(uu_ref, slot, a, b, cwu_ref, bu_ref, sl)).astype(jnp.bfloat16)

    def down(c, slot):
        acc_ref[...] += jnp.dot(act_ref[slot], wd_ref[pl.ds(cols(c), fc), :],
                                preferred_element_type=jnp.float32)

    nchunks = D_FF_PAD // fc
    acc_ref[...] = jnp.zeros_like(acc_ref)
    up(0, 0)
    for c in range(nchunks):
        if c + 1 < nchunks:
            up(c + 1, (c + 1) % 2)
        gate(c, c % 2)
        down(c, c % 2)
    y_ref[...] = x1_ref[...] + acc_ref[...]


def _ffn(h2, x1, seq_len, wg, wu, cwg, cwu, bg, bu, wd, *, tm, fc):
    n = h2.shape[0]
    tiles_per_seq = seq_len // tm
    nblk = n // HALO
    tpb = tm // HALO
    tok = lambda i: (i, 0)
    const = lambda i: (0, 0)
    return pl.pallas_call(
        functools.partial(_ffn_kernel, tm=tm, fc=fc, tiles_per_seq=tiles_per_seq),
        grid=(n // tm,),
        in_specs=[
            pl.BlockSpec((HALO, D_MODEL), lambda i: (jnp.maximum(i * tpb - 1, 0), 0)),
            pl.BlockSpec((tm, D_MODEL), tok),
            pl.BlockSpec((HALO, D_MODEL), lambda i: (jnp.minimum((i + 1) * tpb, nblk - 1), 0)),
            pl.BlockSpec((tm, D_MODEL), tok),
            pl.BlockSpec((D_MODEL, D_FF_PAD), const, pipeline_mode=pl.Buffered(1)),
            pl.BlockSpec((D_MODEL, D_FF_PAD), const, pipeline_mode=pl.Buffered(1)),
            pl.BlockSpec((3, D_FF_PAD), const),
            pl.BlockSpec((3, D_FF_PAD), const),
            pl.BlockSpec((1, D_FF_PAD), const),
            pl.BlockSpec((1, D_FF_PAD), const),
            pl.BlockSpec((D_FF_PAD, D_MODEL), const, pipeline_mode=pl.Buffered(1)),
        ],
        out_specs=pl.BlockSpec((tm, D_MODEL), tok),
        out_shape=jax.ShapeDtypeStruct((n, D_MODEL), jnp.float32),
        scratch_shapes=[pltpu.VMEM((tm, D_MODEL), jnp.float32),
                        pltpu.VMEM((tm + 2 * HALO, D_MODEL), jnp.bfloat16),
                        pltpu.VMEM((2, tm + 2 * HALO, fc), jnp.float32),
                        pltpu.VMEM((2, tm + 2 * HALO, fc), jnp.float32),
                        pltpu.VMEM((2, tm, fc), jnp.bfloat16)],
        compiler_params=pltpu.CompilerParams(dimension_semantics=("arbitrary",),
                                             vmem_limit_bytes=56 * 1024 * 1024),
        name="ffn",
    )(h2, h2, h2, x1, wg, wu, cwg, cwu, bg, bu, wd)


def _rope_tables(seq_len):
    half = HEAD_DIM // 2
    f32 = np.float32
    inv = f32(ROPE_THETA) ** (-np.arange(0, half, 2, dtype=f32) / f32(half))
    t = np.arange(seq_len, dtype=np.int32)
    rows = (t // GRID_W).astype(f32)[:, None] * inv[None, :]
    cols = (t % GRID_W).astype(f32)[:, None] * inv[None, :]
    cos = np.concatenate([np.cos(rows), np.cos(rows), np.cos(cols), np.cos(cols)], axis=1)
    sin = np.concatenate([-np.sin(rows), np.sin(rows), -np.sin(cols), np.sin(cols)], axis=1)
    return jnp.asarray(cos, jnp.float32), jnp.asarray(sin, jnp.float32)


def _pad_cols(a, width):
    return jnp.pad(a, ((0, 0), (0, width - a.shape[1])))


def _prepare(norm1_w, w_in, dn_conv_w, dn_A_log_f, dn_A_log_b, dn_dt_bias_f, dn_dt_bias_b, dn_norm_w,
             attn_q_norm_w, attn_k_norm_w, attn_out_norm_w, w_out, norm2_w, w_ffn_in, ffn_conv_w,
             ffn_conv_b, w_ffn_out):
    bf = jnp.bfloat16
    gate_p = jnp.zeros((SUBLANES, LANES), jnp.float32)
    gate_p = gate_p.at[0, 0:4].set(dn_A_log_f).at[0, 4:8].set(dn_A_log_b)
    gate_p = gate_p.at[1, 0:4].set(dn_dt_bias_f).at[1, 4:8].set(dn_dt_bias_b)
    return dict(
        n1=norm1_w.reshape(1, D_MODEL),
        w_in=_pad_cols(w_in, IN_W_PAD).astype(bf),
        conv_w=dn_conv_w,
        gate_p=gate_p,
        dn_norm=dn_norm_w.reshape(1, DN_DIM),
        qn=attn_q_norm_w.reshape(1, HEAD_DIM),
        kn=attn_k_norm_w.reshape(1, HEAD_DIM),
        on=attn_out_norm_w.reshape(HEAD_DIM, 1),
        w_out=w_out.astype(bf),
        n2=norm2_w.reshape(1, D_MODEL),
        wg=_pad_cols(w_ffn_in[:, :D_FF], D_FF_PAD).astype(bf),
        wu=_pad_cols(w_ffn_in[:, D_FF:], D_FF_PAD).astype(bf),
        cwg=_pad_cols(ffn_conv_w[:, :D_FF], D_FF_PAD),
        cwu=_pad_cols(ffn_conv_w[:, D_FF:], D_FF_PAD),
        bg=_pad_cols(ffn_conv_b[None, :D_FF], D_FF_PAD),
        bu=_pad_cols(ffn_conv_b[None, D_FF:], D_FF_PAD),
        wd=jnp.pad(w_ffn_out, ((0, D_FF_PAD - D_FF), (0, 0))).astype(bf),
    )


def _tiles(seq_len):
    return dict(
        tm_in=min(512, seq_len),
        tq=min(2048, seq_len),
        tk=min(2048, seq_len),
        seg=min(256, seq_len),
        prep_batch=4,
        dn_batch=4,
        tm_mix=min(1024, seq_len),
        tm_ffn=min(1024, seq_len),
        fc=256,
    )


def _layer(x, p):
    batch, seq_len, _ = x.shape
    t = _tiles(seq_len)
    x2 = x.reshape(batch * seq_len, D_MODEL)
    cos, sin = _rope_tables(seq_len)
    aq, ak, avt, dq, dk, dv, dz, gcol, grow = _in_proj(
        x2, seq_len, p["n1"], p["w_in"], cos, sin, p["qn"], p["kn"], p["conv_w"], p["gate_p"],
        tm=t["tm_in"])
    ao = _attention(aq, ak, avt, p["on"], jnp.concatenate([p["qn"], p["kn"]], axis=0),
                    batch, seq_len, tq=t["tq"], tk=t["tk"])
    n = batch * seq_len
    nb = min(t["dn_batch"], batch)
    o_f, o_b = _deltanet(dq.reshape(batch, seq_len, DN_W), dk.reshape(batch, seq_len, DN_W),
                         dv.reshape(batch, seq_len, DN_W), gcol.reshape(batch, seq_len, LANES),
                         grow.reshape(batch, seq_len // CHUNK, GT_ROWS, LANES),
                         batch, seq_len, seg=t["seg"], prep_chunks=max(1, t["prep_batch"] // nb), nb=nb)
    o_f = o_f.reshape(n, DN_W)
    o_b = o_b.reshape(n, DN_W)
    x1, h2 = _mix_out(x2, ao, o_f, o_b, dz, p["dn_norm"], p["w_out"], p["n2"], tm=t["tm_mix"])
    y = _ffn(h2, x1, seq_len, p["wg"], p["wu"], p["cwg"], p["cwu"], p["bg"], p["bu"], p["wd"],
             tm=t["tm_ffn"], fc=t["fc"])
    return y.reshape(batch, seq_len, D_MODEL)


def kernel(x_prompt, x_sample, norm1_w, w_in, dn_conv_w, dn_A_log_f, dn_A_log_b, dn_dt_bias_f, dn_dt_bias_b, dn_norm_w, attn_q_norm_w, attn_k_norm_w, attn_out_norm_w, w_out, norm2_w, w_ffn_in, ffn_conv_w, ffn_conv_b, w_ffn_out):
    depth = norm1_w.shape[0]

    def trunk(x):
        for l in range(depth):
            p = _prepare(norm1_w[l], w_in[l], dn_conv_w[l], dn_A_log_f[l], dn_A_log_b[l],
                         dn_dt_bias_f[l], dn_dt_bias_b[l], dn_norm_w[l], attn_q_norm_w[l],
                         attn_k_norm_w[l], attn_out_norm_w[l], w_out[l], norm2_w[l], w_ffn_in[l],
                         ffn_conv_w[l], ffn_conv_b[l], w_ffn_out[l])
            x = _layer(x, p)
        return x

    return (trunk(x_prompt), trunk(x_sample))
```

```python
import functools
import math

import jax
import jax.numpy as jnp
import numpy as np
from jax import lax
from jax.experimental import pallas as pl
from jax.experimental.pallas import tpu as pltpu

D_MODEL = 1024
GRID_W = 64
HEAD_DIM = 128
N_ATTN_HEADS = 4
N_KV_HEADS = 2
N_DN_HEADS = 4
DN_DIM = 128
DN_CONV_K = 5
CHUNK = 64
ROPE_THETA = 10000.0
D_FF = 2752
EPS = 1e-6

ATTN_Q_W = N_ATTN_HEADS * HEAD_DIM
ATTN_KV_W = N_KV_HEADS * HEAD_DIM
DN_W = N_DN_HEADS * DN_DIM
IN_W = ATTN_Q_W + 2 * ATTN_KV_W + 4 * DN_W + 4 * N_DN_HEADS

LANES = 128
SUBLANES = 8
IN_W_PAD = 3200
D_FF_PAD = 2816
GATE_OFF = ATTN_Q_W + 2 * ATTN_KV_W + 4 * DN_W
HALO = 2 * SUBLANES

VMEM_LIMIT = 48 * 1024 * 1024

GT_G, GT_BETA, GT_EG, GT_EGL, GT_GL, GT_BEG = 0, 8, 16, 24, 32, 40
GT_ROWS = 48

SUM_ROWS = 2 * SUBLANES
ATTN_UNROLL = 4
ATTN_UNROLL_BOUNDED = 8
IN_ROW_BLOCKS = 2
FFN_ROW_BLOCKS = 4
Q_SCALE = (HEAD_DIM ** -0.5) * math.log2(math.e)
SCORE_BOUND = 60.0


def _sigmoid(x):
    return 0.5 * jnp.tanh(0.5 * x) + 0.5


def _silu(x):
    return x * _sigmoid(x)


def _softplus(x):
    return jnp.maximum(x, 0.0) + jnp.log(1.0 + jnp.exp(-jnp.abs(x)))


def _rms_rows(x, w):
    return x * lax.rsqrt(jnp.mean(x * x, axis=-1, keepdims=True) + EPS) * w


def _row_blocks(nrows, nblocks):
    tile = 2 * SUBLANES
    cuts = [-(-(nrows * b // nblocks) // tile) * tile for b in range(nblocks)] + [nrows]
    return list(zip(cuts[:-1], cuts[1:]))


def _aligned(x, m):
    return x if isinstance(x, int) else pl.multiple_of(x, m)


def _in_proj_kernel(xp_ref, x_ref, xn_ref, n1_ref, w_ref, cos_ref, sin_ref, qn_ref, kn_ref,
                    cw_ref, gp_ref,
                    aq_ref, ak_ref, avt_ref, dq_ref, dk_ref, dv_ref, dz_ref, gcol_ref, grow_ref,
                    ext_ref, he_ref, *, tm, tiles_per_seq):
    i = pl.program_id(0)
    first = (i % tiles_per_seq) == 0
    last = (i % tiles_per_seq) == tiles_per_seq - 1

    n1 = n1_ref[...]
    xp = jnp.where(first, jnp.zeros_like(xp_ref), xp_ref[...])
    xn = jnp.where(last, jnp.zeros_like(xn_ref), xn_ref[...])
    he_ref[0:HALO, :] = _rms_rows(xp, n1).astype(jnp.bfloat16)
    he_ref[HALO:HALO + tm, :] = _rms_rows(x_ref[...], n1).astype(jnp.bfloat16)
    he_ref[HALO + tm:tm + 2 * HALO, :] = _rms_rows(xn, n1).astype(jnp.bfloat16)
    h_rows = slice(HALO, HALO + tm)

    cos = cos_ref[...]
    sin = sin_ref[...]
    lane = lax.broadcasted_iota(jnp.int32, (tm, LANES), 1)
    low_half = (lane % 64) < 32

    def rope(xh):
        perm = jnp.where(low_half, pltpu.roll(xh, 96, axis=1), pltpu.roll(xh, 32, axis=1))
        return xh * cos + perm * sin

    def proj(lo, width, r0, nrows):
        return jnp.concatenate(
            [jnp.dot(he_ref[r0 + a:r0 + b, :], w_ref[:, lo:lo + width],
                     preferred_element_type=jnp.float32)
             for a, b in _row_blocks(nrows, IN_ROW_BLOCKS)], axis=0)

    pa = proj(0, ATTN_Q_W + 2 * ATTN_KV_W, HALO, tm)
    dn_off = ATTN_Q_W + 2 * ATTN_KV_W
    for part in range(3):
        ext_ref[:, part * DN_W:(part + 1) * DN_W] = proj(dn_off + part * DN_W, DN_W, 0, tm + 2 * HALO)
    pz = proj(dn_off + 3 * DN_W, DN_W, HALO, tm)
    pg = proj(GATE_OFF, LANES, HALO, tm)

    qn = qn_ref[...]
    kn = kn_ref[...]
    for hh in range(N_ATTN_HEADS):
        xh = pa[:, hh * HEAD_DIM:(hh + 1) * HEAD_DIM]
        aq_ref[:, hh * HEAD_DIM:(hh + 1) * HEAD_DIM] = (rope(_rms_rows(xh, qn)) * Q_SCALE).astype(jnp.bfloat16)
    for hh in range(N_KV_HEADS):
        o = ATTN_Q_W + hh * HEAD_DIM
        xh = pa[:, o:o + HEAD_DIM]
        ak_ref[:, hh * HEAD_DIM:(hh + 1) * HEAD_DIM] = rope(_rms_rows(xh, kn)).astype(jnp.bfloat16)
    av = pa[:, ATTN_Q_W + ATTN_KV_W:ATTN_Q_W + 2 * ATTN_KV_W]
    avt_ref[...] = av.T.astype(jnp.bfloat16)

    pad = DN_CONV_K // 2
    out_refs = (dq_ref, dk_ref, dv_ref)
    for part in range(3):
        for hh in range(N_DN_HEADS):
            c0 = part * DN_W + hh * DN_DIM
            win = ext_ref[HALO - SUBLANES:HALO + tm + SUBLANES, c0:c0 + DN_DIM]
            nwin = tm + 2 * SUBLANES
            acc = None
            for j in range(DN_CONV_K):
                tap = win if j == pad else pltpu.roll(win, (pad - j) % nwin, axis=0)
                term = tap[SUBLANES:SUBLANES + tm] * cw_ref[j:j + 1, c0:c0 + DN_DIM]
                acc = term if acc is None else acc + term
            y = _silu(acc)
            if part < 2:
                y = y * lax.rsqrt(jnp.sum(y * y, axis=-1, keepdims=True) + EPS)
            if part == 0:
                y = y * (DN_DIM ** -0.5)
            out_refs[part][:, hh * DN_DIM:(hh + 1) * DN_DIM] = y.astype(jnp.bfloat16)

    dz_ref[...] = pz.astype(jnp.bfloat16)

    a_log = gp_ref[0:1, :]
    dt_bias = gp_ref[1:2, :]
    g = -jnp.exp(a_log) * _softplus(pg + dt_bias)
    beta = _sigmoid(pg)
    rin = lax.broadcasted_iota(jnp.int32, (tm, LANES), 0) % CHUNK
    pre = g
    suf = g
    s = 1
    while s < CHUNK:
        pre = pre + jnp.where(rin >= s, pltpu.roll(pre, s, axis=0), 0.0)
        suf = suf + jnp.where(rin < CHUNK - s, pltpu.roll(suf, tm - s, axis=0), 0.0)
        s *= 2
    tot = pre + suf - g
    gcum = jnp.where(lane < N_DN_HEADS, pre, suf)
    tab = jnp.where(lane < GT_BETA, gcum, 0.0)
    tab = jnp.where(jnp.logical_and(lane >= GT_BETA, lane < GT_EG), beta, tab)
    tab = jnp.where(jnp.logical_and(lane >= GT_EG, lane < GT_EGL),
                    pltpu.roll(jnp.exp(gcum), GT_EG, axis=1), tab)
    tab = jnp.where(jnp.logical_and(lane >= GT_EGL, lane < GT_GL),
                    pltpu.roll(jnp.exp(tot - gcum), GT_EGL, axis=1), tab)
    tab = jnp.where(jnp.logical_and(lane >= GT_GL, lane < GT_GL + 8),
                    pltpu.roll(jnp.exp(tot), GT_GL, axis=1), tab)
    beg = pltpu.roll(beta, LANES - GT_BETA, axis=1) * jnp.exp(gcum)
    tab = jnp.where(jnp.logical_and(lane >= GT_BEG, lane < GT_BEG + 8),
                    pltpu.roll(beg, GT_BEG, axis=1), tab)
    gcol_ref[...] = tab
    gt = tab.T
    for c in range(tm // CHUNK):
        blk = gt[0:GT_ROWS, c * CHUNK:(c + 1) * CHUNK]
        grow_ref[c] = jnp.concatenate([blk, blk], axis=1)


def _in_proj(x2, seq_len, n1, w_in, cos, sin, qn, kn, conv_w, gate_p, *, tm):
    n = x2.shape[0]
    tiles_per_seq = seq_len // tm
    nblk8 = n // HALO
    tpb = tm // HALO
    const = lambda i: (0, 0)
    tok = lambda i: (i, 0)
    bf = jnp.bfloat16
    out_shape = (
        jax.ShapeDtypeStruct((n, ATTN_Q_W), bf),
        jax.ShapeDtypeStruct((n, ATTN_KV_W), bf),
        jax.ShapeDtypeStruct((ATTN_KV_W, n), bf),
        jax.ShapeDtypeStruct((n, DN_W), bf),
        jax.ShapeDtypeStruct((n, DN_W), bf),
        jax.ShapeDtypeStruct((n, DN_W), bf),
        jax.ShapeDtypeStruct((n, DN_W), bf),
        jax.ShapeDtypeStruct((n, LANES), jnp.float32),
        jax.ShapeDtypeStruct((n // CHUNK, GT_ROWS, LANES), jnp.float32),
    )
    out_specs = (
        pl.BlockSpec((tm, ATTN_Q_W), tok),
        pl.BlockSpec((tm, ATTN_KV_W), tok),
        pl.BlockSpec((ATTN_KV_W, tm), lambda i: (0, i)),
        pl.BlockSpec((tm, DN_W), tok),
        pl.BlockSpec((tm, DN_W), tok),
        pl.BlockSpec((tm, DN_W), tok),
        pl.BlockSpec((tm, DN_W), tok),
        pl.BlockSpec((tm, LANES), tok),
        pl.BlockSpec((tm // CHUNK, GT_ROWS, LANES), lambda i: (i, 0, 0)),
    )
    in_specs = [
        pl.BlockSpec((HALO, D_MODEL), lambda i: (jnp.maximum(i * tpb - 1, 0), 0)),
        pl.BlockSpec((tm, D_MODEL), tok),
        pl.BlockSpec((HALO, D_MODEL), lambda i: (jnp.minimum((i + 1) * tpb, nblk8 - 1), 0)),
        pl.BlockSpec((1, D_MODEL), const),
        pl.BlockSpec((D_MODEL, IN_W_PAD), const),
        pl.BlockSpec((tm, LANES), lambda i: (i % tiles_per_seq, 0)),
        pl.BlockSpec((tm, LANES), lambda i: (i % tiles_per_seq, 0)),
        pl.BlockSpec((1, HEAD_DIM), const),
        pl.BlockSpec((1, HEAD_DIM), const),
        pl.BlockSpec((DN_CONV_K, 3 * DN_W), const),
        pl.BlockSpec((SUBLANES, LANES), const),
    ]
    return pl.pallas_call(
        functools.partial(_in_proj_kernel, tm=tm, tiles_per_seq=tiles_per_seq),
        grid=(n // tm,),
        in_specs=in_specs,
        out_specs=out_specs,
        out_shape=out_shape,
        scratch_shapes=[pltpu.VMEM((tm + 2 * HALO, 3 * DN_W), jnp.float32),
                        pltpu.VMEM((tm + 2 * HALO, D_MODEL), jnp.bfloat16)],
        compiler_params=pltpu.CompilerParams(dimension_semantics=("arbitrary",),
                                             vmem_limit_bytes=VMEM_LIMIT),
        name="in_proj",
    )(x2, x2, x2, n1, w_in, cos, sin, qn, kn, conv_w, gate_p)


def _attn_kernel(q_ref, k_ref, vt_ref, on_ref, nw_ref, o_ref, s_ref, p_ref, al_ref, cm_ref, m_ref, acc_ref,
                 *, tq, tk, seq_len):
    g = N_ATTN_HEADS // N_KV_HEADS
    nqt = tq // LANES
    nkv = seq_len // tk
    items = nqt * nkv
    kv_shift = nkv.bit_length() - 1
    ones = jnp.ones((SUM_ROWS, tk), jnp.bfloat16)

    def split(t):
        if isinstance(t, int):
            return t // nkv, t % nkv
        return lax.shift_right_logical(t, kv_shift), jnp.bitwise_and(t, nkv - 1)

    def scores(t):
        qi, kj = split(t)
        r0 = _aligned(qi * LANES, LANES)
        k0 = _aligned(kj * tk, tk)
        qblk = q_ref[pl.ds(r0, LANES), :]
        qs = jnp.concatenate([qblk[:, j * HEAD_DIM:(j + 1) * HEAD_DIM] for j in range(g)], axis=0)
        kb = k_ref[pl.ds(k0, tk), :]
        return lax.dot_general(kb, qs, (((1,), (1,)), ((), ())),
                               preferred_element_type=jnp.float32)

    def stage_qk(t, slot):
        s = scores(t)
        s_ref[slot] = s
        cm_ref[slot] = jnp.max(s, axis=0, keepdims=True)

    def stage_qk_exp(t, slot):
        p_ref[slot] = jnp.exp2(scores(t)).astype(jnp.bfloat16)

    def stage_softmax(t, slot):
        _, kj = split(t)
        m_prev = jnp.where(kj == 0, -jnp.inf, m_ref[...])
        m_new = jnp.maximum(m_prev, cm_ref[slot])
        al_ref[slot] = jnp.exp2(m_prev - m_new)
        p_ref[slot] = jnp.exp2(s_ref[slot] - m_new).astype(jnp.bfloat16)
        m_ref[...] = m_new

    def stage_pv(t, slot, may_finish, bounded):
        qi, kj = split(t)
        k0 = _aligned(kj * tk, tk)
        lhs = jnp.concatenate([vt_ref[:, pl.ds(k0, tk)], ones], axis=0)
        if bounded:
            prev = jnp.where(kj == 0, 0.0, acc_ref[...])
        else:
            prev = acc_ref[...] * al_ref[slot]
        acc = prev + jnp.dot(lhs, p_ref[slot], preferred_element_type=jnp.float32)
        acc_ref[...] = acc

        def finalize():
            o = acc[0:HEAD_DIM] / acc[HEAD_DIM:HEAD_DIM + 1]
            o = o * lax.rsqrt(jnp.mean(o * o, axis=0, keepdims=True) + EPS) * on_ref[...]
            ot = o.T
            r0 = _aligned(qi * LANES, LANES)
            for j in range(g):
                o_ref[pl.ds(r0, LANES), j * HEAD_DIM:(j + 1) * HEAD_DIM] = (
                    ot[j * LANES:(j + 1) * LANES, :].astype(jnp.bfloat16))

        if not may_finish:
            return
        if isinstance(kj, int):
            if kj == nkv - 1:
                finalize()
        else:
            pl.when(kj == nkv - 1)(finalize)

    def run_body(u, bounded):
        static = isinstance(u, int)
        n = ATTN_UNROLL_BOUNDED if bounded else ATTN_UNROLL
        for j in range(n):
            may_finish = nkv == 1 or j == n - 1
            if bounded:
                if not static or u + 1 + j < items:
                    stage_qk_exp(u + 1 + j, (1 + j) % n)
            else:
                if not static or u + 2 + j < items:
                    stage_qk(u + 2 + j, (2 + j) % n)
                if not static or u + 1 + j < items:
                    stage_softmax(u + 1 + j, (1 + j) % n)
            stage_pv(u + j, j, may_finish, bounded)

    def pipeline(bounded):
        acc_ref[...] = jnp.zeros_like(acc_ref)
        if bounded:
            stage_qk_exp(0, 0)
        else:
            m_ref[...] = jnp.full(m_ref.shape, -jnp.inf, jnp.float32)
            stage_qk(0, 0)
            stage_qk(1, 1)
            stage_softmax(0, 0)

        n = ATTN_UNROLL_BOUNDED if bounded else ATTN_UNROLL

        def body(i, carry):
            run_body(pl.multiple_of(i * n, n), bounded)
            return carry

        lax.fori_loop(0, items // n - 1, body, 0)
        run_body(items - n, bounded)

    w2 = nw_ref[...] * nw_ref[...]
    bound_sq = (1.02 * (HEAD_DIM * Q_SCALE) ** 2) * jnp.max(w2[0:1]) * jnp.max(w2[1:2])
    bounded = bound_sq <= SCORE_BOUND * SCORE_BOUND
    pl.when(bounded)(lambda: pipeline(True))
    pl.when(jnp.logical_not(bounded))(lambda: pipeline(False))


def _attention(aq, ak, avt, out_norm_col, qk_norm_w, batch, seq_len, *, tq, tk):
    n = aq.shape[0]
    g = N_ATTN_HEADS // N_KV_HEADS
    nq = g * LANES
    qpb = seq_len // tq
    nkv = seq_len // tk
    items = (tq // LANES) * nkv
    assert nkv == 1 or nkv % ATTN_UNROLL_BOUNDED == 0
    assert nkv & (nkv - 1) == 0 and items % ATTN_UNROLL_BOUNDED == 0 and items >= ATTN_UNROLL_BOUNDED
    return pl.pallas_call(
        functools.partial(_attn_kernel, tq=tq, tk=tk, seq_len=seq_len),
        grid=(batch, N_KV_HEADS, qpb),
        in_specs=[
            pl.BlockSpec((tq, g * HEAD_DIM), lambda b, h, i: (b * qpb + i, h)),
            pl.BlockSpec((seq_len, HEAD_DIM), lambda b, h, i: (b, h)),
            pl.BlockSpec((HEAD_DIM, seq_len), lambda b, h, i: (h, b)),
            pl.BlockSpec((HEAD_DIM, 1), lambda b, h, i: (0, 0)),
            pl.BlockSpec((2, HEAD_DIM), lambda b, h, i: (0, 0)),
        ],
        out_specs=pl.BlockSpec((tq, g * HEAD_DIM), lambda b, h, i: (b * qpb + i, h)),
        out_shape=jax.ShapeDtypeStruct((n, ATTN_Q_W), jnp.bfloat16),
        scratch_shapes=[
            pltpu.VMEM((ATTN_UNROLL, tk, nq), jnp.float32),
            pltpu.VMEM((ATTN_UNROLL_BOUNDED, tk, nq), jnp.bfloat16),
            pltpu.VMEM((ATTN_UNROLL, 1, nq), jnp.float32),
            pltpu.VMEM((ATTN_UNROLL, 1, nq), jnp.float32),
            pltpu.VMEM((1, nq), jnp.float32),
            pltpu.VMEM((HEAD_DIM + SUM_ROWS, nq), jnp.float32),
        ],
        compiler_params=pltpu.CompilerParams(
            dimension_semantics=("arbitrary", "arbitrary", "arbitrary"),
            vmem_limit_bytes=56 * 1024 * 1024),
        name="attention",
    )(aq, ak, avt, out_norm_col, qk_norm_w)


def _dn_kernel(qf_ref, kf_ref, vf_ref, tf_ref, rf_ref, qb_ref, kb_ref, vb_ref, tb_ref, rb_ref,
               of_ref, ob_ref,
               s_ref, wq_ref, u_ref, at_ref, kdt_ref, *, nb, nchunk, prep_chunks):
    seg = pl.program_id(1)

    @pl.when(seg == 0)
    def _():
        s_ref[...] = jnp.zeros_like(s_ref)

    ri = lax.broadcasted_iota(jnp.int32, (CHUNK, CHUNK), 0)
    ci = lax.broadcasted_iota(jnp.int32, (CHUNK, CHUNK), 1)
    rw = lax.broadcasted_iota(jnp.int32, (CHUNK, 2 * CHUNK), 0)
    cw = lax.broadcasted_iota(jnp.int32, (CHUNK, 2 * CHUNK), 1)
    hi_half = cw >= CHUNK
    eye_hi = jnp.where(cw == rw + CHUNK, 1.0, 0.0)
    rk = lax.broadcasted_iota(jnp.int32, (DN_DIM, DN_DIM), 0)
    ck = lax.broadcasted_iota(jnp.int32, (DN_DIM, DN_DIM), 1)
    eye_k = jnp.where(rk == ck, 1.0, 0.0).astype(jnp.bfloat16)
    incl = (ri >= ci, ri <= ci)
    strict = (ri > ci, ri < ci)
    ins = ((qf_ref, kf_ref, vf_ref, tf_ref, rf_ref), (qb_ref, kb_ref, vb_ref, tb_ref, rb_ref))
    outs = (of_ref, ob_ref)
    bf = jnp.bfloat16
    probs = [(bb, d, hh) for bb in range(nb) for d in range(2) for hh in range(N_DN_HEADS)]

    def mm(a, b):
        return jnp.dot(a.astype(bf), b.astype(bf), preferred_element_type=jnp.float32)

    def prep(ci_, carry):
        units = []
        for pc in range(prep_chunks):
            c = ci_ * prep_chunks + pc
            r0 = pl.multiple_of(c * CHUNK, CHUNK)
            for bb, d, hh in probs:
                q_ref, k_ref, v_ref, t_ref, g_ref = ins[d]
                col = d * N_DN_HEADS + hh
                lo = hh * DN_DIM
                tab = t_ref[bb, pl.ds(r0, CHUNK), :]
                k = k_ref[bb, pl.ds(r0, CHUNK), lo:lo + DN_DIM]
                q = q_ref[bb, pl.ds(r0, CHUNK), lo:lo + DN_DIM]
                rtab = g_ref[bb, c]
                units.append(dict(
                    d=d, unit=(bb * 2 * N_DN_HEADS + col) * nchunk + c, q=q, k=k,
                    v=v_ref[bb, pl.ds(r0, CHUNK), lo:lo + DN_DIM],
                    gc=tab[:, GT_G + col:GT_G + col + 1],
                    beta=tab[:, GT_BETA + col:GT_BETA + col + 1],
                    eg=tab[:, GT_EG + col:GT_EG + col + 1],
                    gr=rtab[GT_G + col:GT_G + col + 1, 0:CHUNK],
                    beta_r=rtab[GT_BETA + col:GT_BETA + col + 1, :],
                    beg_r=rtab[GT_BEG + col:GT_BEG + col + 1, :],
                    egl_r=rtab[GT_EGL + col:GT_EGL + col + 1, 0:CHUNK]))
        for u in units:
            kq = jnp.concatenate([u["k"], u["q"]], axis=0)
            u["kkqk"] = lax.dot_general(kq, u["k"], (((1,), (1,)), ((), ())),
                                        preferred_element_type=jnp.float32)
            u["kt"] = lax.dot_general(eye_k, u["k"], (((1,), (1,)), ((), ())),
                                      preferred_element_type=jnp.float32)
        for u in units:
            d = u["d"]
            dec = jnp.exp(jnp.minimum(u["gc"] - u["gr"], 0.0))
            x = jnp.where(strict[d], -(u["beta"] * u["kkqk"][0:CHUNK]) * dec, 0.0)
            attn = jnp.where(incl[d], u["kkqk"][CHUNK:2 * CHUNK] * dec, 0.0)
            at_ref[u["unit"]] = attn.astype(bf)
            wq_ref[u["unit"], CHUNK:2 * CHUNK, :] = (u["q"].astype(jnp.float32) * u["eg"]).astype(bf)
            kdt_ref[u["unit"]] = (u["kt"] * u["egl_r"]).astype(bf)
            u["z"] = jnp.concatenate([x, jnp.zeros_like(x)], axis=1) + eye_hi
        for _ in range(6):
            for u in units:
                z = u["z"]
                u["z"] = mm(z[:, 0:CHUNK], z) + jnp.where(hi_half, z, 0.0)
        zeros = jnp.zeros((CHUNK, DN_DIM), bf)
        for u in units:
            z = u["z"]
            u_ref[u["unit"]] = jnp.dot((z * u["beta_r"]).astype(bf),
                                       jnp.concatenate([zeros, u["v"]], axis=0),
                                       preferred_element_type=jnp.float32)
            wq_ref[u["unit"], 0:CHUNK, :] = jnp.dot((z * u["beg_r"]).astype(bf),
                                                    jnp.concatenate([zeros, u["k"]], axis=0),
                                                    preferred_element_type=jnp.float32).astype(bf)
        return carry

    lax.fori_loop(0, nchunk // prep_chunks, prep, 0)

    def scan(step, carry):
        chains = []
        for bb, d, hh in probs:
            c = step if d == 0 else nchunk - 1 - step
            r0 = pl.multiple_of(c * CHUNK, CHUNK)
            col = d * N_DN_HEADS + hh
            sidx = bb * 2 * N_DN_HEADS + col
            tabrow = ins[d][3][bb, pl.ds(r0, 1), :]
            chains.append(dict(bb=bb, d=d, hh=hh, sidx=sidx, unit=sidx * nchunk + c, r0=r0,
                               gl=tabrow[:, GT_GL + col:GT_GL + col + 1]))
        for ch in chains:
            ch["st"] = s_ref[ch["sidx"]]
            ch["res"] = jnp.dot(wq_ref[ch["unit"]], ch["st"].astype(bf),
                                preferred_element_type=jnp.float32)
        for ch in chains:
            ch["vn"] = (u_ref[ch["unit"]] - ch["res"][0:CHUNK]).astype(bf)
        for ch in chains:
            ch["o"] = jnp.dot(at_ref[ch["unit"]], ch["vn"], preferred_element_type=jnp.float32)
            ch["upd"] = jnp.dot(kdt_ref[ch["unit"]], ch["vn"], preferred_element_type=jnp.float32)
        for ch in chains:
            hh = ch["hh"]
            o = ch["res"][CHUNK:2 * CHUNK] + ch["o"]
            outs[ch["d"]][ch["bb"], pl.ds(ch["r0"], CHUNK), hh * DN_DIM:(hh + 1) * DN_DIM] = o.astype(bf)
            s_ref[ch["sidx"]] = ch["st"] * ch["gl"] + ch["upd"]
        return carry

    lax.fori_loop(0, nchunk, scan, 0)


def _deltanet(dq, dk, dv, gcol, grow, batch, seq_len, *, seg, prep_chunks, nb):
    nseg = seq_len // seg
    nchunk = seg // CHUNK
    units = nb * 2 * N_DN_HEADS * nchunk
    fwd = lambda b, s: (b, s, 0)
    bwd = lambda b, s: (b, nseg - 1 - s, 0)
    fwd4 = lambda b, s: (b, s, 0, 0)
    bwd4 = lambda b, s: (b, nseg - 1 - s, 0, 0)
    big = lambda m: pl.BlockSpec((nb, seg, DN_W), m)
    tabs = lambda m: pl.BlockSpec((nb, seg, LANES), m)
    rows = lambda m: pl.BlockSpec((nb, nchunk, GT_ROWS, LANES), m)
    bf = jnp.bfloat16
    out = jax.ShapeDtypeStruct((batch, seq_len, DN_W), bf)
    return pl.pallas_call(
        functools.partial(_dn_kernel, nb=nb, nchunk=nchunk, prep_chunks=prep_chunks),
        grid=(batch // nb, nseg),
        in_specs=[big(fwd), big(fwd), big(fwd), tabs(fwd), rows(fwd4),
                  big(bwd), big(bwd), big(bwd), tabs(bwd), rows(bwd4)],
        out_specs=(big(fwd), big(bwd)),
        out_shape=(out, out),
        scratch_shapes=[
            pltpu.VMEM((nb * 2 * N_DN_HEADS, DN_DIM, DN_DIM), jnp.float32),
            pltpu.VMEM((units, 2 * CHUNK, DN_DIM), bf),
            pltpu.VMEM((units, CHUNK, DN_DIM), jnp.float32),
            pltpu.VMEM((units, CHUNK, CHUNK), bf),
            pltpu.VMEM((units, DN_DIM, CHUNK), bf),
        ],
        compiler_params=pltpu.CompilerParams(dimension_semantics=("arbitrary", "arbitrary"),
                                             vmem_limit_bytes=VMEM_LIMIT),
        name="deltanet",
    )(dq, dk, dv, gcol, grow, dq, dk, dv, gcol, grow)


def _mix_kernel(x_ref, ao_ref, of_ref, ob_ref, z_ref, dnw_ref, wo_ref, n2_ref, x1_ref, h2_ref):
    dnw = dnw_ref[...]
    parts = []
    for hh in range(N_DN_HEADS):
        sl = slice(hh * DN_DIM, (hh + 1) * DN_DIM)
        o = of_ref[:, sl].astype(jnp.float32) + ob_ref[:, sl].astype(jnp.float32)
        z = z_ref[:, sl].astype(jnp.float32)
        parts.append((_rms_rows(o, dnw) * _silu(z)).astype(jnp.bfloat16))
    dn = jnp.concatenate(parts, axis=1)
    y = jnp.dot(ao_ref[...], wo_ref[0:ATTN_Q_W, :], preferred_element_type=jnp.float32)
    y = y + jnp.dot(dn, wo_ref[ATTN_Q_W:ATTN_Q_W + DN_W, :], preferred_element_type=jnp.float32)
    x1 = x_ref[...] + y
    x1_ref[...] = x1
    h2_ref[...] = _rms_rows(x1, n2_ref[...]).astype(jnp.bfloat16)


def _mix_out(x2, ao, o_f, o_b, dz, dn_norm, w_out, n2, *, tm):
    n = x2.shape[0]
    tok = lambda i: (i, 0)
    const = lambda i: (0, 0)
    half = pl.BlockSpec((tm, DN_W), tok)
    return pl.pallas_call(
        _mix_kernel,
        grid=(n // tm,),
        in_specs=[pl.BlockSpec((tm, D_MODEL), tok), half, half, half, half,
                  pl.BlockSpec((1, DN_DIM), const),
                  pl.BlockSpec((ATTN_Q_W + DN_W, D_MODEL), const),
                  pl.BlockSpec((1, D_MODEL), const)],
        out_specs=(pl.BlockSpec((tm, D_MODEL), tok), pl.BlockSpec((tm, D_MODEL), tok)),
        out_shape=(jax.ShapeDtypeStruct((n, D_MODEL), jnp.float32),
                   jax.ShapeDtypeStruct((n, D_MODEL), jnp.bfloat16)),
        compiler_params=pltpu.CompilerParams(dimension_semantics=("arbitrary",),
                                             vmem_limit_bytes=VMEM_LIMIT),
        name="mix_out",
    )(x2, ao, o_f, o_b, dz, dn_norm, w_out, n2)


def _ffn_kernel(hp_ref, h_ref, hn_ref, x1_ref, wg_ref, wu_ref, cwg_ref, cwu_ref, bg_ref, bu_ref,
                wd_ref, y_ref, acc_ref, he_ref, ug_ref, uu_ref, act_ref, *, tm, fc, tiles_per_seq):
    i = pl.program_id(0)
    first = (i % tiles_per_seq) == 0
    last = (i % tiles_per_seq) == tiles_per_seq - 1
    rows = tm + 2 * HALO
    he_ref[0:HALO, :] = jnp.where(first, jnp.zeros_like(hp_ref), hp_ref[...])
    he_ref[HALO:HALO + tm, :] = h_ref[...]
    he_ref[HALO + tm:rows, :] = jnp.where(last, jnp.zeros_like(hn_ref), hn_ref[...])

    def cols(c):
        return c * fc if isinstance(c, int) else pl.multiple_of(c * fc, fc)

    def conv(u_ref, slot, a, b, cw_ref, b_ref, sl):
        w = u_ref[slot, a + HALO - SUBLANES:b + HALO + SUBLANES, :]
        n = b - a
        prev = pltpu.roll(w, 1, axis=0)[SUBLANES:SUBLANES + n]
        nxt = pltpu.roll(w, n + 2 * SUBLANES - 1, axis=0)[SUBLANES:SUBLANES + n]
        mid = w[SUBLANES:SUBLANES + n]
        return prev * cw_ref[0:1, sl] + mid * cw_ref[1:2, sl] + nxt * cw_ref[2:3, sl] + b_ref[0:1, sl]

    def up(c, slot):
        sl = pl.ds(cols(c), fc)
        for a, b in _row_blocks(rows, FFN_ROW_BLOCKS):
            ug_ref[slot, a:b, :] = jnp.dot(he_ref[a:b, :], wg_ref[:, sl], preferred_element_type=jnp.float32)
            uu_ref[slot, a:b, :] = jnp.dot(he_ref[a:b, :], wu_ref[:, sl], preferred_element_type=jnp.float32)

    def gate(c, slot):
        sl = pl.ds(cols(c), fc)
        for a, b in _row_blocks(tm, FFN_ROW_BLOCKS):
            act_ref[slot, a:b, :] = (_silu(conv(ug_ref, slot, a, b, cwg_ref, bg_ref, sl))
                                     * conv(uu_ref, slot, a, b, cwu_ref, bu_ref, sl)).astype(jnp.bfloat16)

    def down(c, slot):
        acc_ref[...] += jnp.dot(act_ref[slot], wd_ref[pl.ds(cols(c), fc), :],
                                preferred_element_type=jnp.float32)

    nchunks = D_FF_PAD // fc
    acc_ref[...] = jnp.zeros_like(acc_ref)
    up(0, 0)
    for c in range(nchunks):
        if c + 1 < nchunks:
            up(c + 1, (c + 1) % 2)
        gate(c, c % 2)
        down(c, c % 2)
    y_ref[...] = x1_ref[...] + acc_ref[...]


def _ffn(h2, x1, seq_len, wg, wu, cwg, cwu, bg, bu, wd, *, tm, fc):
    n = h2.shape[0]
    tiles_per_seq = seq_len // tm
    nblk = n // HALO
    tpb = tm // HALO
    tok = lambda i: (i, 0)
    const = lambda i: (0, 0)
    return pl.pallas_call(
        functools.partial(_ffn_kernel, tm=tm, fc=fc, tiles_per_seq=tiles_per_seq),
        grid=(n // tm,),
        in_specs=[
            pl.BlockSpec((HALO, D_MODEL), lambda i: (jnp.maximum(i * tpb - 1, 0), 0)),
            pl.BlockSpec((tm, D_MODEL), tok),
            pl.BlockSpec((HALO, D_MODEL), lambda i: (jnp.minimum((i + 1) * tpb, nblk - 1), 0)),
            pl.BlockSpec((tm, D_MODEL), tok),
            pl.BlockSpec((D_MODEL, D_FF_PAD), const, pipeline_mode=pl.Buffered(1)),
            pl.BlockSpec((D_MODEL, D_FF_PAD), const, pipeline_mode=pl.Buffered(1)),
            pl.BlockSpec((3, D_FF_PAD), const),
            pl.BlockSpec((3, D_FF_PAD), const),
            pl.BlockSpec((1, D_FF_PAD), const),
            pl.BlockSpec((1, D_FF_PAD), const),
            pl.BlockSpec((D_FF_PAD, D_MODEL), const, pipeline_mode=pl.Buffered(1)),
        ],
        out_specs=pl.BlockSpec((tm, D_MODEL), tok),
        out_shape=jax.ShapeDtypeStruct((n, D_MODEL), jnp.float32),
        scratch_shapes=[pltpu.VMEM((tm, D_MODEL), jnp.float32),
                        pltpu.VMEM((tm + 2 * HALO, D_MODEL), jnp.bfloat16),
                        pltpu.VMEM((2, tm + 2 * HALO, fc), jnp.float32),
                        pltpu.VMEM((2, tm + 2 * HALO, fc), jnp.float32),
                        pltpu.VMEM((2, tm, fc), jnp.bfloat16)],
        compiler_params=pltpu.CompilerParams(dimension_semantics=("arbitrary",),
                                             vmem_limit_bytes=56 * 1024 * 1024),
        name="ffn",
    )(h2, h2, h2, x1, wg, wu, cwg, cwu, bg, bu, wd)


def _rope_tables(seq_len):
    half = HEAD_DIM // 2
    f32 = np.float32
    inv = f32(ROPE_THETA) ** (-np.arange(0, half, 2, dtype=f32) / f32(half))
    t = np.arange(seq_len, dtype=np.int32)
    rows = (t // GRID_W).astype(f32)[:, None] * inv[None, :]
    cols = (t % GRID_W).astype(f32)[:, None] * inv[None, :]
    cos = np.concatenate([np.cos(rows), np.cos(rows), np.cos(cols), np.cos(cols)], axis=1)
    sin = np.concatenate([-np.sin(rows), np.sin(rows), -np.sin(cols), np.sin(cols)], axis=1)
    return jnp.asarray(cos, jnp.float32), jnp.asarray(sin, jnp.float32)


def _pad_cols(a, width):
    return jnp.pad(a, ((0, 0), (0, width - a.shape[1])))


def _prepare(norm1_w, w_in, dn_conv_w, dn_A_log_f, dn_A_log_b, dn_dt_bias_f, dn_dt_bias_b, dn_norm_w,
             attn_q_norm_w, attn_k_norm_w, attn_out_norm_w, w_out, norm2_w, w_ffn_in, ffn_conv_w,
             ffn_conv_b, w_ffn_out):
    bf = jnp.bfloat16
    gate_p = jnp.zeros((SUBLANES, LANES), jnp.float32)
    gate_p = gate_p.at[0, 0:4].set(dn_A_log_f).at[0, 4:8].set(dn_A_log_b)
    gate_p = gate_p.at[1, 0:4].set(dn_dt_bias_f).at[1, 4:8].set(dn_dt_bias_b)
    return dict(
        n1=norm1_w.reshape(1, D_MODEL),
        w_in=_pad_cols(w_in, IN_W_PAD).astype(bf),
        conv_w=dn_conv_w,
        gate_p=gate_p,
        dn_norm=dn_norm_w.reshape(1, DN_DIM),
        qn=attn_q_norm_w.reshape(1, HEAD_DIM),
        kn=attn_k_norm_w.reshape(1, HEAD_DIM),
        on=attn_out_norm_w.reshape(HEAD_DIM, 1),
        w_out=w_out.astype(bf),
        n2=norm2_w.reshape(1, D_MODEL),
        wg=_pad_cols(w_ffn_in[:, :D_FF], D_FF_PAD).astype(bf),
        wu=_pad_cols(w_ffn_in[:, D_FF:], D_FF_PAD).astype(bf),
        cwg=_pad_cols(ffn_conv_w[:, :D_FF], D_FF_PAD),
        cwu=_pad_cols(ffn_conv_w[:, D_FF:], D_FF_PAD),
        bg=_pad_cols(ffn_conv_b[None, :D_FF], D_FF_PAD),
        bu=_pad_cols(ffn_conv_b[None, D_FF:], D_FF_PAD),
        wd=jnp.pad(w_ffn_out, ((0, D_FF_PAD - D_FF), (0, 0))).astype(bf),
    )


def _tiles(seq_len):
    return dict(
        tm_in=min(512, seq_len),
        tq=min(4096, seq_len),
        tk=min(2048, seq_len),
        seg=min(256, seq_len),
        prep_batch=4,
        dn_batch=4,
        tm_mix=min(1024, seq_len),
        tm_ffn=min(1024, seq_len),
        fc=256,
    )


def _layer(x, p):
    batch, seq_len, _ = x.shape
    t = _tiles(seq_len)
    x2 = x.reshape(batch * seq_len, D_MODEL)
    cos, sin = _rope_tables(seq_len)
    aq, ak, avt, dq, dk, dv, dz, gcol, grow = _in_proj(
        x2, seq_len, p["n1"], p["w_in"], cos, sin, p["qn"], p["kn"], p["conv_w"], p["gate_p"],
        tm=t["tm_in"])
    ao = _attention(aq, ak, avt, p["on"], jnp.concatenate([p["qn"], p["kn"]], axis=0),
                    batch, seq_len, tq=t["tq"], tk=t["tk"])
    n = batch * seq_len
    nb = min(t["dn_batch"], batch)
    o_f, o_b = _deltanet(dq.reshape(batch, seq_len, DN_W), dk.reshape(batch, seq_len, DN_W),
                         dv.reshape(batch, seq_len, DN_W), gcol.reshape(batch, seq_len, LANES),
                         grow.reshape(batch, seq_len // CHUNK, GT_ROWS, LANES),
                         batch, seq_len, seg=t["seg"], prep_chunks=max(1, t["prep_batch"] // nb), nb=nb)
    o_f = o_f.reshape(n, DN_W)
    o_b = o_b.reshape(n, DN_W)
    x1, h2 = _mix_out(x2, ao, o_f, o_b, dz, p["dn_norm"], p["w_out"], p["n2"], tm=t["tm_mix"])
    y = _ffn(h2, x1, seq_len, p["wg"], p["wu"], p["cwg"], p["cwu"], p["bg"], p["bu"], p["wd"],
             tm=t["tm_ffn"], fc=t["fc"])
    return y.reshape(batch, seq_len, D_MODEL)


def kernel(x_prompt, x_sample, norm1_w, w_in, dn_conv_w, dn_A_log_f, dn_A_log_b, dn_dt_bias_f, dn_dt_bias_b, dn_norm_w, attn_q_norm_w, attn_k_norm_w, attn_out_norm_w, w_out, norm2_w, w_ffn_in, ffn_conv_w, ffn_conv_b, w_ffn_out):
    depth = norm1_w.shape[0]

    def trunk(x):
        for l in range(depth):
            p = _prepare(norm1_w[l], w_in[l], dn_conv_w[l], dn_A_log_f[l], dn_A_log_b[l],
                         dn_dt_bias_f[l], dn_dt_bias_b[l], dn_norm_w[l], attn_q_norm_w[l],
                         attn_k_norm_w[l], attn_out_norm_w[l], w_out[l], norm2_w[l], w_ffn_in[l],
                         ffn_conv_w[l], ffn_conv_b[l], w_ffn_out[l])
            x = _layer(x, p)
        return x

    return (trunk(x_prompt), trunk(x_sample))
```

```python
import functools
import math

import jax
import jax.numpy as jnp
import numpy as np
from jax import lax
from jax.experimental import pallas as pl
from jax.experimental.pallas import tpu as pltpu

D_MODEL = 1024
GRID_W = 64
HEAD_DIM = 128
N_ATTN_HEADS = 4
N_KV_HEADS = 2
N_DN_HEADS = 4
DN_DIM = 128
DN_CONV_K = 5
CHUNK = 64
ROPE_THETA = 10000.0
D_FF = 2752
EPS = 1e-6

ATTN_Q_W = N_ATTN_HEADS * HEAD_DIM
ATTN_KV_W = N_KV_HEADS * HEAD_DIM
DN_W = N_DN_HEADS * DN_DIM
IN_W = ATTN_Q_W + 2 * ATTN_KV_W + 4 * DN_W + 4 * N_DN_HEADS

LANES = 128
SUBLANES = 8
IN_W_PAD = 3200
D_FF_PAD = 2816
GATE_OFF = ATTN_Q_W + 2 * ATTN_KV_W + 4 * DN_W
HALO = 2 * SUBLANES

VMEM_LIMIT = 48 * 1024 * 1024
VMEM_LIMIT_LARGE = 56 * 1024 * 1024

GT_G, GT_BETA, GT_EG, GT_EGL, GT_GL, GT_BEG = 0, 8, 16, 24, 32, 40
GT_ROWS = 48

SUM_ROWS = 2 * SUBLANES
ATTN_UNROLL = 4
ATTN_UNROLL_BOUNDED = 8
IN_ROW_BLOCKS = 2
FFN_ROW_BLOCKS = 4
Q_SCALE = (HEAD_DIM ** -0.5) * math.log2(math.e)
SCORE_BOUND = 60.0


def _sigmoid(x):
    return 0.5 * jnp.tanh(0.5 * x) + 0.5


def _silu(x):
    return x * _sigmoid(x)


def _softplus(x):
    return jnp.maximum(x, 0.0) + jnp.log(1.0 + jnp.exp(-jnp.abs(x)))


def _rms_rows(x, w):
    return x * lax.rsqrt(jnp.mean(x * x, axis=-1, keepdims=True) + EPS) * w


def _row_blocks(nrows, nblocks):
    tile = 2 * SUBLANES
    cuts = [-(-(nrows * b // nblocks) // tile) * tile for b in range(nblocks)] + [nrows]
    return list(zip(cuts[:-1], cuts[1:]))


def _aligned(x, m):
    return x if isinstance(x, int) else pl.multiple_of(x, m)


def _in_proj_kernel(xp_ref, x_ref, xn_ref, n1_ref, w_ref, cos_ref, sin_ref, qn_ref, kn_ref,
                    cw_ref, gp_ref,
                    aq_ref, ak_ref, avt_ref, dq_ref, dk_ref, dv_ref, dz_ref, gcol_ref, grow_ref,
                    ext_ref, he_ref, *, tm, tiles_per_seq):
    i = pl.program_id(0)
    first = (i % tiles_per_seq) == 0
    last = (i % tiles_per_seq) == tiles_per_seq - 1

    n1 = n1_ref[...]
    xp = jnp.where(first, jnp.zeros_like(xp_ref), xp_ref[...])
    xn = jnp.where(last, jnp.zeros_like(xn_ref), xn_ref[...])
    he_ref[0:HALO, :] = _rms_rows(xp, n1).astype(jnp.bfloat16)
    he_ref[HALO:HALO + tm, :] = _rms_rows(x_ref[...], n1).astype(jnp.bfloat16)
    he_ref[HALO + tm:tm + 2 * HALO, :] = _rms_rows(xn, n1).astype(jnp.bfloat16)
    h_rows = slice(HALO, HALO + tm)

    cos = cos_ref[...]
    sin = sin_ref[...]
    lane = lax.broadcasted_iota(jnp.int32, (tm, LANES), 1)
    low_half = (lane % 64) < 32

    def rope(xh):
        perm = jnp.where(low_half, pltpu.roll(xh, 96, axis=1), pltpu.roll(xh, 32, axis=1))
        return xh * cos + perm * sin

    def proj(lo, width, r0, nrows):
        return jnp.concatenate(
            [jnp.dot(he_ref[r0 + a:r0 + b, :], w_ref[:, lo:lo + width],
                     preferred_element_type=jnp.float32)
             for a, b in _row_blocks(nrows, IN_ROW_BLOCKS)], axis=0)

    pa = proj(0, ATTN_Q_W + 2 * ATTN_KV_W, HALO, tm)
    dn_off = ATTN_Q_W + 2 * ATTN_KV_W
    for part in range(3):
        ext_ref[:, part * DN_W:(part + 1) * DN_W] = proj(dn_off + part * DN_W, DN_W, 0, tm + 2 * HALO)
    pz = proj(dn_off + 3 * DN_W, DN_W, HALO, tm)
    pg = proj(GATE_OFF, LANES, HALO, tm)

    qn = qn_ref[...]
    kn = kn_ref[...]
    for hh in range(N_ATTN_HEADS):
        xh = pa[:, hh * HEAD_DIM:(hh + 1) * HEAD_DIM]
        aq_ref[:, hh * HEAD_DIM:(hh + 1) * HEAD_DIM] = (rope(_rms_rows(xh, qn)) * Q_SCALE).astype(jnp.bfloat16)
    for hh in range(N_KV_HEADS):
        o = ATTN_Q_W + hh * HEAD_DIM
        xh = pa[:, o:o + HEAD_DIM]
        ak_ref[:, hh * HEAD_DIM:(hh + 1) * HEAD_DIM] = rope(_rms_rows(xh, kn)).astype(jnp.bfloat16)
    av = pa[:, ATTN_Q_W + ATTN_KV_W:ATTN_Q_W + 2 * ATTN_KV_W]
    avt_ref[...] = av.T.astype(jnp.bfloat16)

    pad = DN_CONV_K // 2
    out_refs = (dq_ref, dk_ref, dv_ref)
    for part in range(3):
        for hh in range(N_DN_HEADS):
            c0 = part * DN_W + hh * DN_DIM
            win = ext_ref[HALO - SUBLANES:HALO + tm + SUBLANES, c0:c0 + DN_DIM]
            nwin = tm + 2 * SUBLANES
            acc = None
            for j in range(DN_CONV_K):
                tap = win if j == pad else pltpu.roll(win, (pad - j) % nwin, axis=0)
                term = tap[SUBLANES:SUBLANES + tm] * cw_ref[j:j + 1, c0:c0 + DN_DIM]
                acc = term if acc is None else acc + term
            y = _silu(acc)
            if part < 2:
                y = y * lax.rsqrt(jnp.sum(y * y, axis=-1, keepdims=True) + EPS)
            if part == 0:
                y = y * (DN_DIM ** -0.5)
            out_refs[part][:, hh * DN_DIM:(hh + 1) * DN_DIM] = y.astype(jnp.bfloat16)

    dz_ref[...] = pz.astype(jnp.bfloat16)

    a_log = gp_ref[0:1, :]
    dt_bias = gp_ref[1:2, :]
    g = -jnp.exp(a_log) * _softplus(pg + dt_bias)
    beta = _sigmoid(pg)
    rin = lax.broadcasted_iota(jnp.int32, (tm, LANES), 0) % CHUNK
    pre = g
    suf = g
    s = 1
    while s < CHUNK:
        pre = pre + jnp.where(rin >= s, pltpu.roll(pre, s, axis=0), 0.0)
        suf = suf + jnp.where(rin < CHUNK - s, pltpu.roll(suf, tm - s, axis=0), 0.0)
        s *= 2
    tot = pre + suf - g
    gcum = jnp.where(lane < N_DN_HEADS, pre, suf)
    tab = jnp.where(lane < GT_BETA, gcum, 0.0)
    tab = jnp.where(jnp.logical_and(lane >= GT_BETA, lane < GT_EG), beta, tab)
    tab = jnp.where(jnp.logical_and(lane >= GT_EG, lane < GT_EGL),
                    pltpu.roll(jnp.exp(gcum), GT_EG, axis=1), tab)
    tab = jnp.where(jnp.logical_and(lane >= GT_EGL, lane < GT_GL),
                    pltpu.roll(jnp.exp(tot - gcum), GT_EGL, axis=1), tab)
    tab = jnp.where(jnp.logical_and(lane >= GT_GL, lane < GT_GL + 8),
                    pltpu.roll(jnp.exp(tot), GT_GL, axis=1), tab)
    beg = pltpu.roll(beta, LANES - GT_BETA, axis=1) * jnp.exp(gcum)
    tab = jnp.where(jnp.logical_and(lane >= GT_BEG, lane < GT_BEG + 8),
                    pltpu.roll(beg, GT_BEG, axis=1), tab)
    gcol_ref[...] = tab
    gt = tab.T
    for c in range(tm // CHUNK):
        blk = gt[0:GT_ROWS, c * CHUNK:(c + 1) * CHUNK]
        grow_ref[c] = jnp.concatenate([blk, blk], axis=1)


def _in_proj(x2, seq_len, n1, w_in, cos, sin, qn, kn, conv_w, gate_p, *, tm):
    n = x2.shape[0]
    tiles_per_seq = seq_len // tm
    nblk8 = n // HALO
    tpb = tm // HALO
    const = lambda i: (0, 0)
    tok = lambda i: (i, 0)
    bf = jnp.bfloat16
    out_shape = (
        jax.ShapeDtypeStruct((n, ATTN_Q_W), bf),
        jax.ShapeDtypeStruct((n, ATTN_KV_W), bf),
        jax.ShapeDtypeStruct((ATTN_KV_W, n), bf),
        jax.ShapeDtypeStruct((n, DN_W), bf),
        jax.ShapeDtypeStruct((n, DN_W), bf),
        jax.ShapeDtypeStruct((n, DN_W), bf),
        jax.ShapeDtypeStruct((n, DN_W), bf),
        jax.ShapeDtypeStruct((n, LANES), jnp.float32),
        jax.ShapeDtypeStruct((n // CHUNK, GT_ROWS, LANES), jnp.float32),
    )
    out_specs = (
        pl.BlockSpec((tm, ATTN_Q_W), tok),
        pl.BlockSpec((tm, ATTN_KV_W), tok),
        pl.BlockSpec((ATTN_KV_W, tm), lambda i: (0, i)),
        pl.BlockSpec((tm, DN_W), tok),
        pl.BlockSpec((tm, DN_W), tok),
        pl.BlockSpec((tm, DN_W), tok),
        pl.BlockSpec((tm, DN_W), tok),
        pl.BlockSpec((tm, LANES), tok),
        pl.BlockSpec((tm // CHUNK, GT_ROWS, LANES), lambda i: (i, 0, 0)),
    )
    in_specs = [
        pl.BlockSpec((HALO, D_MODEL), lambda i: (jnp.maximum(i * tpb - 1, 0), 0)),
        pl.BlockSpec((tm, D_MODEL), tok),
        pl.BlockSpec((HALO, D_MODEL), lambda i: (jnp.minimum((i + 1) * tpb, nblk8 - 1), 0)),
        pl.BlockSpec((1, D_MODEL), const),
        pl.BlockSpec((D_MODEL, IN_W_PAD), const),
        pl.BlockSpec((tm, LANES), lambda i: (i % tiles_per_seq, 0)),
        pl.BlockSpec((tm, LANES), lambda i: (i % tiles_per_seq, 0)),
        pl.BlockSpec((1, HEAD_DIM), const),
        pl.BlockSpec((1, HEAD_DIM), const),
        pl.BlockSpec((DN_CONV_K, 3 * DN_W), const),
        pl.BlockSpec((SUBLANES, LANES), const),
    ]
    return pl.pallas_call(
        functools.partial(_in_proj_kernel, tm=tm, tiles_per_seq=tiles_per_seq),
        grid=(n // tm,),
        in_specs=in_specs,
        out_specs=out_specs,
        out_shape=out_shape,
        scratch_shapes=[pltpu.VMEM((tm + 2 * HALO, 3 * DN_W), jnp.float32),
                        pltpu.VMEM((tm + 2 * HALO, D_MODEL), jnp.bfloat16)],
        compiler_params=pltpu.CompilerParams(dimension_semantics=("arbitrary",),
                                             vmem_limit_bytes=VMEM_LIMIT),
        name="in_proj",
    )(x2, x2, x2, n1, w_in, cos, sin, qn, kn, conv_w, gate_p)


def _attn_kernel(q_ref, k_ref, vt_ref, on_ref, nw_ref, o_ref, s_ref, p_ref, al_ref, cm_ref, m_ref, acc_ref,
                 *, tq, tk, seq_len):
    g = N_ATTN_HEADS // N_KV_HEADS
    nqt = tq // LANES
    nkv = seq_len // tk
    items = nqt * nkv
    kv_shift = nkv.bit_length() - 1
    ones = jnp.ones((SUM_ROWS, tk), jnp.bfloat16)

    def split(t):
        if isinstance(t, int):
            return t // nkv, t % nkv
        return lax.shift_right_logical(t, kv_shift), jnp.bitwise_and(t, nkv - 1)

    def scores(t):
        qi, kj = split(t)
        r0 = _aligned(qi * LANES, LANES)
        k0 = _aligned(kj * tk, tk)
        qblk = q_ref[pl.ds(r0, LANES), :]
        qs = jnp.concatenate([qblk[:, j * HEAD_DIM:(j + 1) * HEAD_DIM] for j in range(g)], axis=0)
        kb = k_ref[pl.ds(k0, tk), :]
        return lax.dot_general(kb, qs, (((1,), (1,)), ((), ())),
                               preferred_element_type=jnp.float32)

    def stage_qk(t, slot):
        s = scores(t)
        s_ref[slot] = s
        cm_ref[slot] = jnp.max(s, axis=0, keepdims=True)

    def stage_qk_exp(t, slot):
        p_ref[slot] = jnp.exp2(scores(t)).astype(jnp.bfloat16)

    def stage_softmax(t, slot):
        _, kj = split(t)
        m_prev = jnp.where(kj == 0, -jnp.inf, m_ref[...])
        m_new = jnp.maximum(m_prev, cm_ref[slot])
        al_ref[slot] = jnp.exp2(m_prev - m_new)
        p_ref[slot] = jnp.exp2(s_ref[slot] - m_new).astype(jnp.bfloat16)
        m_ref[...] = m_new

    def stage_pv(t, slot, may_finish, bounded):
        qi, kj = split(t)
        k0 = _aligned(kj * tk, tk)
        lhs = jnp.concatenate([vt_ref[:, pl.ds(k0, tk)], ones], axis=0)
        if bounded:
            prev = jnp.where(kj == 0, 0.0, acc_ref[...])
        else:
            prev = acc_ref[...] * al_ref[slot]
        acc = prev + jnp.dot(lhs, p_ref[slot], preferred_element_type=jnp.float32)
        acc_ref[...] = acc

        def finalize():
            o = acc[0:HEAD_DIM] / acc[HEAD_DIM:HEAD_DIM + 1]
            o = o * lax.rsqrt(jnp.mean(o * o, axis=0, keepdims=True) + EPS) * on_ref[...]
            ot = o.T
            r0 = _aligned(qi * LANES, LANES)
            for j in range(g):
                o_ref[pl.ds(r0, LANES), j * HEAD_DIM:(j + 1) * HEAD_DIM] = (
                    ot[j * LANES:(j + 1) * LANES, :].astype(jnp.bfloat16))

        if not may_finish:
            return
        if isinstance(kj, int):
            if kj == nkv - 1:
                finalize()
        else:
            pl.when(kj == nkv - 1)(finalize)

    def run_body(u, bounded):
        static = isinstance(u, int)
        n = ATTN_UNROLL_BOUNDED if bounded else ATTN_UNROLL
        for j in range(n):
            may_finish = nkv == 1 or j == n - 1
            if bounded:
                if not static or u + 1 + j < items:
                    stage_qk_exp(u + 1 + j, (1 + j) % n)
            else:
                if not static or u + 2 + j < items:
                    stage_qk(u + 2 + j, (2 + j) % n)
                if not static or u + 1 + j < items:
                    stage_softmax(u + 1 + j, (1 + j) % n)
            stage_pv(u + j, j, may_finish, bounded)

    def pipeline(bounded):
        acc_ref[...] = jnp.zeros_like(acc_ref)
        if bounded:
            stage_qk_exp(0, 0)
        else:
            m_ref[...] = jnp.full(m_ref.shape, -jnp.inf, jnp.float32)
            stage_qk(0, 0)
            stage_qk(1, 1)
            stage_softmax(0, 0)

        n = ATTN_UNROLL_BOUNDED if bounded else ATTN_UNROLL

        def body(i, carry):
            run_body(pl.multiple_of(i * n, n), bounded)
            return carry

        lax.fori_loop(0, items // n - 1, body, 0)
        run_body(items - n, bounded)

    w2 = nw_ref[...] * nw_ref[...]
    bound_sq = (1.02 * (HEAD_DIM * Q_SCALE) ** 2) * jnp.max(w2[0:1]) * jnp.max(w2[1:2])
    bounded = bound_sq <= SCORE_BOUND * SCORE_BOUND
    pl.when(bounded)(lambda: pipeline(True))
    pl.when(jnp.logical_not(bounded))(lambda: pipeline(False))


def _attention(aq, ak, avt, out_norm_col, qk_norm_w, batch, seq_len, *, tq, tk):
    n = aq.shape[0]
    g = N_ATTN_HEADS // N_KV_HEADS
    nq = g * LANES
    qpb = seq_len // tq
    nkv = seq_len // tk
    items = (tq // LANES) * nkv
    assert nkv == 1 or nkv % ATTN_UNROLL_BOUNDED == 0
    assert nkv & (nkv - 1) == 0 and items % ATTN_UNROLL_BOUNDED == 0 and items >= ATTN_UNROLL_BOUNDED
    return pl.pallas_call(
        functools.partial(_attn_kernel, tq=tq, tk=tk, seq_len=seq_len),
        grid=(batch, N_KV_HEADS, qpb),
        in_specs=[
            pl.BlockSpec((tq, g * HEAD_DIM), lambda b, h, i: (b * qpb + i, h)),
            pl.BlockSpec((seq_len, HEAD_DIM), lambda b, h, i: (b, h)),
            pl.BlockSpec((HEAD_DIM, seq_len), lambda b, h, i: (h, b)),
            pl.BlockSpec((HEAD_DIM, 1), lambda b, h, i: (0, 0)),
            pl.BlockSpec((2, HEAD_DIM), lambda b, h, i: (0, 0)),
        ],
        out_specs=pl.BlockSpec((tq, g * HEAD_DIM), lambda b, h, i: (b * qpb + i, h)),
        out_shape=jax.ShapeDtypeStruct((n, ATTN_Q_W), jnp.bfloat16),
        scratch_shapes=[
            pltpu.VMEM((ATTN_UNROLL, tk, nq), jnp.float32),
            pltpu.VMEM((ATTN_UNROLL_BOUNDED, tk, nq), jnp.bfloat16),
            pltpu.VMEM((ATTN_UNROLL, 1, nq), jnp.float32),
            pltpu.VMEM((ATTN_UNROLL, 1, nq), jnp.float32),
            pltpu.VMEM((1, nq), jnp.float32),
            pltpu.VMEM((HEAD_DIM + SUM_ROWS, nq), jnp.float32),
        ],
        compiler_params=pltpu.CompilerParams(
            dimension_semantics=("arbitrary", "arbitrary", "arbitrary"),
            vmem_limit_bytes=VMEM_LIMIT_LARGE),
        name="attention",
    )(aq, ak, avt, out_norm_col, qk_norm_w)


def _dn_kernel(qf_ref, kf_ref, vf_ref, tf_ref, rf_ref, qb_ref, kb_ref, vb_ref, tb_ref, rb_ref,
               of_ref, ob_ref,
               s_ref, wq_ref, u_ref, at_ref, kdt_ref, *, nb, nchunk, prep_chunks):
    seg = pl.program_id(1)

    @pl.when(seg == 0)
    def _():
        s_ref[...] = jnp.zeros_like(s_ref)

    ri = lax.broadcasted_iota(jnp.int32, (CHUNK, CHUNK), 0)
    ci = lax.broadcasted_iota(jnp.int32, (CHUNK, CHUNK), 1)
    rw = lax.broadcasted_iota(jnp.int32, (CHUNK, 2 * CHUNK), 0)
    cw = lax.broadcasted_iota(jnp.int32, (CHUNK, 2 * CHUNK), 1)
    hi_half = cw >= CHUNK
    eye_hi = jnp.where(cw == rw + CHUNK, 1.0, 0.0)
    rk = lax.broadcasted_iota(jnp.int32, (DN_DIM, DN_DIM), 0)
    ck = lax.broadcasted_iota(jnp.int32, (DN_DIM, DN_DIM), 1)
    eye_k = jnp.where(rk == ck, 1.0, 0.0).astype(jnp.bfloat16)
    incl = (ri >= ci, ri <= ci)
    strict = (ri > ci, ri < ci)
    ins = ((qf_ref, kf_ref, vf_ref, tf_ref, rf_ref), (qb_ref, kb_ref, vb_ref, tb_ref, rb_ref))
    outs = (of_ref, ob_ref)
    bf = jnp.bfloat16
    probs = [(bb, d, hh) for bb in range(nb) for d in range(2) for hh in range(N_DN_HEADS)]

    def mm(a, b):
        return jnp.dot(a.astype(bf), b.astype(bf), preferred_element_type=jnp.float32)

    def prep(ci_, carry):
        units = []
        for pc in range(prep_chunks):
            c = ci_ * prep_chunks + pc
            r0 = pl.multiple_of(c * CHUNK, CHUNK)
            for bb, d, hh in probs:
                q_ref, k_ref, v_ref, t_ref, g_ref = ins[d]
                col = d * N_DN_HEADS + hh
                lo = hh * DN_DIM
                tab = t_ref[bb, pl.ds(r0, CHUNK), :]
                k = k_ref[bb, pl.ds(r0, CHUNK), lo:lo + DN_DIM]
                q = q_ref[bb, pl.ds(r0, CHUNK), lo:lo + DN_DIM]
                rtab = g_ref[bb, c]
                units.append(dict(
                    d=d, unit=(bb * 2 * N_DN_HEADS + col) * nchunk + c, q=q, k=k,
                    v=v_ref[bb, pl.ds(r0, CHUNK), lo:lo + DN_DIM],
                    gc=tab[:, GT_G + col:GT_G + col + 1],
                    beta=tab[:, GT_BETA + col:GT_BETA + col + 1],
                    eg=tab[:, GT_EG + col:GT_EG + col + 1],
                    gr=rtab[GT_G + col:GT_G + col + 1, 0:CHUNK],
                    beta_r=rtab[GT_BETA + col:GT_BETA + col + 1, :],
                    beg_r=rtab[GT_BEG + col:GT_BEG + col + 1, :],
                    egl_r=rtab[GT_EGL + col:GT_EGL + col + 1, 0:CHUNK]))
        for u in units:
            kqi = jnp.concatenate([u["k"], u["q"], eye_k], axis=0)
            res = lax.dot_general(kqi, u["k"], (((1,), (1,)), ((), ())),
                                  preferred_element_type=jnp.float32)
            u["kkqk"] = res[0:2 * CHUNK]
            u["kt"] = res[2 * CHUNK:2 * CHUNK + DN_DIM]
        for u in units:
            d = u["d"]
            dec = jnp.exp(jnp.minimum(u["gc"] - u["gr"], 0.0))
            x = jnp.where(strict[d], -(u["beta"] * u["kkqk"][0:CHUNK]) * dec, 0.0)
            attn = jnp.where(incl[d], u["kkqk"][CHUNK:2 * CHUNK] * dec, 0.0)
            at_ref[u["unit"]] = attn.astype(bf)
            wq_ref[u["unit"], CHUNK:2 * CHUNK, :] = (u["q"].astype(jnp.float32) * u["eg"]).astype(bf)
            kdt_ref[u["unit"]] = (u["kt"] * u["egl_r"]).astype(bf)
            u["z"] = jnp.concatenate([x, jnp.zeros_like(x)], axis=1) + eye_hi
        for _ in range(6):
            for u in units:
                z = u["z"]
                u["z"] = mm(z[:, 0:CHUNK], z) + jnp.where(hi_half, z, 0.0)
        zeros = jnp.zeros((CHUNK, DN_DIM), bf)
        for u in units:
            z = u["z"]
            u_ref[u["unit"]] = jnp.dot((z * u["beta_r"]).astype(bf),
                                       jnp.concatenate([zeros, u["v"]], axis=0),
                                       preferred_element_type=jnp.float32)
            wq_ref[u["unit"], 0:CHUNK, :] = jnp.dot((z * u["beg_r"]).astype(bf),
                                                    jnp.concatenate([zeros, u["k"]], axis=0),
                                                    preferred_element_type=jnp.float32).astype(bf)
        return carry

    lax.fori_loop(0, nchunk // prep_chunks, prep, 0)

    def scan(step, carry):
        chains = []
        for bb, d, hh in probs:
            c = step if d == 0 else nchunk - 1 - step
            r0 = pl.multiple_of(c * CHUNK, CHUNK)
            col = d * N_DN_HEADS + hh
            sidx = bb * 2 * N_DN_HEADS + col
            tabrow = ins[d][3][bb, pl.ds(r0, 1), :]
            chains.append(dict(bb=bb, d=d, hh=hh, sidx=sidx, unit=sidx * nchunk + c, r0=r0,
                               gl=tabrow[:, GT_GL + col:GT_GL + col + 1]))
        for ch in chains:
            ch["st"] = s_ref[ch["sidx"]]
            ch["res"] = jnp.dot(wq_ref[ch["unit"]], ch["st"].astype(bf),
                                preferred_element_type=jnp.float32)
        for ch in chains:
            ch["vn"] = (u_ref[ch["unit"]] - ch["res"][0:CHUNK]).astype(bf)
        for ch in chains:
            ch["o"] = jnp.dot(at_ref[ch["unit"]], ch["vn"], preferred_element_type=jnp.float32)
            ch["upd"] = jnp.dot(kdt_ref[ch["unit"]], ch["vn"], preferred_element_type=jnp.float32)
        for ch in chains:
            hh = ch["hh"]
            o = ch["res"][CHUNK:2 * CHUNK] + ch["o"]
            outs[ch["d"]][ch["bb"], pl.ds(ch["r0"], CHUNK), hh * DN_DIM:(hh + 1) * DN_DIM] = o.astype(bf)
            s_ref[ch["sidx"]] = ch["st"] * ch["gl"] + ch["upd"]
        return carry

    lax.fori_loop(0, nchunk, scan, 0)


def _deltanet(dq, dk, dv, gcol, grow, batch, seq_len, *, seg, prep_chunks, nb):
    nseg = seq_len // seg
    nchunk = seg // CHUNK
    units = nb * 2 * N_DN_HEADS * nchunk
    fwd = lambda b, s: (b, s, 0)
    bwd = lambda b, s: (b, nseg - 1 - s, 0)
    fwd4 = lambda b, s: (b, s, 0, 0)
    bwd4 = lambda b, s: (b, nseg - 1 - s, 0, 0)
    big = lambda m: pl.BlockSpec((nb, seg, DN_W), m)
    tabs = lambda m: pl.BlockSpec((nb, seg, LANES), m)
    rows = lambda m: pl.BlockSpec((nb, nchunk, GT_ROWS, LANES), m)
    bf = jnp.bfloat16
    out = jax.ShapeDtypeStruct((batch, seq_len, DN_W), bf)
    return pl.pallas_call(
        functools.partial(_dn_kernel, nb=nb, nchunk=nchunk, prep_chunks=prep_chunks),
        grid=(batch // nb, nseg),
        in_specs=[big(fwd), big(fwd), big(fwd), tabs(fwd), rows(fwd4),
                  big(bwd), big(bwd), big(bwd), tabs(bwd), rows(bwd4)],
        out_specs=(big(fwd), big(bwd)),
        out_shape=(out, out),
        scratch_shapes=[
            pltpu.VMEM((nb * 2 * N_DN_HEADS, DN_DIM, DN_DIM), jnp.float32),
            pltpu.VMEM((units, 2 * CHUNK, DN_DIM), bf),
            pltpu.VMEM((units, CHUNK, DN_DIM), jnp.float32),
            pltpu.VMEM((units, CHUNK, CHUNK), bf),
            pltpu.VMEM((units, DN_DIM, CHUNK), bf),
        ],
        compiler_params=pltpu.CompilerParams(dimension_semantics=("arbitrary", "arbitrary"),
                                             vmem_limit_bytes=VMEM_LIMIT),
        name="deltanet",
    )(dq, dk, dv, gcol, grow, dq, dk, dv, gcol, grow)


def _mix_kernel(x_ref, ao_ref, of_ref, ob_ref, z_ref, dnw_ref, wo_ref, n2_ref, x1_ref, h2_ref):
    dnw = dnw_ref[...]
    parts = []
    for hh in range(N_DN_HEADS):
        sl = slice(hh * DN_DIM, (hh + 1) * DN_DIM)
        o = of_ref[:, sl].astype(jnp.float32) + ob_ref[:, sl].astype(jnp.float32)
        z = z_ref[:, sl].astype(jnp.float32)
        parts.append((_rms_rows(o, dnw) * _silu(z)).astype(jnp.bfloat16))
    dn = jnp.concatenate(parts, axis=1)
    y = jnp.dot(ao_ref[...], wo_ref[0:ATTN_Q_W, :], preferred_element_type=jnp.float32)
    y = y + jnp.dot(dn, wo_ref[ATTN_Q_W:ATTN_Q_W + DN_W, :], preferred_element_type=jnp.float32)
    x1 = x_ref[...] + y
    x1_ref[...] = x1
    h2_ref[...] = _rms_rows(x1, n2_ref[...]).astype(jnp.bfloat16)


def _mix_out(x2, ao, o_f, o_b, dz, dn_norm, w_out, n2, *, tm):
    n = x2.shape[0]
    tok = lambda i: (i, 0)
    const = lambda i: (0, 0)
    half = pl.BlockSpec((tm, DN_W), tok)
    return pl.pallas_call(
        _mix_kernel,
        grid=(n // tm,),
        in_specs=[pl.BlockSpec((tm, D_MODEL), tok), half, half, half, half,
                  pl.BlockSpec((1, DN_DIM), const),
                  pl.BlockSpec((ATTN_Q_W + DN_W, D_MODEL), const),
                  pl.BlockSpec((1, D_MODEL), const)],
        out_specs=(pl.BlockSpec((tm, D_MODEL), tok), pl.BlockSpec((tm, D_MODEL), tok)),
        out_shape=(jax.ShapeDtypeStruct((n, D_MODEL), jnp.float32),
                   jax.ShapeDtypeStruct((n, D_MODEL), jnp.bfloat16)),
        compiler_params=pltpu.CompilerParams(dimension_semantics=("arbitrary",),
                                             vmem_limit_bytes=VMEM_LIMIT),
        name="mix_out",
    )(x2, ao, o_f, o_b, dz, dn_norm, w_out, n2)


def _ffn_kernel(hp_ref, h_ref, hn_ref, x1_ref, wg_ref, wu_ref, cwg_ref, cwu_ref, bg_ref, bu_ref,
                wd_ref, y_ref, acc_ref, he_ref, ug_ref, uu_ref, act_ref, *, tm, fc, tiles_per_seq):
    i = pl.program_id(0)
    first = (i % tiles_per_seq) == 0
    last = (i % tiles_per_seq) == tiles_per_seq - 1
    rows = tm + 2 * HALO
    he_ref[0:HALO, :] = jnp.where(first, jnp.zeros_like(hp_ref), hp_ref[...])
    he_ref[HALO:HALO + tm, :] = h_ref[...]
    he_ref[HALO + tm:rows, :] = jnp.where(last, jnp.zeros_like(hn_ref), hn_ref[...])

    def cols(c):
        return c * fc if isinstance(c, int) else pl.multiple_of(c * fc, fc)

    def conv(u_ref, slot, a, b, cw_ref, b_ref, sl):
        w = u_ref[slot, a + HALO - SUBLANES:b + HALO + SUBLANES, :]
        n = b - a
        prev = pltpu.roll(w, 1, axis=0)[SUBLANES:SUBLANES + n]
        nxt = pltpu.roll(w, n + 2 * SUBLANES - 1, axis=0)[SUBLANES:SUBLANES + n]
        mid = w[SUBLANES:SUBLANES + n]
        return prev * cw_ref[0:1, sl] + mid * cw_ref[1:2, sl] + nxt * cw_ref[2:3, sl] + b_ref[0:1, sl]

    def up(c, slot):
        sl = pl.ds(cols(c), fc)
        for a, b in _row_blocks(rows, FFN_ROW_BLOCKS):
            ug_ref[slot, a:b, :] = jnp.dot(he_ref[a:b, :], wg_ref[:, sl], preferred_element_type=jnp.float32)
            uu_ref[slot, a:b, :] = jnp.dot(he_ref[a:b, :], wu_ref[:, sl], preferred_element_type=jnp.float32)

    def gate(c, slot):
        sl = pl.ds(cols(c), fc)
        for a, b in _row_blocks(tm, FFN_ROW_BLOCKS):
            act_ref[slot, a:b, :] = (_silu(conv(ug_ref, slot, a, b, cwg_ref, bg_ref, sl))
                                     * conv(uu_ref, slot, a, b, cwu_ref, bu_ref, sl)).astype(jnp.bfloat16)

    def down(c, slot):
        acc_ref[...] += jnp.dot(act_ref[slot], wd_ref[pl.ds(cols(c), fc), :],
                                preferred_element_type=jnp.float32)

    nchunks = D_FF_PAD // fc
    acc_ref[...] = jnp.zeros_like(acc_ref)
    up(0, 0)
    for c in range(nchunks):
        if c + 1 < nchunks:
            up(c + 1, (c + 1) % 2)
        gate(c, c % 2)
        down(c, c % 2)
    y_ref[...] = x1_ref[...] + acc_ref[...]


def _ffn(h2, x1, seq_len, wg, wu, cwg, cwu, bg, bu, wd, *, tm, fc):
    n = h2.shape[0]
    tiles_per_seq = seq_len // tm
    nblk = n // HALO
    tpb = tm // HALO
    tok = lambda i: (i, 0)
    const = lambda i: (0, 0)
    return pl.pallas_call(
        functools.partial(_ffn_kernel, tm=tm, fc=fc, tiles_per_seq=tiles_per_seq),
        grid=(n // tm,),
        in_specs=[
            pl.BlockSpec((HALO, D_MODEL), lambda i: (jnp.maximum(i * tpb - 1, 0), 0)),
            pl.BlockSpec((tm, D_MODEL), tok),
            pl.BlockSpec((HALO, D_MODEL), lambda i: (jnp.minimum((i + 1) * tpb, nblk - 1), 0)),
            pl.BlockSpec((tm, D_MODEL), tok),
            pl.BlockSpec((D_MODEL, D_FF_PAD), const, pipeline_mode=pl.Buffered(1)),
            pl.BlockSpec((D_MODEL, D_FF_PAD), const, pipeline_mode=pl.Buffered(1)),
            pl.BlockSpec((3, D_FF_PAD), const),
            pl.BlockSpec((3, D_FF_PAD), const),
            pl.BlockSpec((1, D_FF_PAD), const),
            pl.BlockSpec((1, D_FF_PAD), const),
            pl.BlockSpec((D_FF_PAD, D_MODEL), const, pipeline_mode=pl.Buffered(1)),
        ],
        out_specs=pl.BlockSpec((tm, D_MODEL), tok),
        out_shape=jax.ShapeDtypeStruct((n, D_MODEL), jnp.float32),
        scratch_shapes=[pltpu.VMEM((tm, D_MODEL), jnp.float32),
                        pltpu.VMEM((tm + 2 * HALO, D_MODEL), jnp.bfloat16),
                        pltpu.VMEM((2, tm + 2 * HALO, fc), jnp.float32),
                        pltpu.VMEM((2, tm + 2 * HALO, fc), jnp.float32),
                        pltpu.VMEM((2, tm, fc), jnp.bfloat16)],
        compiler_params=pltpu.CompilerParams(dimension_semantics=("arbitrary",),
                                             vmem_limit_bytes=VMEM_LIMIT_LARGE),
        name="ffn",
    )(h2, h2, h2, x1, wg, wu, cwg, cwu, bg, bu, wd)


def _rope_tables(seq_len):
    half = HEAD_DIM // 2
    f32 = np.float32
    inv = f32(ROPE_THETA) ** (-np.arange(0, half, 2, dtype=f32) / f32(half))
    t = np.arange(seq_len, dtype=np.int32)
    rows = (t // GRID_W).astype(f32)[:, None] * inv[None, :]
    cols = (t % GRID_W).astype(f32)[:, None] * inv[None, :]
    cos = np.concatenate([np.cos(rows), np.cos(rows), np.cos(cols), np.cos(cols)], axis=1)
    sin = np.concatenate([-np.sin(rows), np.sin(rows), -np.sin(cols), np.sin(cols)], axis=1)
    return jnp.asarray(cos, jnp.float32), jnp.asarray(sin, jnp.float32)


def _pad_cols(a, width):
    return jnp.pad(a, ((0, 0), (0, width - a.shape[1])))


def _prepare(norm1_w, w_in, dn_conv_w, dn_A_log_f, dn_A_log_b, dn_dt_bias_f, dn_dt_bias_b, dn_norm_w,
             attn_q_norm_w, attn_k_norm_w, attn_out_norm_w, w_out, norm2_w, w_ffn_in, ffn_conv_w,
             ffn_conv_b, w_ffn_out):
    bf = jnp.bfloat16
    gate_p = jnp.zeros((SUBLANES, LANES), jnp.float32)
    gate_p = gate_p.at[0, 0:4].set(dn_A_log_f).at[0, 4:8].set(dn_A_log_b)
    gate_p = gate_p.at[1, 0:4].set(dn_dt_bias_f).at[1, 4:8].set(dn_dt_bias_b)
    return dict(
        n1=norm1_w.reshape(1, D_MODEL),
        w_in=_pad_cols(w_in, IN_W_PAD).astype(bf),
        conv_w=dn_conv_w,
        gate_p=gate_p,
        dn_norm=dn_norm_w.reshape(1, DN_DIM),
        qn=attn_q_norm_w.reshape(1, HEAD_DIM),
        kn=attn_k_norm_w.reshape(1, HEAD_DIM),
        on=attn_out_norm_w.reshape(HEAD_DIM, 1),
        w_out=w_out.astype(bf),
        n2=norm2_w.reshape(1, D_MODEL),
        wg=_pad_cols(w_ffn_in[:, :D_FF], D_FF_PAD).astype(bf),
        wu=_pad_cols(w_ffn_in[:, D_FF:], D_FF_PAD).astype(bf),
        cwg=_pad_cols(ffn_conv_w[:, :D_FF], D_FF_PAD),
        cwu=_pad_cols(ffn_conv_w[:, D_FF:], D_FF_PAD),
        bg=_pad_cols(ffn_conv_b[None, :D_FF], D_FF_PAD),
        bu=_pad_cols(ffn_conv_b[None, D_FF:], D_FF_PAD),
        wd=jnp.pad(w_ffn_out, ((0, D_FF_PAD - D_FF), (0, 0))).astype(bf),
    )


def _tiles(seq_len):
    return dict(
        tm_in=min(512, seq_len),
        tq=min(4096, seq_len),
        tk=min(2048, seq_len),
        seg=min(256, seq_len),
        prep_batch=4,
        dn_batch=4,
        tm_mix=min(1024, seq_len),
        tm_ffn=min(1024, seq_len),
        fc=256,
    )


def _layer(x, p):
    batch, seq_len, _ = x.shape
    t = _tiles(seq_len)
    x2 = x.reshape(batch * seq_len, D_MODEL)
    cos, sin = _rope_tables(seq_len)
    aq, ak, avt, dq, dk, dv, dz, gcol, grow = _in_proj(
        x2, seq_len, p["n1"], p["w_in"], cos, sin, p["qn"], p["kn"], p["conv_w"], p["gate_p"],
        tm=t["tm_in"])
    ao = _attention(aq, ak, avt, p["on"], jnp.concatenate([p["qn"], p["kn"]], axis=0),
                    batch, seq_len, tq=t["tq"], tk=t["tk"])
    n = batch * seq_len
    nb = min(t["dn_batch"], batch)
    o_f, o_b = _deltanet(dq.reshape(batch, seq_len, DN_W), dk.reshape(batch, seq_len, DN_W),
                         dv.reshape(batch, seq_len, DN_W), gcol.reshape(batch, seq_len, LANES),
                         grow.reshape(batch, seq_len // CHUNK, GT_ROWS, LANES),
                         batch, seq_len, seg=t["seg"], prep_chunks=max(1, t["prep_batch"] // nb), nb=nb)
    o_f = o_f.reshape(n, DN_W)
    o_b = o_b.reshape(n, DN_W)
    x1, h2 = _mix_out(x2, ao, o_f, o_b, dz, p["dn_norm"], p["w_out"], p["n2"], tm=t["tm_mix"])
    y = _ffn(h2, x1, seq_len, p["wg"], p["wu"], p["cwg"], p["cwu"], p["bg"], p["bu"], p["wd"],
             tm=t["tm_ffn"], fc=t["fc"])
    return y.reshape(batch, seq_len, D_MODEL)


def kernel(x_prompt, x_sample, norm1_w, w_in, dn_conv_w, dn_A_log_f, dn_A_log_b, dn_dt_bias_f, dn_dt_bias_b, dn_norm_w, attn_q_norm_w, attn_k_norm_w, attn_out_norm_w, w_out, norm2_w, w_ffn_in, ffn_conv_w, ffn_conv_b, w_ffn_out):
    depth = norm1_w.shape[0]

    def trunk(x):
        for l in range(depth):
            p = _prepare(norm1_w[l], w_in[l], dn_conv_w[l], dn_A_log_f[l], dn_A_log_b[l],
                         dn_dt_bias_f[l], dn_dt_bias_b[l], dn_norm_w[l], attn_q_norm_w[l],
                         attn_k_norm_w[l], attn_out_norm_w[l], w_out[l], norm2_w[l], w_ffn_in[l],
                         ffn_conv_w[l], ffn_conv_b[l], w_ffn_out[l])
            x = _layer(x, p)
        return x

    return (trunk(x_prompt), trunk(x_sample))
```

```python
import functools
import math

import jax
import jax.numpy as jnp
import numpy as np
from jax import lax
from jax.experimental import pallas as pl
from jax.experimental.pallas import tpu as pltpu

D_MODEL = 1024
GRID_W = 64
HEAD_DIM = 128
N_ATTN_HEADS = 4
N_KV_HEADS = 2
N_DN_HEADS = 4
DN_DIM = 128
DN_CONV_K = 5
CHUNK = 64
ROPE_THETA = 10000.0
D_FF = 2752
EPS = 1e-6

ATTN_Q_W = N_ATTN_HEADS * HEAD_DIM
ATTN_KV_W = N_KV_HEADS * HEAD_DIM
DN_W = N_DN_HEADS * DN_DIM
IN_W = ATTN_Q_W + 2 * ATTN_KV_W + 4 * DN_W + 4 * N_DN_HEADS

LANES = 128
SUBLANES = 8
IN_W_PAD = 3200
D_FF_PAD = 2816
GATE_OFF = ATTN_Q_W + 2 * ATTN_KV_W + 4 * DN_W
HALO = 2 * SUBLANES

VMEM_LIMIT = 48 * 1024 * 1024
VMEM_LIMIT_LARGE = 56 * 1024 * 1024

GT_G, GT_BETA, GT_EG, GT_EGL, GT_GL, GT_BEG = 0, 8, 16, 24, 32, 40
GT_ROWS = 48

SUM_ROWS = 2 * SUBLANES
ATTN_UNROLL = 4
ATTN_UNROLL_BOUNDED = 8
IN_ROW_BLOCKS = 2
FFN_ROW_BLOCKS = 4
Q_SCALE = (HEAD_DIM ** -0.5) * math.log2(math.e)
SCORE_BOUND = 60.0


def _sigmoid(x):
    return 0.5 * jnp.tanh(0.5 * x) + 0.5


def _silu(x):
    return x * _sigmoid(x)


def _softplus(x):
    return jnp.maximum(x, 0.0) + jnp.log(1.0 + jnp.exp(-jnp.abs(x)))


def _rms_rows(x, w):
    return x * lax.rsqrt(jnp.mean(x * x, axis=-1, keepdims=True) + EPS) * w


def _row_blocks(nrows, nblocks):
    tile = 2 * SUBLANES
    cuts = [-(-(nrows * b // nblocks) // tile) * tile for b in range(nblocks)] + [nrows]
    return list(zip(cuts[:-1], cuts[1:]))


def _aligned(x, m):
    return x if isinstance(x, int) else pl.multiple_of(x, m)


def _in_proj_kernel(xp_ref, x_ref, xn_ref, n1_ref, w_ref, cos_ref, sin_ref, qn_ref, kn_ref,
                    cw_ref, gp_ref,
                    aq_ref, ak_ref, avt_ref, dq_ref, dk_ref, dv_ref, dz_ref, gcol_ref, grow_ref,
                    ext_ref, he_ref, *, tm, tiles_per_seq):
    i = pl.program_id(0)
    first = (i % tiles_per_seq) == 0
    last = (i % tiles_per_seq) == tiles_per_seq - 1

    n1 = n1_ref[...]
    xp = jnp.where(first, jnp.zeros_like(xp_ref), xp_ref[...])
    xn = jnp.where(last, jnp.zeros_like(xn_ref), xn_ref[...])
    he_ref[0:HALO, :] = _rms_rows(xp, n1).astype(jnp.bfloat16)
    he_ref[HALO:HALO + tm, :] = _rms_rows(x_ref[...], n1).astype(jnp.bfloat16)
    he_ref[HALO + tm:tm + 2 * HALO, :] = _rms_rows(xn, n1).astype(jnp.bfloat16)
    h_rows = slice(HALO, HALO + tm)

    cos = cos_ref[...]
    sin = sin_ref[...]
    lane = lax.broadcasted_iota(jnp.int32, (tm, LANES), 1)
    low_half = (lane % 64) < 32

    def rope(xh):
        perm = jnp.where(low_half, pltpu.roll(xh, 96, axis=1), pltpu.roll(xh, 32, axis=1))
        return xh * cos + perm * sin

    def proj(lo, width, r0, nrows):
        return jnp.concatenate(
            [jnp.dot(he_ref[r0 + a:r0 + b, :], w_ref[:, lo:lo + width],
                     preferred_element_type=jnp.float32)
             for a, b in _row_blocks(nrows, IN_ROW_BLOCKS)], axis=0)

    pa = proj(0, ATTN_Q_W + 2 * ATTN_KV_W, HALO, tm)
    dn_off = ATTN_Q_W + 2 * ATTN_KV_W
    for part in range(3):
        ext_ref[:, part * DN_W:(part + 1) * DN_W] = proj(dn_off + part * DN_W, DN_W, 0, tm + 2 * HALO)
    pz = proj(dn_off + 3 * DN_W, DN_W, HALO, tm)
    pg = proj(GATE_OFF, LANES, HALO, tm)

    qn = qn_ref[...]
    kn = kn_ref[...]
    for hh in range(N_ATTN_HEADS):
        xh = pa[:, hh * HEAD_DIM:(hh + 1) * HEAD_DIM]
        aq_ref[:, hh * HEAD_DIM:(hh + 1) * HEAD_DIM] = (rope(_rms_rows(xh, qn)) * Q_SCALE).astype(jnp.bfloat16)
    for hh in range(N_KV_HEADS):
        o = ATTN_Q_W + hh * HEAD_DIM
        xh = pa[:, o:o + HEAD_DIM]
        ak_ref[:, hh * HEAD_DIM:(hh + 1) * HEAD_DIM] = rope(_rms_rows(xh, kn)).astype(jnp.bfloat16)
    av = pa[:, ATTN_Q_W + ATTN_KV_W:ATTN_Q_W + 2 * ATTN_KV_W]
    avt_ref[...] = av.T.astype(jnp.bfloat16)

    pad = DN_CONV_K // 2
    out_refs = (dq_ref, dk_ref, dv_ref)
    for part in range(3):
        for hh in range(N_DN_HEADS):
            c0 = part * DN_W + hh * DN_DIM
            win = ext_ref[HALO - SUBLANES:HALO + tm + SUBLANES, c0:c0 + DN_DIM]
            nwin = tm + 2 * SUBLANES
            acc = None
            for j in range(DN_CONV_K):
                tap = win if j == pad else pltpu.roll(win, (pad - j) % nwin, axis=0)
                term = tap[SUBLANES:SUBLANES + tm] * cw_ref[j:j + 1, c0:c0 + DN_DIM]
                acc = term if acc is None else acc + term
            y = _silu(acc)
            if part < 2:
                y = y * lax.rsqrt(jnp.sum(y * y, axis=-1, keepdims=True) + EPS)
            if part == 0:
                y = y * (DN_DIM ** -0.5)
            out_refs[part][:, hh * DN_DIM:(hh + 1) * DN_DIM] = y.astype(jnp.bfloat16)

    dz_ref[...] = pz.astype(jnp.bfloat16)

    a_log = gp_ref[0:1, :]
    dt_bias = gp_ref[1:2, :]
    g = -jnp.exp(a_log) * _softplus(pg + dt_bias)
    beta = _sigmoid(pg)
    rin = lax.broadcasted_iota(jnp.int32, (tm, LANES), 0) % CHUNK
    pre = g
    suf = g
    s = 1
    while s < CHUNK:
        pre = pre + jnp.where(rin >= s, pltpu.roll(pre, s, axis=0), 0.0)
        suf = suf + jnp.where(rin < CHUNK - s, pltpu.roll(suf, tm - s, axis=0), 0.0)
        s *= 2
    tot = pre + suf - g
    gcum = jnp.where(lane < N_DN_HEADS, pre, suf)
    tab = jnp.where(lane < GT_BETA, gcum, 0.0)
    tab = jnp.where(jnp.logical_and(lane >= GT_BETA, lane < GT_EG), beta, tab)
    tab = jnp.where(jnp.logical_and(lane >= GT_EG, lane < GT_EGL),
                    pltpu.roll(jnp.exp(gcum), GT_EG, axis=1), tab)
    tab = jnp.where(jnp.logical_and(lane >= GT_EGL, lane < GT_GL),
                    pltpu.roll(jnp.exp(tot - gcum), GT_EGL, axis=1), tab)
    tab = jnp.where(jnp.logical_and(lane >= GT_GL, lane < GT_GL + 8),
                    pltpu.roll(jnp.exp(tot), GT_GL, axis=1), tab)
    beg = pltpu.roll(beta, LANES - GT_BETA, axis=1) * jnp.exp(gcum)
    tab = jnp.where(jnp.logical_and(lane >= GT_BEG, lane < GT_BEG + 8),
                    pltpu.roll(beg, GT_BEG, axis=1), tab)
    gcol_ref[...] = tab
    gt = tab.T
    for c in range(tm // CHUNK):
        blk = gt[0:GT_ROWS, c * CHUNK:(c + 1) * CHUNK]
        grow_ref[c] = jnp.concatenate([blk, blk], axis=1)


def _in_proj(x2, seq_len, n1, w_in, cos, sin, qn, kn, conv_w, gate_p, *, tm):
    n = x2.shape[0]
    tiles_per_seq = seq_len // tm
    nblk8 = n // HALO
    tpb = tm // HALO
    const = lambda i: (0, 0)
    tok = lambda i: (i, 0)
    bf = jnp.bfloat16
    out_shape = (
        jax.ShapeDtypeStruct((n, ATTN_Q_W), bf),
        jax.ShapeDtypeStruct((n, ATTN_KV_W), bf),
        jax.ShapeDtypeStruct((ATTN_KV_W, n), bf),
        jax.ShapeDtypeStruct((n, DN_W), bf),
        jax.ShapeDtypeStruct((n, DN_W), bf),
        jax.ShapeDtypeStruct((n, DN_W), bf),
        jax.ShapeDtypeStruct((n, DN_W), bf),
        jax.ShapeDtypeStruct((n, LANES), jnp.float32),
        jax.ShapeDtypeStruct((n // CHUNK, GT_ROWS, LANES), jnp.float32),
    )
    out_specs = (
        pl.BlockSpec((tm, ATTN_Q_W), tok),
        pl.BlockSpec((tm, ATTN_KV_W), tok),
        pl.BlockSpec((ATTN_KV_W, tm), lambda i: (0, i)),
        pl.BlockSpec((tm, DN_W), tok),
        pl.BlockSpec((tm, DN_W), tok),
        pl.BlockSpec((tm, DN_W), tok),
        pl.BlockSpec((tm, DN_W), tok),
        pl.BlockSpec((tm, LANES), tok),
        pl.BlockSpec((tm // CHUNK, GT_ROWS, LANES), lambda i: (i, 0, 0)),
    )
    in_specs = [
        pl.BlockSpec((HALO, D_MODEL), lambda i: (jnp.maximum(i * tpb - 1, 0), 0)),
        pl.BlockSpec((tm, D_MODEL), tok),
        pl.BlockSpec((HALO, D_MODEL), lambda i: (jnp.minimum((i + 1) * tpb, nblk8 - 1), 0)),
        pl.BlockSpec((1, D_MODEL), const),
        pl.BlockSpec((D_MODEL, IN_W_PAD), const),
        pl.BlockSpec((tm, LANES), lambda i: (i % tiles_per_seq, 0)),
        pl.BlockSpec((tm, LANES), lambda i: (i % tiles_per_seq, 0)),
        pl.BlockSpec((1, HEAD_DIM), const),
        pl.BlockSpec((1, HEAD_DIM), const),
        pl.BlockSpec((DN_CONV_K, 3 * DN_W), const),
        pl.BlockSpec((SUBLANES, LANES), const),
    ]
    return pl.pallas_call(
        functools.partial(_in_proj_kernel, tm=tm, tiles_per_seq=tiles_per_seq),
        grid=(n // tm,),
        in_specs=in_specs,
        out_specs=out_specs,
        out_shape=out_shape,
        scratch_shapes=[pltpu.VMEM((tm + 2 * HALO, 3 * DN_W), jnp.float32),
                        pltpu.VMEM((tm + 2 * HALO, D_MODEL), jnp.bfloat16)],
        compiler_params=pltpu.CompilerParams(dimension_semantics=("arbitrary",),
                                             vmem_limit_bytes=VMEM_LIMIT),
        name="in_proj",
    )(x2, x2, x2, n1, w_in, cos, sin, qn, kn, conv_w, gate_p)


def _attn_kernel(q_ref, k_ref, vt_ref, on_ref, nw_ref, o_ref, s_ref, p_ref, al_ref, cm_ref, m_ref, acc_ref,
                 *, tq, tk, seq_len):
    g = N_ATTN_HEADS // N_KV_HEADS
    nqt = tq // LANES
    nkv = seq_len // tk
    items = nqt * nkv
    kv_shift = nkv.bit_length() - 1
    ones = jnp.ones((SUM_ROWS, tk), jnp.bfloat16)

    def split(t):
        if isinstance(t, int):
            return t // nkv, t % nkv
        return lax.shift_right_logical(t, kv_shift), jnp.bitwise_and(t, nkv - 1)

    def scores(t):
        qi, kj = split(t)
        r0 = _aligned(qi * LANES, LANES)
        k0 = _aligned(kj * tk, tk)
        qblk = q_ref[pl.ds(r0, LANES), :]
        qs = jnp.concatenate([qblk[:, j * HEAD_DIM:(j + 1) * HEAD_DIM] for j in range(g)], axis=0)
        kb = k_ref[pl.ds(k0, tk), :]
        return lax.dot_general(kb, qs, (((1,), (1,)), ((), ())),
                               preferred_element_type=jnp.float32)

    def stage_qk(t, slot):
        s = scores(t)
        s_ref[slot] = s
        cm_ref[slot] = jnp.max(s, axis=0, keepdims=True)

    def stage_qk_exp(t, slot):
        p_ref[slot] = jnp.exp2(scores(t)).astype(jnp.bfloat16)

    def stage_softmax(t, slot):
        _, kj = split(t)
        m_prev = jnp.where(kj == 0, -jnp.inf, m_ref[...])
        m_new = jnp.maximum(m_prev, cm_ref[slot])
        al_ref[slot] = jnp.exp2(m_prev - m_new)
        p_ref[slot] = jnp.exp2(s_ref[slot] - m_new).astype(jnp.bfloat16)
        m_ref[...] = m_new

    def stage_pv(t, slot, may_finish, bounded):
        qi, kj = split(t)
        k0 = _aligned(kj * tk, tk)
        lhs = jnp.concatenate([vt_ref[:, pl.ds(k0, tk)], ones], axis=0)
        if bounded:
            prev = jnp.where(kj == 0, 0.0, acc_ref[...])
        else:
            prev = acc_ref[...] * al_ref[slot]
        acc = prev + jnp.dot(lhs, p_ref[slot], preferred_element_type=jnp.float32)
        acc_ref[...] = acc

        def finalize():
            o = acc[0:HEAD_DIM] / acc[HEAD_DIM:HEAD_DIM + 1]
            o = o * lax.rsqrt(jnp.mean(o * o, axis=0, keepdims=True) + EPS) * on_ref[...]
            ot = o.T
            r0 = _aligned(qi * LANES, LANES)
            for j in range(g):
                o_ref[pl.ds(r0, LANES), j * HEAD_DIM:(j + 1) * HEAD_DIM] = (
                    ot[j * LANES:(j + 1) * LANES, :].astype(jnp.bfloat16))

        if not may_finish:
            return
        if isinstance(kj, int):
            if kj == nkv - 1:
                finalize()
        else:
            pl.when(kj == nkv - 1)(finalize)

    def run_body(u, bounded):
        static = isinstance(u, int)
        n = ATTN_UNROLL_BOUNDED if bounded else ATTN_UNROLL
        for j in range(n):
            may_finish = nkv == 1 or j == n - 1
            if bounded:
                if not static or u + 1 + j < items:
                    stage_qk_exp(u + 1 + j, (1 + j) % n)
            else:
                if not static or u + 2 + j < items:
                    stage_qk(u + 2 + j, (2 + j) % n)
                if not static or u + 1 + j < items:
                    stage_softmax(u + 1 + j, (1 + j) % n)
            stage_pv(u + j, j, may_finish, bounded)

    def pipeline(bounded):
        acc_ref[...] = jnp.zeros_like(acc_ref)
        if bounded:
            stage_qk_exp(0, 0)
        else:
            m_ref[...] = jnp.full(m_ref.shape, -jnp.inf, jnp.float32)
            stage_qk(0, 0)
            stage_qk(1, 1)
            stage_softmax(0, 0)

        n = ATTN_UNROLL_BOUNDED if bounded else ATTN_UNROLL

        def body(i, carry):
            run_body(pl.multiple_of(i * n, n), bounded)
            return carry

        lax.fori_loop(0, items // n - 1, body, 0)
        run_body(items - n, bounded)

    w2 = nw_ref[...] * nw_ref[...]
    bound_sq = (1.02 * (HEAD_DIM * Q_SCALE) ** 2) * jnp.max(w2[0:1]) * jnp.max(w2[1:2])
    bounded = bound_sq <= SCORE_BOUND * SCORE_BOUND
    pl.when(bounded)(lambda: pipeline(True))
    pl.when(jnp.logical_not(bounded))(lambda: pipeline(False))


def _attention(aq, ak, avt, out_norm_col, qk_norm_w, batch, seq_len, *, tq, tk):
    n = aq.shape[0]
    g = N_ATTN_HEADS // N_KV_HEADS
    nq = g * LANES
    qpb = seq_len // tq
    nkv = seq_len // tk
    items = (tq // LANES) * nkv
    assert nkv == 1 or nkv % ATTN_UNROLL_BOUNDED == 0
    assert nkv & (nkv - 1) == 0 and items % ATTN_UNROLL_BOUNDED == 0 and items >= ATTN_UNROLL_BOUNDED
    return pl.pallas_call(
        functools.partial(_attn_kernel, tq=tq, tk=tk, seq_len=seq_len),
        grid=(batch, N_KV_HEADS, qpb),
        in_specs=[
            pl.BlockSpec((tq, g * HEAD_DIM), lambda b, h, i: (b * qpb + i, h)),
            pl.BlockSpec((seq_len, HEAD_DIM), lambda b, h, i: (b, h)),
            pl.BlockSpec((HEAD_DIM, seq_len), lambda b, h, i: (h, b)),
            pl.BlockSpec((HEAD_DIM, 1), lambda b, h, i: (0, 0)),
            pl.BlockSpec((2, HEAD_DIM), lambda b, h, i: (0, 0)),
        ],
        out_specs=pl.BlockSpec((tq, g * HEAD_DIM), lambda b, h, i: (b * qpb + i, h)),
        out_shape=jax.ShapeDtypeStruct((n, ATTN_Q_W), jnp.bfloat16),
        scratch_shapes=[
            pltpu.VMEM((ATTN_UNROLL, tk, nq), jnp.float32),
            pltpu.VMEM((ATTN_UNROLL_BOUNDED, tk, nq), jnp.bfloat16),
            pltpu.VMEM((ATTN_UNROLL, 1, nq), jnp.float32),
            pltpu.VMEM((ATTN_UNROLL, 1, nq), jnp.float32),
            pltpu.VMEM((1, nq), jnp.float32),
            pltpu.VMEM((HEAD_DIM + SUM_ROWS, nq), jnp.float32),
        ],
        compiler_params=pltpu.CompilerParams(
            dimension_semantics=("arbitrary", "arbitrary", "arbitrary"),
            vmem_limit_bytes=VMEM_LIMIT_LARGE),
        name="attention",
    )(aq, ak, avt, out_norm_col, qk_norm_w)


def _dn_kernel(qf_ref, kf_ref, vf_ref, tf_ref, rf_ref, qb_ref, kb_ref, vb_ref, tb_ref, rb_ref,
               of_ref, ob_ref,
               s_ref, wq_ref, u_ref, at_ref, kdt_ref, *, nb, nchunk, prep_chunks):
    seg = pl.program_id(1)

    @pl.when(seg == 0)
    def _():
        s_ref[...] = jnp.zeros_like(s_ref)

    ri = lax.broadcasted_iota(jnp.int32, (CHUNK, CHUNK), 0)
    ci = lax.broadcasted_iota(jnp.int32, (CHUNK, CHUNK), 1)
    rw = lax.broadcasted_iota(jnp.int32, (CHUNK, 2 * CHUNK), 0)
    cw = lax.broadcasted_iota(jnp.int32, (CHUNK, 2 * CHUNK), 1)
    hi_half = cw >= CHUNK
    eye_hi = jnp.where(cw == rw + CHUNK, 1.0, 0.0)
    rk = lax.broadcasted_iota(jnp.int32, (DN_DIM, DN_DIM), 0)
    ck = lax.broadcasted_iota(jnp.int32, (DN_DIM, DN_DIM), 1)
    eye_k = jnp.where(rk == ck, 1.0, 0.0).astype(jnp.bfloat16)
    incl = (ri >= ci, ri <= ci)
    strict = (ri > ci, ri < ci)
    ins = ((qf_ref, kf_ref, vf_ref, tf_ref, rf_ref), (qb_ref, kb_ref, vb_ref, tb_ref, rb_ref))
    outs = (of_ref, ob_ref)
    bf = jnp.bfloat16
    probs = [(bb, d, hh) for bb in range(nb) for d in range(2) for hh in range(N_DN_HEADS)]

    def mm(a, b):
        return jnp.dot(a.astype(bf), b.astype(bf), preferred_element_type=jnp.float32)

    def prep(ci_, carry):
        units = []
        for pc in range(prep_chunks):
            c = ci_ * prep_chunks + pc
            r0 = pl.multiple_of(c * CHUNK, CHUNK)
            for bb, d, hh in probs:
                q_ref, k_ref, v_ref, t_ref, g_ref = ins[d]
                col = d * N_DN_HEADS + hh
                lo = hh * DN_DIM
                tab = t_ref[bb, pl.ds(r0, CHUNK), :]
                k = k_ref[bb, pl.ds(r0, CHUNK), lo:lo + DN_DIM]
                q = q_ref[bb, pl.ds(r0, CHUNK), lo:lo + DN_DIM]
                rtab = g_ref[bb, c]
                units.append(dict(
                    d=d, unit=(bb * 2 * N_DN_HEADS + col) * nchunk + c, q=q, k=k,
                    v=v_ref[bb, pl.ds(r0, CHUNK), lo:lo + DN_DIM],
                    gc=tab[:, GT_G + col:GT_G + col + 1],
                    beta=tab[:, GT_BETA + col:GT_BETA + col + 1],
                    eg=tab[:, GT_EG + col:GT_EG + col + 1],
                    gr=rtab[GT_G + col:GT_G + col + 1, 0:CHUNK],
                    beta_r=rtab[GT_BETA + col:GT_BETA + col + 1, :],
                    beg_r=rtab[GT_BEG + col:GT_BEG + col + 1, :],
                    egl_r=rtab[GT_EGL + col:GT_EGL + col + 1, 0:CHUNK]))
        for u in units:
            kq = jnp.concatenate([u["k"], u["q"]], axis=0)
            u["kkqk"] = lax.dot_general(kq, u["k"], (((1,), (1,)), ((), ())),
                                        preferred_element_type=jnp.float32)
            u["kt"] = lax.dot_general(eye_k, u["k"], (((1,), (1,)), ((), ())),
                                      preferred_element_type=jnp.float32)
        for u in units:
            d = u["d"]
            dec = jnp.exp(jnp.minimum(u["gc"] - u["gr"], 0.0))
            x = jnp.where(strict[d], -(u["beta"] * u["kkqk"][0:CHUNK]) * dec, 0.0)
            attn = jnp.where(incl[d], u["kkqk"][CHUNK:2 * CHUNK] * dec, 0.0)
            at_ref[u["unit"]] = attn.astype(bf)
            wq_ref[u["unit"], CHUNK:2 * CHUNK, :] = (u["q"].astype(jnp.float32) * u["eg"]).astype(bf)
            kdt_ref[u["unit"]] = (u["kt"] * u["egl_r"]).astype(bf)
            u["z"] = jnp.concatenate([x, jnp.zeros_like(x)], axis=1) + eye_hi
        for _ in range(6):
            for u in units:
                z = u["z"]
                u["z"] = mm(z[:, 0:CHUNK], z) + jnp.where(hi_half, z, 0.0)
        zeros = jnp.zeros((CHUNK, DN_DIM), bf)
        for u in units:
            z = u["z"]
            u_ref[u["unit"]] = jnp.dot((z * u["beta_r"]).astype(bf),
                                       jnp.concatenate([zeros, u["v"]], axis=0),
                                       preferred_element_type=jnp.float32)
            wq_ref[u["unit"], 0:CHUNK, :] = jnp.dot((z * u["beg_r"]).astype(bf),
                                                    jnp.concatenate([zeros, u["k"]], axis=0),
                                                    preferred_element_type=jnp.float32).astype(bf)
        return carry

    lax.fori_loop(0, nchunk // prep_chunks, prep, 0)

    def scan(step, carry):
        chains = []
        for bb, d, hh in probs:
            c = step if d == 0 else nchunk - 1 - step
            r0 = pl.multiple_of(c * CHUNK, CHUNK)
            col = d * N_DN_HEADS + hh
            sidx = bb * 2 * N_DN_HEADS + col
            tabrow = ins[d][3][bb, pl.ds(r0, 1), :]
            chains.append(dict(bb=bb, d=d, hh=hh, sidx=sidx, unit=sidx * nchunk + c, r0=r0,
                               gl=tabrow[:, GT_GL + col:GT_GL + col + 1]))
        for ch in chains:
            ch["st"] = s_ref[ch["sidx"]]
            ch["res"] = jnp.dot(wq_ref[ch["unit"]], ch["st"].astype(bf),
                                preferred_element_type=jnp.float32)
        for ch in chains:
            ch["vn"] = (u_ref[ch["unit"]] - ch["res"][0:CHUNK]).astype(bf)
        for ch in chains:
            ch["o"] = jnp.dot(at_ref[ch["unit"]], ch["vn"], preferred_element_type=jnp.float32)
            ch["upd"] = jnp.dot(kdt_ref[ch["unit"]], ch["vn"], preferred_element_type=jnp.float32)
        for ch in chains:
            hh = ch["hh"]
            o = ch["res"][CHUNK:2 * CHUNK] + ch["o"]
            outs[ch["d"]][ch["bb"], pl.ds(ch["r0"], CHUNK), hh * DN_DIM:(hh + 1) * DN_DIM] = o.astype(bf)
            s_ref[ch["sidx"]] = ch["st"] * ch["gl"] + ch["upd"]
        return carry

    lax.fori_loop(0, nchunk, scan, 0)


def _deltanet(dq, dk, dv, gcol, grow, batch, seq_len, *, seg, prep_chunks, nb):
    nseg = seq_len // seg
    nchunk = seg // CHUNK
    units = nb * 2 * N_DN_HEADS * nchunk
    fwd = lambda b, s: (b, s, 0)
    bwd = lambda b, s: (b, nseg - 1 - s, 0)
    fwd4 = lambda b, s: (b, s, 0, 0)
    bwd4 = lambda b, s: (b, nseg - 1 - s, 0, 0)
    big = lambda m: pl.BlockSpec((nb, seg, DN_W), m)
    tabs = lambda m: pl.BlockSpec((nb, seg, LANES), m)
    rows = lambda m: pl.BlockSpec((nb, nchunk, GT_ROWS, LANES), m)
    bf = jnp.bfloat16
    out = jax.ShapeDtypeStruct((batch, seq_len, DN_W), bf)
    return pl.pallas_call(
        functools.partial(_dn_kernel, nb=nb, nchunk=nchunk, prep_chunks=prep_chunks),
        grid=(batch // nb, nseg),
        in_specs=[big(fwd), big(fwd), big(fwd), tabs(fwd), rows(fwd4),
                  big(bwd), big(bwd), big(bwd), tabs(bwd), rows(bwd4)],
        out_specs=(big(fwd), big(bwd)),
        out_shape=(out, out),
        scratch_shapes=[
            pltpu.VMEM((nb * 2 * N_DN_HEADS, DN_DIM, DN_DIM), jnp.float32),
            pltpu.VMEM((units, 2 * CHUNK, DN_DIM), bf),
            pltpu.VMEM((units, CHUNK, DN_DIM), jnp.float32),
            pltpu.VMEM((units, CHUNK, CHUNK), bf),
            pltpu.VMEM((units, DN_DIM, CHUNK), bf),
        ],
        compiler_params=pltpu.CompilerParams(dimension_semantics=("arbitrary", "arbitrary"),
                                             vmem_limit_bytes=VMEM_LIMIT),
        name="deltanet",
    )(dq, dk, dv, gcol, grow, dq, dk, dv, gcol, grow)


def _mix_kernel(x_ref, ao_ref, of_ref, ob_ref, z_ref, dnw_ref, wo_ref, n2_ref, x1_ref, h2_ref):
    dnw = dnw_ref[...]
    parts = []
    for hh in range(N_DN_HEADS):
        sl = slice(hh * DN_DIM, (hh + 1) * DN_DIM)
        o = of_ref[:, sl].astype(jnp.float32) + ob_ref[:, sl].astype(jnp.float32)
        z = z_ref[:, sl].astype(jnp.float32)
        parts.append((_rms_rows(o, dnw) * _silu(z)).astype(jnp.bfloat16))
    dn = jnp.concatenate(parts, axis=1)
    y = jnp.dot(ao_ref[...], wo_ref[0:ATTN_Q_W, :], preferred_element_type=jnp.float32)
    y = y + jnp.dot(dn, wo_ref[ATTN_Q_W:ATTN_Q_W + DN_W, :], preferred_element_type=jnp.float32)
    x1 = x_ref[...] + y
    x1_ref[...] = x1
    h2_ref[...] = _rms_rows(x1, n2_ref[...]).astype(jnp.bfloat16)


def _mix_out(x2, ao, o_f, o_b, dz, dn_norm, w_out, n2, *, tm):
    n = x2.shape[0]
    tok = lambda i: (i, 0)
    const = lambda i: (0, 0)
    half = pl.BlockSpec((tm, DN_W), tok)
    return pl.pallas_call(
        _mix_kernel,
        grid=(n // tm,),
        in_specs=[pl.BlockSpec((tm, D_MODEL), tok), half, half, half, half,
                  pl.BlockSpec((1, DN_DIM), const),
                  pl.BlockSpec((ATTN_Q_W + DN_W, D_MODEL), const),
                  pl.BlockSpec((1, D_MODEL), const)],
        out_specs=(pl.BlockSpec((tm, D_MODEL), tok), pl.BlockSpec((tm, D_MODEL), tok)),
        out_shape=(jax.ShapeDtypeStruct((n, D_MODEL), jnp.float32),
                   jax.ShapeDtypeStruct((n, D_MODEL), jnp.bfloat16)),
        compiler_params=pltpu.CompilerParams(dimension_semantics=("arbitrary",),
                                             vmem_limit_bytes=VMEM_LIMIT),
        name="mix_out",
    )(x2, ao, o_f, o_b, dz, dn_norm, w_out, n2)


def _ffn_kernel(hp_ref, h_ref, hn_ref, x1_ref, wg_ref, wu_ref, cwg_ref, cwu_ref, bg_ref, bu_ref,
                wd_ref, y_ref, acc_ref, he_ref, ug_ref, uu_ref, act_ref, *, tm, fc, tiles_per_seq):
    i = pl.program_id(0)
    first = (i % tiles_per_seq) == 0
    last = (i % tiles_per_seq) == tiles_per_seq - 1
    rows = tm + 2 * HALO
    he_ref[0:HALO, :] = jnp.where(first, jnp.zeros_like(hp_ref), hp_ref[...])
    he_ref[HALO:HALO + tm, :] = h_ref[...]
    he_ref[HALO + tm:rows, :] = jnp.where(last, jnp.zeros_like(hn_ref), hn_ref[...])

    def cols(c):
        return c * fc if isinstance(c, int) else pl.multiple_of(c * fc, fc)

    def conv(u_ref, slot, a, b, cw_ref, b_ref, sl):
        w = u_ref[slot, a + HALO - SUBLANES:b + HALO + SUBLANES, :]
        n = b - a
        prev = pltpu.roll(w, 1, axis=0)[SUBLANES:SUBLANES + n]
        nxt = pltpu.roll(w, n + 2 * SUBLANES - 1, axis=0)[SUBLANES:SUBLANES + n]
        mid = w[SUBLANES:SUBLANES + n]
        return prev * cw_ref[0:1, sl] + mid * cw_ref[1:2, sl] + nxt * cw_ref[2:3, sl] + b_ref[0:1, sl]

    def up(c, slot):
        sl = pl.ds(cols(c), fc)
        for a, b in _row_blocks(rows, FFN_ROW_BLOCKS):
            ug_ref[slot, a:b, :] = jnp.dot(he_ref[a:b, :], wg_ref[:, sl], preferred_element_type=jnp.float32)
            uu_ref[slot, a:b, :] = jnp.dot(he_ref[a:b, :], wu_ref[:, sl], preferred_element_type=jnp.float32)

    def gate(c, slot):
        sl = pl.ds(cols(c), fc)
        for a, b in _row_blocks(tm, FFN_ROW_BLOCKS):
            act_ref[slot, a:b, :] = (_silu(conv(ug_ref, slot, a, b, cwg_ref, bg_ref, sl))
                                     * conv(uu_ref, slot, a, b, cwu_ref, bu_ref, sl)).astype(jnp.bfloat16)

    def down(c, slot):
        acc_ref[...] += jnp.dot(act_ref[slot], wd_ref[pl.ds(cols(c), fc), :],
                                preferred_element_type=jnp.float32)

    nchunks = D_FF_PAD // fc
    acc_ref[...] = jnp.zeros_like(acc_ref)
    up(0, 0)
    for c in range(nchunks):
        if c + 1 < nchunks:
            up(c + 1, (c + 1) % 2)
        gate(c, c % 2)
        down(c, c % 2)
    y_ref[...] = x1_ref[...] + acc_ref[...]


def _ffn(h2, x1, seq_len, wg, wu, cwg, cwu, bg, bu, wd, *, tm, fc):
    n = h2.shape[0]
    tiles_per_seq = seq_len // tm
    nblk = n // HALO
    tpb = tm // HALO
    tok = lambda i: (i, 0)
    const = lambda i: (0, 0)
    return pl.pallas_call(
        functools.partial(_ffn_kernel, tm=tm, fc=fc, tiles_per_seq=tiles_per_seq),
        grid=(n // tm,),
        in_specs=[
            pl.BlockSpec((HALO, D_MODEL), lambda i: (jnp.maximum(i * tpb - 1, 0), 0)),
            pl.BlockSpec((tm, D_MODEL), tok),
            pl.BlockSpec((HALO, D_MODEL), lambda i: (jnp.minimum((i + 1) * tpb, nblk - 1), 0)),
            pl.BlockSpec((tm, D_MODEL), tok),
            pl.BlockSpec((D_MODEL, D_FF_PAD), const, pipeline_mode=pl.Buffered(1)),
            pl.BlockSpec((D_MODEL, D_FF_PAD), const, pipeline_mode=pl.Buffered(1)),
            pl.BlockSpec((3, D_FF_PAD), const),
            pl.BlockSpec((3, D_FF_PAD), const),
            pl.BlockSpec((1, D_FF_PAD), const),
            pl.BlockSpec((1, D_FF_PAD), const),
            pl.BlockSpec((D_FF_PAD, D_MODEL), const, pipeline_mode=pl.Buffered(1)),
        ],
        out_specs=pl.BlockSpec((tm, D_MODEL), tok),
        out_shape=jax.ShapeDtypeStruct((n, D_MODEL), jnp.float32),
        scratch_shapes=[pltpu.VMEM((tm, D_MODEL), jnp.float32),
                        pltpu.VMEM((tm + 2 * HALO, D_MODEL), jnp.bfloat16),
                        pltpu.VMEM((2, tm + 2 * HALO, fc), jnp.float32),
                        pltpu.VMEM((2, tm + 2 * HALO, fc), jnp.float32),
                        pltpu.VMEM((2, tm, fc), jnp.bfloat16)],
        compiler_params=pltpu.CompilerParams(dimension_semantics=("arbitrary",),
                                             vmem_limit_bytes=VMEM_LIMIT_LARGE),
        name="ffn",
    )(h2, h2, h2, x1, wg, wu, cwg, cwu, bg, bu, wd)


def _rope_tables(seq_len):
    half = HEAD_DIM // 2
    f32 = np.float32
    inv = f32(ROPE_THETA) ** (-np.arange(0, half, 2, dtype=f32) / f32(half))
    t = np.arange(seq_len, dtype=np.int32)
    rows = (t // GRID_W).astype(f32)[:, None] * inv[None, :]
    cols = (t % GRID_W).astype(f32)[:, None] * inv[None, :]
    cos = np.concatenate([np.cos(rows), np.cos(rows), np.cos(cols), np.cos(cols)], axis=1)
    sin = np.concatenate([-np.sin(rows), np.sin(rows), -np.sin(cols), np.sin(cols)], axis=1)
    return jnp.asarray(cos, jnp.float32), jnp.asarray(sin, jnp.float32)


def _pad_cols(a, width):
    return jnp.pad(a, ((0, 0), (0, width - a.shape[1])))


def _prepare(norm1_w, w_in, dn_conv_w, dn_A_log_f, dn_A_log_b, dn_dt_bias_f, dn_dt_bias_b, dn_norm_w,
             attn_q_norm_w, attn_k_norm_w, attn_out_norm_w, w_out, norm2_w, w_ffn_in, ffn_conv_w,
             ffn_conv_b, w_ffn_out):
    bf = jnp.bfloat16
    gate_p = jnp.zeros((SUBLANES, LANES), jnp.float32)
    gate_p = gate_p.at[0, 0:4].set(dn_A_log_f).at[0, 4:8].set(dn_A_log_b)
    gate_p = gate_p.at[1, 0:4].set(dn_dt_bias_f).at[1, 4:8].set(dn_dt_bias_b)
    return dict(
        n1=norm1_w.reshape(1, D_MODEL),
        w_in=_pad_cols(w_in, IN_W_PAD).astype(bf),
        conv_w=dn_conv_w,
        gate_p=gate_p,
        dn_norm=dn_norm_w.reshape(1, DN_DIM),
        qn=attn_q_norm_w.reshape(1, HEAD_DIM),
        kn=attn_k_norm_w.reshape(1, HEAD_DIM),
        on=attn_out_norm_w.reshape(HEAD_DIM, 1),
        w_out=w_out.astype(bf),
        n2=norm2_w.reshape(1, D_MODEL),
        wg=_pad_cols(w_ffn_in[:, :D_FF], D_FF_PAD).astype(bf),
        wu=_pad_cols(w_ffn_in[:, D_FF:], D_FF_PAD).astype(bf),
        cwg=_pad_cols(ffn_conv_w[:, :D_FF], D_FF_PAD),
        cwu=_pad_cols(ffn_conv_w[:, D_FF:], D_FF_PAD),
        bg=_pad_cols(ffn_conv_b[None, :D_FF], D_FF_PAD),
        bu=_pad_cols(ffn_conv_b[None, D_FF:], D_FF_PAD),
        wd=jnp.pad(w_ffn_out, ((0, D_FF_PAD - D_FF), (0, 0))).astype(bf),
    )


def _tiles(seq_len):
    return dict(
        tm_in=min(512, seq_len),
        tq=min(4096, seq_len),
        tk=min(2048, seq_len),
        seg=min(256, seq_len),
        prep_batch=4,
        dn_batch=4,
        tm_mix=min(1024, seq_len),
        tm_ffn=min(1024, seq_len),
        fc=256,
    )


def _layer(x, p):
    batch, seq_len, _ = x.shape
    t = _tiles(seq_len)
    x2 = x.reshape(batch * seq_len, D_MODEL)
    cos, sin = _rope_tables(seq_len)
    aq, ak, avt, dq, dk, dv, dz, gcol, grow = _in_proj(
        x2, seq_len, p["n1"], p["w_in"], cos, sin, p["qn"], p["kn"], p["conv_w"], p["gate_p"],
        tm=t["tm_in"])
    ao = _attention(aq, ak, avt, p["on"], jnp.concatenate([p["qn"], p["kn"]], axis=0),
                    batch, seq_len, tq=t["tq"], tk=t["tk"])
    n = batch * seq_len
    nb = min(t["dn_batch"], batch)
    o_f, o_b = _deltanet(dq.reshape(batch, seq_len, DN_W), dk.reshape(batch, seq_len, DN_W),
                         dv.reshape(batch, seq_len, DN_W), gcol.reshape(batch, seq_len, LANES),
                         grow.reshape(batch, seq_len // CHUNK, GT_ROWS, LANES),
                         batch, seq_len, seg=t["seg"], prep_chunks=max(1, t["prep_batch"] // nb), nb=nb)
    o_f = o_f.reshape(n, DN_W)
    o_b = o_b.reshape(n, DN_W)
    x1, h2 = _mix_out(x2, ao, o_f, o_b, dz, p["dn_norm"], p["w_out"], p["n2"], tm=t["tm_mix"])
    y = _ffn(h2, x1, seq_len, p["wg"], p["wu"], p["cwg"], p["cwu"], p["bg"], p["bu"], p["wd"],
             tm=t["tm_ffn"], fc=t["fc"])
    return y.reshape(batch, seq_len, D_MODEL)


def kernel(x_prompt, x_sample, norm1_w, w_in, dn_conv_w, dn_A_log_f, dn_A_log_b, dn_dt_bias_f, dn_dt_bias_b, dn_norm_w, attn_q_norm_w, attn_k_norm_w, attn_out_norm_w, w_out, norm2_w, w_ffn_in, ffn_conv_w, ffn_conv_b, w_ffn_out):
    depth = norm1_w.shape[0]

    def trunk(x):
        for l in range(depth):
            p = _prepare(norm1_w[l], w_in[l], dn_conv_w[l], dn_A_log_f[l], dn_A_log_b[l],
                         dn_dt_bias_f[l], dn_dt_bias_b[l], dn_norm_w[l], attn_q_norm_w[l],
                         attn_k_norm_w[l], attn_out_norm_w[l], w_out[l], norm2_w[l], w_ffn_in[l],
                         ffn_conv_w[l], ffn_conv_b[l], w_ffn_out[l])
            x = _layer(x, p)
        return x

    return (trunk(x_prompt), trunk(x_sample))
```

```python
import functools
import math

import jax
import jax.numpy as jnp
import numpy as np
from jax import lax
from jax.experimental import pallas as pl
from jax.experimental.pallas import tpu as pltpu

D_MODEL = 1024
GRID_W = 64
HEAD_DIM = 128
N_ATTN_HEADS = 4
N_KV_HEADS = 2
N_DN_HEADS = 4
DN_DIM = 128
DN_CONV_K = 5
CHUNK = 64
ROPE_THETA = 10000.0
D_FF = 2752
EPS = 1e-6

ATTN_Q_W = N_ATTN_HEADS * HEAD_DIM
ATTN_KV_W = N_KV_HEADS * HEAD_DIM
DN_W = N_DN_HEADS * DN_DIM
IN_W = ATTN_Q_W + 2 * ATTN_KV_W + 4 * DN_W + 4 * N_DN_HEADS

LANES = 128
SUBLANES = 8
IN_W_PAD = 3200
D_FF_PAD = 2816
GATE_OFF = ATTN_Q_W + 2 * ATTN_KV_W + 4 * DN_W
HALO = 2 * SUBLANES

VMEM_LIMIT = 48 * 1024 * 1024
VMEM_LIMIT_LARGE = 56 * 1024 * 1024

GT_G, GT_BETA, GT_EG, GT_EGL, GT_GL, GT_BEG = 0, 8, 16, 24, 32, 40
GT_ROWS = 48

SUM_ROWS = 2 * SUBLANES
ATTN_UNROLL = 4
ATTN_UNROLL_BOUNDED = 8
IN_ROW_BLOCKS = 2
FFN_ROW_BLOCKS = 4
Q_SCALE = (HEAD_DIM ** -0.5) * math.log2(math.e)
SCORE_BOUND = 60.0


def _sigmoid(x):
    return 0.5 * jnp.tanh(0.5 * x) + 0.5


def _silu(x):
    return x * _sigmoid(x)


def _softplus(x):
    return jnp.maximum(x, 0.0) + jnp.log(1.0 + jnp.exp(-jnp.abs(x)))


def _rms_rows(x, w):
    return x * lax.rsqrt(jnp.mean(x * x, axis=-1, keepdims=True) + EPS) * w


def _row_blocks(nrows, nblocks):
    tile = 2 * SUBLANES
    cuts = [-(-(nrows * b // nblocks) // tile) * tile for b in range(nblocks)] + [nrows]
    return list(zip(cuts[:-1], cuts[1:]))


def _aligned(x, m):
    return x if isinstance(x, int) else pl.multiple_of(x, m)


def _in_proj_kernel(xp_ref, x_ref, xn_ref, n1_ref, w_ref, cos_ref, sin_ref, qn_ref, kn_ref,
                    cw_ref, gp_ref,
                    aq_ref, ak_ref, avt_ref, dq_ref, dk_ref, dv_ref, dz_ref, gcol_ref, grow_ref,
                    ext_ref, he_ref, *, tm, tiles_per_seq):
    i = pl.program_id(0)
    first = (i % tiles_per_seq) == 0
    last = (i % tiles_per_seq) == tiles_per_seq - 1

    n1 = n1_ref[...]
    xp = jnp.where(first, jnp.zeros_like(xp_ref), xp_ref[...])
    xn = jnp.where(last, jnp.zeros_like(xn_ref), xn_ref[...])
    he_ref[0:HALO, :] = _rms_rows(xp, n1).astype(jnp.bfloat16)
    he_ref[HALO:HALO + tm, :] = _rms_rows(x_ref[...], n1).astype(jnp.bfloat16)
    he_ref[HALO + tm:tm + 2 * HALO, :] = _rms_rows(xn, n1).astype(jnp.bfloat16)
    h_rows = slice(HALO, HALO + tm)

    cos = cos_ref[...]
    sin = sin_ref[...]
    lane = lax.broadcasted_iota(jnp.int32, (tm, LANES), 1)
    low_half = (lane % 64) < 32

    def rope(xh):
        perm = jnp.where(low_half, pltpu.roll(xh, 96, axis=1), pltpu.roll(xh, 32, axis=1))
        return xh * cos + perm * sin

    def proj(lo, width, r0, nrows):
        return jnp.concatenate(
            [jnp.dot(he_ref[r0 + a:r0 + b, :], w_ref[:, lo:lo + width],
                     preferred_element_type=jnp.float32)
             for a, b in _row_blocks(nrows, IN_ROW_BLOCKS)], axis=0)

    pa = proj(0, ATTN_Q_W + 2 * ATTN_KV_W, HALO, tm)
    dn_off = ATTN_Q_W + 2 * ATTN_KV_W
    for part in range(3):
        ext_ref[:, part * DN_W:(part + 1) * DN_W] = proj(dn_off + part * DN_W, DN_W, 0, tm + 2 * HALO)
    pz = proj(dn_off + 3 * DN_W, DN_W, HALO, tm)
    pg = proj(GATE_OFF, LANES, HALO, tm)

    qn = qn_ref[...]
    kn = kn_ref[...]
    for hh in range(N_ATTN_HEADS):
        xh = pa[:, hh * HEAD_DIM:(hh + 1) * HEAD_DIM]
        aq_ref[:, hh * HEAD_DIM:(hh + 1) * HEAD_DIM] = (rope(_rms_rows(xh, qn)) * Q_SCALE).astype(jnp.bfloat16)
    for hh in range(N_KV_HEADS):
        o = ATTN_Q_W + hh * HEAD_DIM
        xh = pa[:, o:o + HEAD_DIM]
        ak_ref[:, hh * HEAD_DIM:(hh + 1) * HEAD_DIM] = rope(_rms_rows(xh, kn)).astype(jnp.bfloat16)
    av = pa[:, ATTN_Q_W + ATTN_KV_W:ATTN_Q_W + 2 * ATTN_KV_W]
    avt_ref[...] = av.T.astype(jnp.bfloat16)

    pad = DN_CONV_K // 2
    out_refs = (dq_ref, dk_ref, dv_ref)
    for part in range(3):
        for hh in range(N_DN_HEADS):
            c0 = part * DN_W + hh * DN_DIM
            win = ext_ref[HALO - SUBLANES:HALO + tm + SUBLANES, c0:c0 + DN_DIM]
            nwin = tm + 2 * SUBLANES
            acc = None
            for j in range(DN_CONV_K):
                tap = win if j == pad else pltpu.roll(win, (pad - j) % nwin, axis=0)
                term = tap[SUBLANES:SUBLANES + tm] * cw_ref[j:j + 1, c0:c0 + DN_DIM]
                acc = term if acc is None else acc + term
            y = _silu(acc)
            if part < 2:
                y = y * lax.rsqrt(jnp.sum(y * y, axis=-1, keepdims=True) + EPS)
            if part == 0:
                y = y * (DN_DIM ** -0.5)
            out_refs[part][:, hh * DN_DIM:(hh + 1) * DN_DIM] = y.astype(jnp.bfloat16)

    dz_ref[...] = pz.astype(jnp.bfloat16)

    a_log = gp_ref[0:1, :]
    dt_bias = gp_ref[1:2, :]
    g = -jnp.exp(a_log) * _softplus(pg + dt_bias)
    beta = _sigmoid(pg)
    rin = lax.broadcasted_iota(jnp.int32, (tm, LANES), 0) % CHUNK
    pre = g
    suf = g
    s = 1
    while s < CHUNK:
        pre = pre + jnp.where(rin >= s, pltpu.roll(pre, s, axis=0), 0.0)
        suf = suf + jnp.where(rin < CHUNK - s, pltpu.roll(suf, tm - s, axis=0), 0.0)
        s *= 2
    tot = pre + suf - g
    gcum = jnp.where(lane < N_DN_HEADS, pre, suf)
    tab = jnp.where(lane < GT_BETA, gcum, 0.0)
    tab = jnp.where(jnp.logical_and(lane >= GT_BETA, lane < GT_EG), beta, tab)
    tab = jnp.where(jnp.logical_and(lane >= GT_EG, lane < GT_EGL),
                    pltpu.roll(jnp.exp(gcum), GT_EG, axis=1), tab)
    tab = jnp.where(jnp.logical_and(lane >= GT_EGL, lane < GT_GL),
                    pltpu.roll(jnp.exp(tot - gcum), GT_EGL, axis=1), tab)
    tab = jnp.where(jnp.logical_and(lane >= GT_GL, lane < GT_GL + 8),
                    pltpu.roll(jnp.exp(tot), GT_GL, axis=1), tab)
    beg = pltpu.roll(beta, LANES - GT_BETA, axis=1) * jnp.exp(gcum)
    tab = jnp.where(jnp.logical_and(lane >= GT_BEG, lane < GT_BEG + 8),
                    pltpu.roll(beg, GT_BEG, axis=1), tab)
    gcol_ref[...] = tab
    gt = tab.T
    for c in range(tm // CHUNK):
        blk = gt[0:GT_ROWS, c * CHUNK:(c + 1) * CHUNK]
        grow_ref[c] = jnp.concatenate([blk, blk], axis=1)


def _in_proj(x2, seq_len, n1, w_in, cos, sin, qn, kn, conv_w, gate_p, *, tm):
    n = x2.shape[0]
    tiles_per_seq = seq_len // tm
    nblk8 = n // HALO
    tpb = tm // HALO
    const = lambda i: (0, 0)
    tok = lambda i: (i, 0)
    bf = jnp.bfloat16
    out_shape = (
        jax.ShapeDtypeStruct((n, ATTN_Q_W), bf),
        jax.ShapeDtypeStruct((n, ATTN_KV_W), bf),
        jax.ShapeDtypeStruct((ATTN_KV_W, n), bf),
        jax.ShapeDtypeStruct((n, DN_W), bf),
        jax.ShapeDtypeStruct((n, DN_W), bf),
        jax.ShapeDtypeStruct((n, DN_W), bf),
        jax.ShapeDtypeStruct((n, DN_W), bf),
        jax.ShapeDtypeStruct((n, LANES), jnp.float32),
        jax.ShapeDtypeStruct((n // CHUNK, GT_ROWS, LANES), jnp.float32),
    )
    out_specs = (
        pl.BlockSpec((tm, ATTN_Q_W), tok),
        pl.BlockSpec((tm, ATTN_KV_W), tok),
        pl.BlockSpec((ATTN_KV_W, tm), lambda i: (0, i)),
        pl.BlockSpec((tm, DN_W), tok),
        pl.BlockSpec((tm, DN_W), tok),
        pl.BlockSpec((tm, DN_W), tok),
        pl.BlockSpec((tm, DN_W), tok),
        pl.BlockSpec((tm, LANES), tok),
        pl.BlockSpec((tm // CHUNK, GT_ROWS, LANES), lambda i: (i, 0, 0)),
    )
    in_specs = [
        pl.BlockSpec((HALO, D_MODEL), lambda i: (jnp.maximum(i * tpb - 1, 0), 0)),
        pl.BlockSpec((tm, D_MODEL), tok),
        pl.BlockSpec((HALO, D_MODEL), lambda i: (jnp.minimum((i + 1) * tpb, nblk8 - 1), 0)),
        pl.BlockSpec((1, D_MODEL), const),
        pl.BlockSpec((D_MODEL, IN_W_PAD), const),
        pl.BlockSpec((tm, LANES), lambda i: (i % tiles_per_seq, 0)),
        pl.BlockSpec((tm, LANES), lambda i: (i % tiles_per_seq, 0)),
        pl.BlockSpec((1, HEAD_DIM), const),
        pl.BlockSpec((1, HEAD_DIM), const),
        pl.BlockSpec((DN_CONV_K, 3 * DN_W), const),
        pl.BlockSpec((SUBLANES, LANES), const),
    ]
    return pl.pallas_call(
        functools.partial(_in_proj_kernel, tm=tm, tiles_per_seq=tiles_per_seq),
        grid=(n // tm,),
        in_specs=in_specs,
        out_specs=out_specs,
        out_shape=out_shape,
        scratch_shapes=[pltpu.VMEM((tm + 2 * HALO, 3 * DN_W), jnp.float32),
                        pltpu.VMEM((tm + 2 * HALO, D_MODEL), jnp.bfloat16)],
        compiler_params=pltpu.CompilerParams(dimension_semantics=("arbitrary",),
                                             vmem_limit_bytes=VMEM_LIMIT),
        name="in_proj",
    )(x2, x2, x2, n1, w_in, cos, sin, qn, kn, conv_w, gate_p)


def _attn_kernel(q_ref, k_ref, vt_ref, on_ref, nw_ref, o_ref, s_ref, p_ref, al_ref, cm_ref, m_ref, acc_ref,
                 *, tq, tk, seq_len):
    g = N_ATTN_HEADS // N_KV_HEADS
    nqt = tq // LANES
    nkv = seq_len // tk
    items = nqt * nkv
    kv_shift = nkv.bit_length() - 1
    ones = jnp.ones((SUM_ROWS, tk), jnp.bfloat16)

    def split(t):
        if isinstance(t, int):
            return t // nkv, t % nkv
        return lax.shift_right_logical(t, kv_shift), jnp.bitwise_and(t, nkv - 1)

    def scores(t):
        qi, kj = split(t)
        r0 = _aligned(qi * LANES, LANES)
        k0 = _aligned(kj * tk, tk)
        qblk = q_ref[pl.ds(r0, LANES), :]
        qs = jnp.concatenate([qblk[:, j * HEAD_DIM:(j + 1) * HEAD_DIM] for j in range(g)], axis=0)
        kb = k_ref[pl.ds(k0, tk), :]
        return lax.dot_general(kb, qs, (((1,), (1,)), ((), ())),
                               preferred_element_type=jnp.float32)

    def stage_qk(t, slot):
        s = scores(t)
        s_ref[slot] = s
        cm_ref[slot] = jnp.max(s, axis=0, keepdims=True)

    def stage_qk_exp(t, slot):
        p_ref[slot] = jnp.exp2(scores(t)).astype(jnp.bfloat16)

    def stage_softmax(t, slot):
        _, kj = split(t)
        m_prev = jnp.where(kj == 0, -jnp.inf, m_ref[...])
        m_new = jnp.maximum(m_prev, cm_ref[slot])
        al_ref[slot] = jnp.exp2(m_prev - m_new)
        p_ref[slot] = jnp.exp2(s_ref[slot] - m_new).astype(jnp.bfloat16)
        m_ref[...] = m_new

    def stage_pv(t, slot, may_finish, bounded):
        qi, kj = split(t)
        k0 = _aligned(kj * tk, tk)
        lhs = jnp.concatenate([vt_ref[:, pl.ds(k0, tk)], ones], axis=0)
        if bounded:
            prev = jnp.where(kj == 0, 0.0, acc_ref[...])
        else:
            prev = acc_ref[...] * al_ref[slot]
        acc = prev + jnp.dot(lhs, p_ref[slot], preferred_element_type=jnp.float32)
        acc_ref[...] = acc

        def finalize():
            o = acc[0:HEAD_DIM] / acc[HEAD_DIM:HEAD_DIM + 1]
            o = o * lax.rsqrt(jnp.mean(o * o, axis=0, keepdims=True) + EPS) * on_ref[...]
            ot = o.T
            r0 = _aligned(qi * LANES, LANES)
            for j in range(g):
                o_ref[pl.ds(r0, LANES), j * HEAD_DIM:(j + 1) * HEAD_DIM] = (
                    ot[j * LANES:(j + 1) * LANES, :].astype(jnp.bfloat16))

        if not may_finish:
            return
        if isinstance(kj, int):
            if kj == nkv - 1:
                finalize()
        else:
            pl.when(kj == nkv - 1)(finalize)

    def run_body(u, bounded):
        static = isinstance(u, int)
        n = ATTN_UNROLL_BOUNDED if bounded else ATTN_UNROLL
        for j in range(n):
            may_finish = nkv == 1 or j == n - 1
            if bounded:
                if not static or u + 1 + j < items:
                    stage_qk_exp(u + 1 + j, (1 + j) % n)
            else:
                if not static or u + 2 + j < items:
                    stage_qk(u + 2 + j, (2 + j) % n)
                if not static or u + 1 + j < items:
                    stage_softmax(u + 1 + j, (1 + j) % n)
            stage_pv(u + j, j, may_finish, bounded)

    def pipeline(bounded):
        acc_ref[...] = jnp.zeros_like(acc_ref)
        if bounded:
            stage_qk_exp(0, 0)
        else:
            m_ref[...] = jnp.full(m_ref.shape, -jnp.inf, jnp.float32)
            stage_qk(0, 0)
            stage_qk(1, 1)
            stage_softmax(0, 0)

        n = ATTN_UNROLL_BOUNDED if bounded else ATTN_UNROLL

        def body(i, carry):
            run_body(pl.multiple_of(i * n, n), bounded)
            return carry

        lax.fori_loop(0, items // n - 1, body, 0)
        run_body(items - n, bounded)

    w2 = nw_ref[...] * nw_ref[...]
    bound_sq = (1.02 * (HEAD_DIM * Q_SCALE) ** 2) * jnp.max(w2[0:1]) * jnp.max(w2[1:2])
    bounded = bound_sq <= SCORE_BOUND * SCORE_BOUND
    pl.when(bounded)(lambda: pipeline(True))
    pl.when(jnp.logical_not(bounded))(lambda: pipeline(False))


def _attention(aq, ak, avt, out_norm_col, qk_norm_w, batch, seq_len, *, tq, tk):
    n = aq.shape[0]
    g = N_ATTN_HEADS // N_KV_HEADS
    nq = g * LANES
    qpb = seq_len // tq
    nkv = seq_len // tk
    items = (tq // LANES) * nkv
    assert nkv == 1 or nkv % ATTN_UNROLL_BOUNDED == 0
    assert nkv & (nkv - 1) == 0 and items % ATTN_UNROLL_BOUNDED == 0 and items >= ATTN_UNROLL_BOUNDED
    return pl.pallas_call(
        functools.partial(_attn_kernel, tq=tq, tk=tk, seq_len=seq_len),
        grid=(batch, N_KV_HEADS, qpb),
        in_specs=[
            pl.BlockSpec((tq, g * HEAD_DIM), lambda b, h, i: (b * qpb + i, h)),
            pl.BlockSpec((seq_len, HEAD_DIM), lambda b, h, i: (b, h)),
            pl.BlockSpec((HEAD_DIM, seq_len), lambda b, h, i: (h, b)),
            pl.BlockSpec((HEAD_DIM, 1), lambda b, h, i: (0, 0)),
            pl.BlockSpec((2, HEAD_DIM), lambda b, h, i: (0, 0)),
        ],
        out_specs=pl.BlockSpec((tq, g * HEAD_DIM), lambda b, h, i: (b * qpb + i, h)),
        out_shape=jax.ShapeDtypeStruct((n, ATTN_Q_W), jnp.bfloat16),
        scratch_shapes=[
            pltpu.VMEM((ATTN_UNROLL, tk, nq), jnp.float32),
            pltpu.VMEM((ATTN_UNROLL_BOUNDED, tk, nq), jnp.bfloat16),
            pltpu.VMEM((ATTN_UNROLL, 1, nq), jnp.float32),
            pltpu.VMEM((ATTN_UNROLL, 1, nq), jnp.float32),
            pltpu.VMEM((1, nq), jnp.float32),
            pltpu.VMEM((HEAD_DIM + SUM_ROWS, nq), jnp.float32),
        ],
        compiler_params=pltpu.CompilerParams(
            dimension_semantics=("arbitrary", "arbitrary", "arbitrary"),
            vmem_limit_bytes=VMEM_LIMIT_LARGE),
        name="attention",
    )(aq, ak, avt, out_norm_col, qk_norm_w)


def _dn_kernel(qf_ref, kf_ref, vf_ref, tf_ref, rf_ref, qb_ref, kb_ref, vb_ref, tb_ref, rb_ref,
               of_ref, ob_ref,
               s_ref, wq_ref, u_ref, at_ref, kdt_ref, *, nb, nchunk, prep_chunks):
    seg = pl.program_id(1)

    @pl.when(seg == 0)
    def _():
        s_ref[...] = jnp.zeros_like(s_ref)

    ri = lax.broadcasted_iota(jnp.int32, (CHUNK, CHUNK), 0)
    ci = lax.broadcasted_iota(jnp.int32, (CHUNK, CHUNK), 1)
    rw = lax.broadcasted_iota(jnp.int32, (CHUNK, 2 * CHUNK), 0)
    cw = lax.broadcasted_iota(jnp.int32, (CHUNK, 2 * CHUNK), 1)
    hi_half = cw >= CHUNK
    eye_hi = jnp.where(cw == rw + CHUNK, 1.0, 0.0)
    rk = lax.broadcasted_iota(jnp.int32, (DN_DIM, DN_DIM), 0)
    ck = lax.broadcasted_iota(jnp.int32, (DN_DIM, DN_DIM), 1)
    eye_k = jnp.where(rk == ck, 1.0, 0.0).astype(jnp.bfloat16)
    incl = (ri >= ci, ri <= ci)
    strict = (ri > ci, ri < ci)
    ins = ((qf_ref, kf_ref, vf_ref, tf_ref, rf_ref), (qb_ref, kb_ref, vb_ref, tb_ref, rb_ref))
    outs = (of_ref, ob_ref)
    bf = jnp.bfloat16
    probs = [(bb, d, hh) for bb in range(nb) for d in range(2) for hh in range(N_DN_HEADS)]

    def mm(a, b):
        return jnp.dot(a.astype(bf), b.astype(bf), preferred_element_type=jnp.float32)

    def prep(ci_, carry):
        units = []
        for pc in range(prep_chunks):
            c = ci_ * prep_chunks + pc
            r0 = pl.multiple_of(c * CHUNK, CHUNK)
            for bb, d, hh in probs:
                q_ref, k_ref, v_ref, t_ref, g_ref = ins[d]
                col = d * N_DN_HEADS + hh
                lo = hh * DN_DIM
                tab = t_ref[bb, pl.ds(r0, CHUNK), :]
                k = k_ref[bb, pl.ds(r0, CHUNK), lo:lo + DN_DIM]
                q = q_ref[bb, pl.ds(r0, CHUNK), lo:lo + DN_DIM]
                rtab = g_ref[bb, c]
                units.append(dict(
                    d=d, unit=(bb * 2 * N_DN_HEADS + col) * nchunk + c, q=q, k=k,
                    v=v_ref[bb, pl.ds(r0, CHUNK), lo:lo + DN_DIM],
                    gc=tab[:, GT_G + col:GT_G + col + 1],
                    beta=tab[:, GT_BETA + col:GT_BETA + col + 1],
                    eg=tab[:, GT_EG + col:GT_EG + col + 1],
                    gr=rtab[GT_G + col:GT_G + col + 1, 0:CHUNK],
                    beta_r=rtab[GT_BETA + col:GT_BETA + col + 1, :],
                    beg_r=rtab[GT_BEG + col:GT_BEG + col + 1, :],
                    egl_r=rtab[GT_EGL + col:GT_EGL + col + 1, 0:CHUNK]))
        for u in units:
            kq = jnp.concatenate([u["k"], u["q"]], axis=0)
            u["kkqk"] = lax.dot_general(kq, u["k"], (((1,), (1,)), ((), ())),
                                        preferred_element_type=jnp.float32)
            u["kt"] = lax.dot_general(eye_k, u["k"], (((1,), (1,)), ((), ())),
                                      preferred_element_type=jnp.float32)
        for u in units:
            d = u["d"]
            dec = jnp.exp(jnp.minimum(u["gc"] - u["gr"], 0.0))
            x = jnp.where(strict[d], -(u["beta"] * u["kkqk"][0:CHUNK]) * dec, 0.0)
            attn = jnp.where(incl[d], u["kkqk"][CHUNK:2 * CHUNK] * dec, 0.0)
            at_ref[u["unit"]] = attn.astype(bf)
            wq_ref[u["unit"], CHUNK:2 * CHUNK, :] = (u["q"].astype(jnp.float32) * u["eg"]).astype(bf)
            kdt_ref[u["unit"]] = (u["kt"] * u["egl_r"]).astype(bf)
            u["z"] = jnp.concatenate([x, jnp.zeros_like(x)], axis=1) + eye_hi
        for _ in range(6):
            for u in units:
                z = u["z"]
                u["z"] = mm(z[:, 0:CHUNK], z) + jnp.where(hi_half, z, 0.0)
        zeros = jnp.zeros((CHUNK, DN_DIM), bf)
        for u in units:
            z = u["z"]
            u_ref[u["unit"]] = jnp.dot((z * u["beta_r"]).astype(bf),
                                       jnp.concatenate([zeros, u["v"]], axis=0),
                                       preferred_element_type=jnp.float32)
            wq_ref[u["unit"], 0:CHUNK, :] = jnp.dot((z * u["beg_r"]).astype(bf),
                                                    jnp.concatenate([zeros, u["k"]], axis=0),
                                                    preferred_element_type=jnp.float32).astype(bf)
        return carry

    lax.fori_loop(0, nchunk // prep_chunks, prep, 0)

    def scan(step, carry):
        chains = []
        for bb, d, hh in probs:
            c = step if d == 0 else nchunk - 1 - step
            r0 = pl.multiple_of(c * CHUNK, CHUNK)
            col = d * N_DN_HEADS + hh
            sidx = bb * 2 * N_DN_HEADS + col
            tabrow = ins[d][3][bb, pl.ds(r0, 1), :]
            chains.append(dict(bb=bb, d=d, hh=hh, sidx=sidx, unit=sidx * nchunk + c, r0=r0,
                               gl=tabrow[:, GT_GL + col:GT_GL + col + 1]))
        for ch in chains:
            ch["st"] = s_ref[ch["sidx"]]
            ch["res"] = jnp.dot(wq_ref[ch["unit"]], ch["st"].astype(bf),
                                preferred_element_type=jnp.float32)
        for ch in chains:
            ch["vn"] = (u_ref[ch["unit"]] - ch["res"][0:CHUNK]).astype(bf)
        for ch in chains:
            ch["o"] = jnp.dot(at_ref[ch["unit"]], ch["vn"], preferred_element_type=jnp.float32)
            ch["upd"] = jnp.dot(kdt_ref[ch["unit"]], ch["vn"], preferred_element_type=jnp.float32)
        for ch in chains:
            hh = ch["hh"]
            o = ch["res"][CHUNK:2 * CHUNK] + ch["o"]
            outs[ch["d"]][ch["bb"], pl.ds(ch["r0"], CHUNK), hh * DN_DIM:(hh + 1) * DN_DIM] = o.astype(bf)
            s_ref[ch["sidx"]] = ch["st"] * ch["gl"] + ch["upd"]
        return carry

    lax.fori_loop(0, nchunk, scan, 0)


def _deltanet(dq, dk, dv, gcol, grow, batch, seq_len, *, seg, prep_chunks, nb):
    nseg = seq_len // seg
    nchunk = seg // CHUNK
    units = nb * 2 * N_DN_HEADS * nchunk
    fwd = lambda b, s: (b, s, 0)
    bwd = lambda b, s: (b, nseg - 1 - s, 0)
    fwd4 = lambda b, s: (b, s, 0, 0)
    bwd4 = lambda b, s: (b, nseg - 1 - s, 0, 0)
    big = lambda m: pl.BlockSpec((nb, seg, DN_W), m)
    tabs = lambda m: pl.BlockSpec((nb, seg, LANES), m)
    rows = lambda m: pl.BlockSpec((nb, nchunk, GT_ROWS, LANES), m)
    bf = jnp.bfloat16
    out = jax.ShapeDtypeStruct((batch, seq_len, DN_W), bf)
    return pl.pallas_call(
        functools.partial(_dn_kernel, nb=nb, nchunk=nchunk, prep_chunks=prep_chunks),
        grid=(batch // nb, nseg),
        in_specs=[big(fwd), big(fwd), big(fwd), tabs(fwd), rows(fwd4),
                  big(bwd), big(bwd), big(bwd), tabs(bwd), rows(bwd4)],
        out_specs=(big(fwd), big(bwd)),
        out_shape=(out, out),
        scratch_shapes=[
            pltpu.VMEM((nb * 2 * N_DN_HEADS, DN_DIM, DN_DIM), jnp.float32),
            pltpu.VMEM((units, 2 * CHUNK, DN_DIM), bf),
            pltpu.VMEM((units, CHUNK, DN_DIM), jnp.float32),
            pltpu.VMEM((units, CHUNK, CHUNK), bf),
            pltpu.VMEM((units, DN_DIM, CHUNK), bf),
        ],
        compiler_params=pltpu.CompilerParams(dimension_semantics=("arbitrary", "arbitrary"),
                                             vmem_limit_bytes=VMEM_LIMIT),
        name="deltanet",
    )(dq, dk, dv, gcol, grow, dq, dk, dv, gcol, grow)


def _mix_kernel(x_ref, ao_ref, of_ref, ob_ref, z_ref, dnw_ref, wo_ref, x1_ref):
    dnw = dnw_ref[...]
    parts = []
    for hh in range(N_DN_HEADS):
        sl = slice(hh * DN_DIM, (hh + 1) * DN_DIM)
        o = of_ref[:, sl].astype(jnp.float32) + ob_ref[:, sl].astype(jnp.float32)
        z = z_ref[:, sl].astype(jnp.float32)
        parts.append((_rms_rows(o, dnw) * _silu(z)).astype(jnp.bfloat16))
    dn = jnp.concatenate(parts, axis=1)
    y = jnp.dot(ao_ref[...], wo_ref[0:ATTN_Q_W, :], preferred_element_type=jnp.float32)
    y = y + jnp.dot(dn, wo_ref[ATTN_Q_W:ATTN_Q_W + DN_W, :], preferred_element_type=jnp.float32)
    x1_ref[...] = x_ref[...] + y


def _mix_out(x2, ao, o_f, o_b, dz, dn_norm, w_out, *, tm):
    n = x2.shape[0]
    tok = lambda i: (i, 0)
    const = lambda i: (0, 0)
    half = pl.BlockSpec((tm, DN_W), tok)
    return pl.pallas_call(
        _mix_kernel,
        grid=(n // tm,),
        in_specs=[pl.BlockSpec((tm, D_MODEL), tok), half, half, half, half,
                  pl.BlockSpec((1, DN_DIM), const),
                  pl.BlockSpec((ATTN_Q_W + DN_W, D_MODEL), const)],
        out_specs=pl.BlockSpec((tm, D_MODEL), tok),
        out_shape=jax.ShapeDtypeStruct((n, D_MODEL), jnp.float32),
        compiler_params=pltpu.CompilerParams(dimension_semantics=("arbitrary",),
                                             vmem_limit_bytes=VMEM_LIMIT),
        name="mix_out",
    )(x2, ao, o_f, o_b, dz, dn_norm, w_out)


def _ffn_kernel(xp_ref, x1_ref, xn_ref, n2_ref, wg_ref, wu_ref, cwg_ref, cwu_ref, bg_ref, bu_ref,
                wd_ref, y_ref, acc_ref, he_ref, ug_ref, uu_ref, act_ref, *, tm, fc, tiles_per_seq):
    i = pl.program_id(0)
    first = (i % tiles_per_seq) == 0
    last = (i % tiles_per_seq) == tiles_per_seq - 1
    rows = tm + 2 * HALO
    n2 = n2_ref[...]
    xp = jnp.where(first, jnp.zeros_like(xp_ref), xp_ref[...])
    xn = jnp.where(last, jnp.zeros_like(xn_ref), xn_ref[...])
    he_ref[0:HALO, :] = _rms_rows(xp, n2).astype(jnp.bfloat16)
    he_ref[HALO:HALO + tm, :] = _rms_rows(x1_ref[...], n2).astype(jnp.bfloat16)
    he_ref[HALO + tm:rows, :] = _rms_rows(xn, n2).astype(jnp.bfloat16)

    def cols(c):
        return c * fc if isinstance(c, int) else pl.multiple_of(c * fc, fc)

    def conv(u_ref, slot, a, b, cw_ref, b_ref, sl):
        w = u_ref[slot, a + HALO - SUBLANES:b + HALO + SUBLANES, :]
        n = b - a
        prev = pltpu.roll(w, 1, axis=0)[SUBLANES:SUBLANES + n]
        nxt = pltpu.roll(w, n + 2 * SUBLANES - 1, axis=0)[SUBLANES:SUBLANES + n]
        mid = w[SUBLANES:SUBLANES + n]
        return prev * cw_ref[0:1, sl] + mid * cw_ref[1:2, sl] + nxt * cw_ref[2:3, sl] + b_ref[0:1, sl]

    def up(c, slot):
        sl = pl.ds(cols(c), fc)
        for a, b in _row_blocks(rows, FFN_ROW_BLOCKS):
            ug_ref[slot, a:b, :] = jnp.dot(he_ref[a:b, :], wg_ref[:, sl], preferred_element_type=jnp.float32)
            uu_ref[slot, a:b, :] = jnp.dot(he_ref[a:b, :], wu_ref[:, sl], preferred_element_type=jnp.float32)

    def gate(c, slot):
        sl = pl.ds(cols(c), fc)
        for a, b in _row_blocks(tm, FFN_ROW_BLOCKS):
            act_ref[slot, a:b, :] = (_silu(conv(ug_ref, slot, a, b, cwg_ref, bg_ref, sl))
                                     * conv(uu_ref, slot, a, b, cwu_ref, bu_ref, sl)).astype(jnp.bfloat16)

    def down(c, slot):
        acc_ref[...] += jnp.dot(act_ref[slot], wd_ref[pl.ds(cols(c), fc), :],
                                preferred_element_type=jnp.float32)

    nchunks = D_FF_PAD // fc
    acc_ref[...] = jnp.zeros_like(acc_ref)
    up(0, 0)
    for c in range(nchunks):
        if c + 1 < nchunks:
            up(c + 1, (c + 1) % 2)
        gate(c, c % 2)
        down(c, c % 2)
    y_ref[...] = x1_ref[...] + acc_ref[...]


def _ffn(x1, seq_len, n2, wg, wu, cwg, cwu, bg, bu, wd, *, tm, fc):
    n = x1.shape[0]
    tiles_per_seq = seq_len // tm
    nblk = n // HALO
    tpb = tm // HALO
    tok = lambda i: (i, 0)
    const = lambda i: (0, 0)
    return pl.pallas_call(
        functools.partial(_ffn_kernel, tm=tm, fc=fc, tiles_per_seq=tiles_per_seq),
        grid=(n // tm,),
        in_specs=[
            pl.BlockSpec((HALO, D_MODEL), lambda i: (jnp.maximum(i * tpb - 1, 0), 0)),
            pl.BlockSpec((tm, D_MODEL), tok),
            pl.BlockSpec((HALO, D_MODEL), lambda i: (jnp.minimum((i + 1) * tpb, nblk - 1), 0)),
            pl.BlockSpec((1, D_MODEL), const),
            pl.BlockSpec((D_MODEL, D_FF_PAD), const, pipeline_mode=pl.Buffered(1)),
            pl.BlockSpec((D_MODEL, D_FF_PAD), const, pipeline_mode=pl.Buffered(1)),
            pl.BlockSpec((3, D_FF_PAD), const),
            pl.BlockSpec((3, D_FF_PAD), const),
            pl.BlockSpec((1, D_FF_PAD), const),
            pl.BlockSpec((1, D_FF_PAD), const),
            pl.BlockSpec((D_FF_PAD, D_MODEL), const, pipeline_mode=pl.Buffered(1)),
        ],
        out_specs=pl.BlockSpec((tm, D_MODEL), tok),
        out_shape=jax.ShapeDtypeStruct((n, D_MODEL), jnp.float32),
        scratch_shapes=[pltpu.VMEM((tm, D_MODEL), jnp.float32),
                        pltpu.VMEM((tm + 2 * HALO, D_MODEL), jnp.bfloat16),
                        pltpu.VMEM((2, tm + 2 * HALO, fc), jnp.float32),
                        pltpu.VMEM((2, tm + 2 * HALO, fc), jnp.float32),
                        pltpu.VMEM((2, tm, fc), jnp.bfloat16)],
        compiler_params=pltpu.CompilerParams(dimension_semantics=("arbitrary",),
                                             vmem_limit_bytes=VMEM_LIMIT_LARGE),
        name="ffn",
    )(x1, x1, x1, n2, wg, wu, cwg, cwu, bg, bu, wd)


def _rope_tables(seq_len):
    half = HEAD_DIM // 2
    f32 = np.float32
    inv = f32(ROPE_THETA) ** (-np.arange(0, half, 2, dtype=f32) / f32(half))
    t = np.arange(seq_len, dtype=np.int32)
    rows = (t // GRID_W).astype(f32)[:, None] * inv[None, :]
    cols = (t % GRID_W).astype(f32)[:, None] * inv[None, :]
    cos = np.concatenate([np.cos(rows), np.cos(rows), np.cos(cols), np.cos(cols)], axis=1)
    sin = np.concatenate([-np.sin(rows), np.sin(rows), -np.sin(cols), np.sin(cols)], axis=1)
    return jnp.asarray(cos, jnp.float32), jnp.asarray(sin, jnp.float32)


def _pad_cols(a, width):
    return jnp.pad(a, ((0, 0), (0, width - a.shape[1])))


def _prepare(norm1_w, w_in, dn_conv_w, dn_A_log_f, dn_A_log_b, dn_dt_bias_f, dn_dt_bias_b, dn_norm_w,
             attn_q_norm_w, attn_k_norm_w, attn_out_norm_w, w_out, norm2_w, w_ffn_in, ffn_conv_w,
             ffn_conv_b, w_ffn_out):
    bf = jnp.bfloat16
    gate_p = jnp.zeros((SUBLANES, LANES), jnp.float32)
    gate_p = gate_p.at[0, 0:4].set(dn_A_log_f).at[0, 4:8].set(dn_A_log_b)
    gate_p = gate_p.at[1, 0:4].set(dn_dt_bias_f).at[1, 4:8].set(dn_dt_bias_b)
    return dict(
        n1=norm1_w.reshape(1, D_MODEL),
        w_in=_pad_cols(w_in, IN_W_PAD).astype(bf),
        conv_w=dn_conv_w,
        gate_p=gate_p,
        dn_norm=dn_norm_w.reshape(1, DN_DIM),
        qn=attn_q_norm_w.reshape(1, HEAD_DIM),
        kn=attn_k_norm_w.reshape(1, HEAD_DIM),
        on=attn_out_norm_w.reshape(HEAD_DIM, 1),
        w_out=w_out.astype(bf),
        n2=norm2_w.reshape(1, D_MODEL),
        wg=_pad_cols(w_ffn_in[:, :D_FF], D_FF_PAD).astype(bf),
        wu=_pad_cols(w_ffn_in[:, D_FF:], D_FF_PAD).astype(bf),
        cwg=_pad_cols(ffn_conv_w[:, :D_FF], D_FF_PAD),
        cwu=_pad_cols(ffn_conv_w[:, D_FF:], D_FF_PAD),
        bg=_pad_cols(ffn_conv_b[None, :D_FF], D_FF_PAD),
        bu=_pad_cols(ffn_conv_b[None, D_FF:], D_FF_PAD),
        wd=jnp.pad(w_ffn_out, ((0, D_FF_PAD - D_FF), (0, 0))).astype(bf),
    )


def _tiles(seq_len):
    return dict(
        tm_in=min(512, seq_len),
        tq=min(4096, seq_len),
        tk=min(2048, seq_len),
        seg=min(256, seq_len),
        prep_batch=4,
        dn_batch=4,
        tm_mix=min(1024, seq_len),
        tm_ffn=min(1024, seq_len),
        fc=256,
    )


def _layer(x, p):
    batch, seq_len, _ = x.shape
    t = _tiles(seq_len)
    x2 = x.reshape(batch * seq_len, D_MODEL)
    cos, sin = _rope_tables(seq_len)
    aq, ak, avt, dq, dk, dv, dz, gcol, grow = _in_proj(
        x2, seq_len, p["n1"], p["w_in"], cos, sin, p["qn"], p["kn"], p["conv_w"], p["gate_p"],
        tm=t["tm_in"])
    ao = _attention(aq, ak, avt, p["on"], jnp.concatenate([p["qn"], p["kn"]], axis=0),
                    batch, seq_len, tq=t["tq"], tk=t["tk"])
    n = batch * seq_len
    nb = min(t["dn_batch"], batch)
    o_f, o_b = _deltanet(dq.reshape(batch, seq_len, DN_W), dk.reshape(batch, seq_len, DN_W),
                         dv.reshape(batch, seq_len, DN_W), gcol.reshape(batch, seq_len, LANES),
                         grow.reshape(batch, seq_len // CHUNK, GT_ROWS, LANES),
                         batch, seq_len, seg=t["seg"], prep_chunks=max(1, t["prep_batch"] // nb), nb=nb)
    o_f = o_f.reshape(n, DN_W)
    o_b = o_b.reshape(n, DN_W)
    x1 = _mix_out(x2, ao, o_f, o_b, dz, p["dn_norm"], p["w_out"], tm=t["tm_mix"])
    y = _ffn(x1, seq_len, p["n2"], p["wg"], p["wu"], p["cwg"], p["cwu"], p["bg"], p["bu"], p["wd"],
             tm=t["tm_ffn"], fc=t["fc"])
    return y.reshape(batch, seq_len, D_MODEL)


def kernel(x_prompt, x_sample, norm1_w, w_in, dn_conv_w, dn_A_log_f, dn_A_log_b, dn_dt_bias_f, dn_dt_bias_b, dn_norm_w, attn_q_norm_w, attn_k_norm_w, attn_out_norm_w, w_out, norm2_w, w_ffn_in, ffn_conv_w, ffn_conv_b, w_ffn_out):
    depth = norm1_w.shape[0]

    def trunk(x):
        for l in range(depth):
            p = _prepare(norm1_w[l], w_in[l], dn_conv_w[l], dn_A_log_f[l], dn_A_log_b[l],
                         dn_dt_bias_f[l], dn_dt_bias_b[l], dn_norm_w[l], attn_q_norm_w[l],
                         attn_k_norm_w[l], attn_out_norm_w[l], w_out[l], norm2_w[l], w_ffn_in[l],
                         ffn_conv_w[l], ffn_conv_b[l], w_ffn_out[l])
            x = _layer(x, p)
        return x

    return (trunk(x_prompt), trunk(x_sample))
```
